```python
import math
import jax
import jax.numpy as jnp
from jax import lax
import numpy as np

D_MODEL = 1024
BATCH = 16
SEQ = 2048
DEPTH = 4

GRID_W = 64
CTX_LEN = 256
N_BRANCH = 4
BRANCH_W = D_MODEL // N_BRANCH
HEAD_DIM = 64
N_HEADS = BRANCH_W // HEAD_DIM
FOURIER_GROUPS = N_HEADS
CHUNK = 64
CONV_K = 5
SSD_STATE = 64
SSD_GROUPS = 2
N_EXPERTS = 16
EXPERT_FF = 1024
EC_CAPACITY_FACTOR = 2
DEEPNORM_ALPHA = (2.0 * DEPTH) ** 0.25
DEEPNORM_BETA = (8.0 * DEPTH) ** -0.25
NORM_EPS = 1e-6

IN_SPLITS = (
    ("a_q", BRANCH_W), ("a_f_fwd", BRANCH_W), ("a_f_bwd", BRANCH_W), ("a_v", BRANCH_W), ("a_g", BRANCH_W),
    ("b_qkv", 3 * BRANCH_W), ("b_g", BRANCH_W), ("b_a", 2 * N_HEADS), ("b_beta", 2 * N_HEADS),
    ("c_u", BRANCH_W),
    ("d_xbc", BRANCH_W + 2 * SSD_GROUPS * SSD_STATE), ("d_z", BRANCH_W), ("d_dt", 2 * N_HEADS),
)
IN_COLS = sum(s for _, s in IN_SPLITS)

kernel_name = "hybrid_bidir_diffusion_trunk"


def _layer_norm(x):
    xf = x.astype(jnp.float32)
    mu = jnp.mean(xf, axis=-1, keepdims=True)
    var = jnp.mean(jnp.square(xf - mu), axis=-1, keepdims=True)
    return (xf - mu) * lax.rsqrt(var + NORM_EPS)


def post_norm(x, g, b):
    return (_layer_norm(x) * g + b).astype(x.dtype)


def modulate(x, shift, scale):
    return (_layer_norm(x) * (1.0 + scale) + shift).astype(x.dtype)


def _rms(x):
    return x * lax.rsqrt(jnp.mean(jnp.square(x), axis=-1, keepdims=True) + NORM_EPS)


def to_heads(x):
    b_, t_, w_ = x.shape
    return x.reshape(b_, t_, N_HEADS, w_ // N_HEADS).transpose(0, 2, 1, 3)


def from_heads(x):
    b_, h_, t_, d_ = x.shape
    return x.transpose(0, 2, 1, 3).reshape(b_, t_, h_ * d_)


def head_rms_norm(o, w):
    return from_heads(_rms(o)) * w


def l2norm(x):
    return x * lax.rsqrt(jnp.sum(jnp.square(x), axis=-1, keepdims=True) + NORM_EPS)


def split_columns(z):
    sizes = [s for _, s in IN_SPLITS]
    parts = jnp.split(z, np.cumsum(sizes)[:-1].tolist(), axis=-1)
    return {name: p for (name, _), p in zip(IN_SPLITS, parts)}


def conv_centred(x, w):
    pad = CONV_K // 2
    return lax.conv_general_dilated(x, w[:, None, :], window_strides=(1,), padding=[(pad, pad)],
                                    dimension_numbers=("NWC", "WIO", "NWC"),
                                    feature_group_count=x.shape[-1])


def sincos_grid(rows, cols, dim):
    quarter = dim // 4
    omega = 1.0 / (10000.0 ** (jnp.arange(quarter, dtype=jnp.float32) / quarter))
    er = jnp.arange(rows, dtype=jnp.float32)[:, None] * omega
    ec = jnp.arange(cols, dtype=jnp.float32)[:, None] * omega
    er = jnp.concatenate([jnp.sin(er), jnp.cos(er)], axis=-1)
    ec = jnp.concatenate([jnp.sin(ec), jnp.cos(ec)], axis=-1)
    emb = jnp.concatenate([jnp.broadcast_to(er[:, None, :], (rows, cols, dim // 2)),
                           jnp.broadcast_to(ec[None, :, :], (rows, cols, dim // 2))], axis=-1)
    return emb.reshape(rows * cols, dim)


def _chunk(x):
    return x.reshape(x.shape[0], x.shape[1], x.shape[2] // CHUNK, CHUNK, *x.shape[3:])


def _tri():
    return jnp.tril(jnp.ones((CHUNK, CHUNK), dtype=bool))


def gla_vector_scan(q, k, v, logf, s0):
    tri = _tri()
    xs = tuple(jnp.moveaxis(_chunk(a), 2, 0) for a in (q, k, v, logf))

    def step(S, inp):
        qc, kc, vc, lf = inp
        g = jnp.cumsum(lf, axis=2)
        rel = jnp.where(tri[:, :, None], g[:, :, :, None, :] - g[:, :, None, :, :], -jnp.inf)
        att = jnp.einsum("bhid,bhijd,bhjd->bhij", qc, jnp.exp(rel), kc)
        g_last = g[:, :, -1:, :]
        o = att @ vc + (qc * jnp.exp(g)) @ S
        S = jnp.exp(g_last[:, :, 0, :, None]) * S + jnp.einsum("bhjd,bhje->bhde", kc * jnp.exp(g_last - g), vc)
        return S, o

    S, o = lax.scan(step, s0, xs)
    return jnp.moveaxis(o, 0, 2).reshape(v.shape), S


def ssd_scan(q, k, v, loga, s0):
    tri = _tri()
    qc, kc, vc, la = _chunk(q), _chunk(k), _chunk(v), _chunk(loga)
    g = jnp.cumsum(la, axis=-1)
    L = jnp.exp(jnp.where(tri, g[..., :, None] - g[..., None, :], -jnp.inf))
    o_intra = jnp.einsum("bhnij,bhnje->bhnie", jnp.einsum("bhnid,bhnjd->bhnij", qc, kc) * L, vc)
    g_last = g[..., -1]
    dS = jnp.einsum("bhnjd,bhnje->bhnde", kc * jnp.exp(g_last[..., None] - g)[..., None], vc)

    def step(S, inp):
        dS_c, gl_c = inp
        return jnp.exp(gl_c)[..., None, None] * S + dS_c, S

    S_T, S_in = lax.scan(step, s0, (jnp.moveaxis(dS, 2, 0), jnp.moveaxis(g_last, 2, 0)))
    S_in = jnp.moveaxis(S_in, 0, 2)
    o = o_intra + jnp.einsum("bhnid,bhnde->bhnie", qc * jnp.exp(g)[..., None], S_in)
    return o.reshape(v.shape), S_T


def gated_delta_scan(q, k, v, loga, beta, s0):
    tri = _tri()
    strict = tri & ~jnp.eye(CHUNK, dtype=bool)
    qc, kc, vc, la, bt = _chunk(q), _chunk(k), _chunk(v), _chunk(loga), _chunk(beta)
    g = jnp.cumsum(la, axis=-1)
    L = jnp.exp(jnp.where(tri, g[..., :, None] - g[..., None, :], -jnp.inf))
    a_mat = jnp.where(strict, bt[..., :, None] * jnp.einsum("bhnid,bhnjd->bhnij", kc, kc) * L, 0.0)
    t_mat = a_mat + jnp.eye(CHUNK, dtype=a_mat.dtype)
    u = lax.linalg.triangular_solve(t_mat, bt[..., None] * vc, left_side=True, lower=True, unit_diagonal=True)
    w = lax.linalg.triangular_solve(t_mat, bt[..., None] * kc * jnp.exp(g)[..., None],
                                    left_side=True, lower=True, unit_diagonal=True)
    qk = jnp.einsum("bhnid,bhnjd->bhnij", qc, kc) * L
    g_last = g[..., -1]
    q_dec = qc * jnp.exp(g)[..., None]
    k_dec = kc * jnp.exp(g_last[..., None] - g)[..., None]
    xs = tuple(jnp.moveaxis(a, 2, 0) for a in (u, w, qk, q_dec, k_dec, g_last))

    def step(S, inp):
        u_c, w_c, qk_c, qd_c, kd_c, gl_c = inp
        v_new = u_c - w_c @ S
        o_c = qd_c @ S + qk_c @ v_new
        S = jnp.exp(gl_c)[..., None, None] * S + jnp.einsum("bhjd,bhje->bhde", kd_c, v_new)
        return S, o_c

    S, o = lax.scan(step, s0, xs)
    return jnp.moveaxis(o, 0, 2).reshape(v.shape), S


def bidirectional(scan_fn, ctx_fwd, lat_fwd, ctx_bwd, lat_bwd, s0):
    def flip(t):
        return tuple(jnp.flip(a, axis=2) for a in t)
    oc_f, sc_f = scan_fn(*ctx_fwd, s0)
    ol_f, _ = scan_fn(*lat_fwd, sc_f)
    oc_b, sc_b = scan_fn(*flip(ctx_bwd), s0)
    ol_b, _ = scan_fn(*flip(lat_bwd), sc_b)
    return oc_f + jnp.flip(oc_b, axis=2), ol_f + jnp.flip(ol_b, axis=2)


def hgrn_lower_bounds(logits):
    cum = jnp.cumsum(jax.nn.softmax(logits.astype(jnp.float32), axis=1), axis=1)
    return cum - cum[:, :1]


def hgrn2_mixer(zc, zl, lb, norm_w):
    def inputs(z, d):
        zf = z[("a_f_fwd", "a_f_bwd")[d]].astype(jnp.float32)
        lbd = lb[d]
        logf = jnp.logaddexp(jnp.log(lbd), jnp.log1p(-lbd) + jax.nn.log_sigmoid(zf))
        inp = (1.0 - lbd) * jax.nn.sigmoid(-zf)
        q = jax.nn.silu(z["a_q"].astype(jnp.float32))
        v = z["a_v"].astype(jnp.float32)
        return (to_heads(q), to_heads(inp), to_heads(v), to_heads(logf))

    b_ = zc["a_q"].shape[0]
    s0 = jnp.zeros((b_, N_HEADS, HEAD_DIM, HEAD_DIM), jnp.float32)
    oc, ol = bidirectional(gla_vector_scan, inputs(zc, 0), inputs(zl, 0), inputs(zc, 1), inputs(zl, 1), s0)

    def finish(o, z):
        return head_rms_norm(o, norm_w) * jax.nn.silu(z["a_g"].astype(jnp.float32))
    return finish(oc, zc), finish(ol, zl)


def gated_deltanet_mixer(zc, zl, conv_w, a_log, dt_bias, norm_w):
    def inputs(z):
        qkv = jax.nn.silu(conv_centred(z["b_qkv"], conv_w)).astype(jnp.float32)
        q, k, v = jnp.split(qkv, 3, axis=-1)
        q = l2norm(to_heads(q)) * HEAD_DIM ** -0.5
        k = l2norm(to_heads(k))
        v = to_heads(v)
        a = z["b_a"].astype(jnp.float32)
        bb = z["b_beta"].astype(jnp.float32)
        dirs = []
        for d in range(2):
            sl = slice(d * N_HEADS, (d + 1) * N_HEADS)
            loga = -jnp.exp(a_log[d].astype(jnp.float32)) * jax.nn.softplus(a[..., sl] + dt_bias[d])
            beta = jax.nn.sigmoid(bb[..., sl])
            dirs.append((q, k, v, jnp.swapaxes(loga, 1, 2), jnp.swapaxes(beta, 1, 2)))
        return dirs

    ctx_in, lat_in = inputs(zc), inputs(zl)
    b_ = zc["b_g"].shape[0]
    s0 = jnp.zeros((b_, N_HEADS, HEAD_DIM, HEAD_DIM), jnp.float32)
    oc, ol = bidirectional(gated_delta_scan, ctx_in[0], lat_in[0], ctx_in[1], lat_in[1], s0)

    def finish(o, z):
        return head_rms_norm(o, norm_w) * jax.nn.silu(z["b_g"].astype(jnp.float32))
    return finish(oc, zc), finish(ol, zl)


def fourier_mixer(u):
    b_, t_, _ = u.shape
    ug = u.astype(jnp.float32).reshape(b_, t_, FOURIER_GROUPS, BRANCH_W // FOURIER_GROUPS)
    return jnp.fft.fft2(ug, axes=(1, 3), norm="ortho").real.reshape(b_, t_, BRANCH_W)


def ssd_mixer(zc, zl, conv_w, conv_b, a_log, dt_bias, d_skip, norm_w):
    def group_heads(t):
        b_, t_, _ = t.shape
        t = jnp.repeat(t.reshape(b_, t_, SSD_GROUPS, SSD_STATE), N_HEADS // SSD_GROUPS, axis=2)
        return t.transpose(0, 2, 1, 3)

    def inputs(z):
        xbc = jax.nn.silu(conv_centred(z["d_xbc"], conv_w) + conv_b).astype(jnp.float32)
        xs, bs, cs = jnp.split(xbc, [BRANCH_W, BRANCH_W + SSD_GROUPS * SSD_STATE], axis=-1)
        xh, bh, ch = to_heads(xs), group_heads(bs), group_heads(cs)
        dt_raw = z["d_dt"].astype(jnp.float32)
        dirs = []
        for d in range(2):
            dt = jax.nn.softplus(dt_raw[..., d * N_HEADS:(d + 1) * N_HEADS] + dt_bias[d])
            dt = jnp.swapaxes(dt, 1, 2)
            loga = -jnp.exp(a_log[d].astype(jnp.float32))[:, None] * dt
            dirs.append((ch, bh, xh * dt[..., None], loga))
        return xh, dirs

    xh_c, ctx_in = inputs(zc)
    xh_l, lat_in = inputs(zl)
    b_ = xh_c.shape[0]
    s0 = jnp.zeros((b_, N_HEADS, SSD_STATE, HEAD_DIM), jnp.float32)
    oc, ol = bidirectional(ssd_scan, ctx_in[0], lat_in[0], ctx_in[1], lat_in[1], s0)

    def finish(o, xh, z):
        y = from_heads(o + d_skip[:, None, None] * xh) * jax.nn.silu(z["d_z"].astype(jnp.float32))
        return _rms(y) * norm_w
    return finish(oc, xh_c, zc), finish(ol, xh_l, zl)


def merge_branches(h, outs, w_gate, b_gate, w_branch, w_out):
    t = [jax.nn.sigmoid(h @ w_gate[g] + b_gate[g]) * (outs[g].astype(h.dtype) @ w_branch[g])
         for g in range(N_BRANCH)]
    return (t[0] + t[1] + t[2] + t[3]) @ w_out


def token_mixer(hc, hl, w_in, lb, hgrn_norm_w, gdn_conv_w, gdn_a_log, gdn_dt_bias, gdn_norm_w,
                ssd_conv_w, ssd_conv_b, ssd_a_log, ssd_dt_bias, ssd_d, ssd_norm_w,
                w_gate, b_gate, w_branch, w_out, with_ctx):
    zc = split_columns(hc @ w_in)
    zl = split_columns(hl @ w_in)
    a_c, a_l = hgrn2_mixer(zc, zl, lb, hgrn_norm_w)
    b_c, b_l = gated_deltanet_mixer(zc, zl, gdn_conv_w, gdn_a_log, gdn_dt_bias, gdn_norm_w)
    d_c, d_l = ssd_mixer(zc, zl, ssd_conv_w, ssd_conv_b, ssd_a_log, ssd_dt_bias, ssd_d, ssd_norm_w)
    y_l = merge_branches(hl, (a_l, b_l, fourier_mixer(zl["c_u"]), d_l), w_gate, b_gate, w_branch, w_out)
    y_c = None
    if with_ctx:
        y_c = merge_branches(hc, (a_c, b_c, fourier_mixer(zc["c_u"]), d_c), w_gate, b_gate, w_branch, w_out)
    return y_c, y_l


def expert_choice_ffn(h, w_router, w_ff_gate, w_ff_up, w_ff_down):
    b_, t_, _ = h.shape
    cap = EC_CAPACITY_FACTOR * t_ // N_EXPERTS
    aff = jax.nn.softmax(jnp.einsum("btd,de->bte", h, w_router).astype(jnp.float32), axis=-1)
    weight, idx = lax.top_k(jnp.swapaxes(aff, 1, 2), cap)
    bidx = jnp.arange(b_)[:, None, None]
    xe = h[bidx, idx]
    hid = jax.nn.silu(jnp.einsum("becd,edf->becf", xe, w_ff_gate)) * jnp.einsum("becd,edf->becf", xe, w_ff_up)
    ye = jnp.einsum("becf,efd->becd", hid, w_ff_down) * weight[..., None].astype(h.dtype)
    return jnp.zeros_like(h).at[bidx, idx].add(ye)


def setup_inputs(seed: int = 0) -> dict:
    key = jax.random.key(seed)
    ks = iter(jax.random.split(key, 48))
    f32 = jnp.float32
    L, D, W, H, E, F = DEPTH, D_MODEL, BRANCH_W, N_HEADS, N_EXPERTS, EXPERT_FF
    XBC = W + 2 * SSD_GROUPS * SSD_STATE

    def nrm(shape, scale):
        return jax.random.normal(next(ks), shape, f32) * scale

    def gain(shape):
        return 1.0 + nrm(shape, 0.02)

    def a_log_init(shape):
        return jnp.log(jax.random.uniform(next(ks), shape, f32, 1.0, 16.0))

    def dt_bias_init(shape):
        dt = jnp.exp(jax.random.uniform(next(ks), shape, f32, math.log(1e-3), math.log(1e-1)))
        return dt + jnp.log(-jnp.expm1(-dt))

    return {
        "x": nrm((BATCH, SEQ, D), 1.0),
        "c": nrm((BATCH, D), 1.0),
        "ctx": nrm((BATCH, CTX_LEN, D), 1.0),
        "c_ctx": nrm((D,), 1.0),
        "ada_w": nrm((L, D, 6 * D), 0.5 * D ** -0.5),
        "ada_b": nrm((L, 6 * D), 0.02),
        "w_in": nrm((L, D, IN_COLS), D ** -0.5),
        "hgrn_lb_logits": nrm((2, L, W), 0.5),
        "hgrn_norm_w": gain((L, W)),
        "gdn_conv_w": nrm((L, CONV_K, 3 * W), CONV_K ** -0.5),
        "gdn_a_log": a_log_init((L, 2, H)),
        "gdn_dt_bias": dt_bias_init((L, 2, H)),
        "gdn_norm_w": gain((L, W)),
        "ssd_conv_w": nrm((L, CONV_K, XBC), CONV_K ** -0.5),
        "ssd_conv_b": nrm((L, XBC), 0.02),
        "ssd_a_log": a_log_init((L, 2, H)),
        "ssd_dt_bias": dt_bias_init((L, 2, H)),
        "ssd_d": gain((L, H)),
        "ssd_norm_w": gain((L, W)),
        "w_gate": nrm((L, N_BRANCH, D, D), D ** -0.5),
        "b_gate": nrm((L, N_BRANCH, D), 0.02),
        "w_branch": nrm((L, N_BRANCH, W, D), W ** -0.5),
        "w_out": nrm((L, D, D), DEEPNORM_BETA * D ** -0.5),
        "ln1_g": gain((L, D)),
        "ln1_b": nrm((L, D), 0.02),
        "w_router": nrm((L, D, E), D ** -0.5),
        "w_ff_gate": nrm((L, E, D, F), D ** -0.5),
        "w_ff_up": nrm((L, E, D, F), D ** -0.5),
        "w_ff_down": nrm((L, E, F, D), DEEPNORM_BETA * F ** -0.5),
        "ln2_g": gain((L, D)),
        "ln2_b": nrm((L, D), 0.02),
    }


def reference(x, c, ctx, c_ctx, ada_w, ada_b, w_in, hgrn_lb_logits, hgrn_norm_w,
              gdn_conv_w, gdn_a_log, gdn_dt_bias, gdn_norm_w,
              ssd_conv_w, ssd_conv_b, ssd_a_log, ssd_dt_bias, ssd_d, ssd_norm_w,
              w_gate, b_gate, w_branch, w_out, ln1_g, ln1_b,
              w_router, w_ff_gate, w_ff_up, w_ff_down, ln2_g, ln2_b):
    n_lat = x.shape[1]
    rows = n_lat // GRID_W
    xl = x + sincos_grid(rows, GRID_W, D_MODEL).astype(x.dtype)
    xc = ctx
    lb_all = hgrn_lower_bounds(hgrn_lb_logits)
    for l in range(DEPTH):
        with_ctx = l < DEPTH - 1
        mod_l = jax.nn.silu(c) @ ada_w[l] + ada_b[l]
        mod_c = jax.nn.silu(c_ctx) @ ada_w[l] + ada_b[l]
        sh1_l, sc1_l, g1_l, sh2_l, sc2_l, g2_l = jnp.split(mod_l[:, None, :], 6, axis=-1)
        sh1_c, sc1_c, g1_c, sh2_c, sc2_c, g2_c = jnp.split(mod_c, 6, axis=-1)
        hl = modulate(xl, sh1_l, sc1_l)
        hc = modulate(xc, sh1_c, sc1_c)
        y_c, y_l = token_mixer(hc, hl, w_in[l], lb_all[:, l], hgrn_norm_w[l],
                               gdn_conv_w[l], gdn_a_log[l], gdn_dt_bias[l], gdn_norm_w[l],
                               ssd_conv_w[l], ssd_conv_b[l], ssd_a_log[l], ssd_dt_bias[l], ssd_d[l], ssd_norm_w[l],
                               w_gate[l], b_gate[l], w_branch[l], w_out[l], with_ctx)
        xl = post_norm(DEEPNORM_ALPHA * xl + g1_l * y_l, ln1_g[l], ln1_b[l])
        hl2 = modulate(xl, sh2_l, sc2_l)
        xl = post_norm(DEEPNORM_ALPHA * xl + g2_l * expert_choice_ffn(hl2, w_router[l], w_ff_gate[l], w_ff_up[l], w_ff_down[l]),
                       ln2_g[l], ln2_b[l])
        if with_ctx:
            xc = post_norm(DEEPNORM_ALPHA * xc + g1_c * y_c, ln1_g[l], ln1_b[l])
            hc2 = modulate(xc, sh2_c, sc2_c)
            xc = post_norm(DEEPNORM_ALPHA * xc + g2_c * expert_choice_ffn(hc2, w_router[l], w_ff_gate[l], w_ff_up[l], w_ff_down[l]),
                           ln2_g[l], ln2_b[l])
    return xl
```

```python
import functools
import math

import jax
import jax.numpy as jnp
import numpy as np
from jax import lax
from jax.experimental import pallas as pl
from jax.experimental.pallas import tpu as pltpu

D_MODEL = 1024
DEPTH = 4
GRID_W = 64
N_BRANCH = 4
BRANCH_W = D_MODEL // N_BRANCH
HEAD_DIM = 64
N_HEADS = BRANCH_W // HEAD_DIM
FOURIER_GROUPS = N_HEADS
CHUNK = 64
CONV_K = 5
SSD_STATE = 64
SSD_GROUPS = 2
N_EXPERTS = 16
EXPERT_FF = 1024
EC_CAPACITY_FACTOR = 2
DEEPNORM_ALPHA = (2.0 * DEPTH) ** 0.25
NORM_EPS = 1e-6

IN_SPLITS = (
    ("a_q", BRANCH_W), ("a_f_fwd", BRANCH_W), ("a_f_bwd", BRANCH_W), ("a_v", BRANCH_W), ("a_g", BRANCH_W),
    ("b_qkv", 3 * BRANCH_W), ("b_g", BRANCH_W), ("b_a", 2 * N_HEADS), ("b_beta", 2 * N_HEADS),
    ("c_u", BRANCH_W),
    ("d_xbc", BRANCH_W + 2 * SSD_GROUPS * SSD_STATE), ("d_z", BRANCH_W), ("d_dt", 2 * N_HEADS),
)
IN_COLS = sum(s for _, s in IN_SPLITS)
LANE = 128
IN_COLS_PAD = -(-IN_COLS // LANE) * LANE
MXU_ROWS = 256
VMEM_LIMIT = 48 * 1024 * 1024

F32 = jnp.float32
BF16 = jnp.bfloat16


def _const_spec(shape):
    nd = len(shape)
    return pl.BlockSpec(shape, lambda *_: (0,) * nd, pipeline_mode=pl.Buffered(1))


def _proj_body(a_ref, w_ref, o_ref):
    o_ref[...] = jnp.dot(a_ref[...].astype(BF16), w_ref[...], preferred_element_type=F32)


def project(a, w):
    m, k = a.shape
    n = w.shape[1]
    tm = MXU_ROWS
    return pl.pallas_call(
        _proj_body,
        grid=(m // tm,),
        in_specs=[pl.BlockSpec((tm, k), lambda i: (i, 0)), _const_spec((k, n))],
        out_specs=pl.BlockSpec((tm, n), lambda i: (i, 0)),
        out_shape=jax.ShapeDtypeStruct((m, n), F32),
        compiler_params=pltpu.CompilerParams(dimension_semantics=("parallel",), vmem_limit_bytes=VMEM_LIMIT),
        name="project",
    )(a, w)


def _merge_body(h_ref, oa_ref, ob_ref, oc_ref, od_ref, wg_ref, bg_ref, wb_ref, wo_ref, y_ref):
    h = h_ref[...].astype(BF16)
    acc = None
    for g, o_ref in enumerate((oa_ref, ob_ref, oc_ref, od_ref)):
        gate = jax.nn.sigmoid(jnp.dot(h, wg_ref[g], preferred_element_type=F32) + bg_ref[g])
        t = gate * jnp.dot(o_ref[...].astype(BF16), wb_ref[g], preferred_element_type=F32)
        acc = t if acc is None else acc + t
    y_ref[...] = jnp.dot(acc.astype(BF16), wo_ref[...], preferred_element_type=F32)


def merge_branches(h, outs, w_gate, b_gate, w_branch, w_out):
    m, d = h.shape
    w = outs[0].shape[1]
    tm = MXU_ROWS
    row = lambda width: pl.BlockSpec((tm, width), lambda i: (i, 0))
    return pl.pallas_call(
        _merge_body,
        grid=(m // tm,),
        in_specs=[row(d), row(w), row(w), row(w), row(w),
                  _const_spec(w_gate.shape), _const_spec(b_gate.shape),
                  _const_spec(w_branch.shape), _const_spec(w_out.shape)],
        out_specs=row(d),
        out_shape=jax.ShapeDtypeStruct((m, d), F32),
        compiler_params=pltpu.CompilerParams(dimension_semantics=("parallel",), vmem_limit_bytes=VMEM_LIMIT),
        name="merge",
    )(h, *outs, w_gate, b_gate, w_branch, w_out)


def _expert_body(x_ref, wg_ref, wu_ref, wd_ref, y_ref):
    bb, _, cap, d = x_ref.shape
    x = x_ref[...].reshape(bb * cap, d).astype(BF16)
    gate = jnp.dot(x, wg_ref[0], preferred_element_type=F32)
    up = jnp.dot(x, wu_ref[0], preferred_element_type=F32)
    hid = (gate * jax.nn.sigmoid(gate) * up).astype(BF16)
    y_ref[...] = jnp.dot(hid, wd_ref[0], preferred_element_type=F32).reshape(y_ref.shape)


def expert_swiglu(xe, w_ff_gate, w_ff_up, w_ff_down):
    b, e, cap, d = xe.shape
    f = w_ff_gate.shape[-1]
    bb = min(b, max(1, MXU_ROWS // cap))
    x_spec = pl.BlockSpec((bb, 1, cap, d), lambda ei, bi: (bi, ei, 0, 0))
    w_spec = lambda shape: pl.BlockSpec((1,) + shape, lambda ei, bi: (ei, 0, 0))
    return pl.pallas_call(
        _expert_body,
        grid=(e, b // bb),
        in_specs=[x_spec, w_spec((d, f)), w_spec((d, f)), w_spec((f, d))],
        out_specs=x_spec,
        out_shape=jax.ShapeDtypeStruct(xe.shape, F32),
        compiler_params=pltpu.CompilerParams(dimension_semantics=("parallel", "parallel"),
                                             vmem_limit_bytes=VMEM_LIMIT),
        name="expert_swiglu",
    )(xe, w_ff_gate, w_ff_up, w_ff_down)


def _layer_norm(x):
    xf = x.astype(F32)
    mu = jnp.mean(xf, axis=-1, keepdims=True)
    var = jnp.mean(jnp.square(xf - mu), axis=-1, keepdims=True)
    return (xf - mu) * lax.rsqrt(var + NORM_EPS)


def post_norm(x, g, b):
    return (_layer_norm(x) * g + b).astype(x.dtype)


def modulate(x, shift, scale):
    return (_layer_norm(x) * (1.0 + scale) + shift).astype(x.dtype)


def _rms(x):
    return x * lax.rsqrt(jnp.mean(jnp.square(x), axis=-1, keepdims=True) + NORM_EPS)


def to_heads(x):
    b_, t_, w_ = x.shape
    return x.reshape(b_, t_, N_HEADS, w_ // N_HEADS).transpose(0, 2, 1, 3)


def from_heads(x):
    b_, h_, t_, d_ = x.shape
    return x.transpose(0, 2, 1, 3).reshape(b_, t_, h_ * d_)


def head_rms_norm(o, w):
    return from_heads(_rms(o)) * w


def l2norm(x):
    return x * lax.rsqrt(jnp.sum(jnp.square(x), axis=-1, keepdims=True) + NORM_EPS)


def split_columns(z):
    sizes = [s for _, s in IN_SPLITS]
    parts = jnp.split(z[..., :IN_COLS], np.cumsum(sizes)[:-1].tolist(), axis=-1)
    return {name: p for (name, _), p in zip(IN_SPLITS, parts)}


def conv_centred(x, w):
    pad = CONV_K // 2
    return lax.conv_general_dilated(x, w[:, None, :], window_strides=(1,), padding=[(pad, pad)],
                                    dimension_numbers=("NWC", "WIO", "NWC"),
                                    feature_group_count=x.shape[-1])


def sincos_grid(rows, cols, dim):
    quarter = dim // 4
    omega = 1.0 / (10000.0 ** (jnp.arange(quarter, dtype=F32) / quarter))
    er = jnp.arange(rows, dtype=F32)[:, None] * omega
    ec = jnp.arange(cols, dtype=F32)[:, None] * omega
    er = jnp.concatenate([jnp.sin(er), jnp.cos(er)], axis=-1)
    ec = jnp.concatenate([jnp.sin(ec), jnp.cos(ec)], axis=-1)
    emb = jnp.concatenate([jnp.broadcast_to(er[:, None, :], (rows, cols, dim // 2)),
                           jnp.broadcast_to(ec[None, :, :], (rows, cols, dim // 2))], axis=-1)
    return emb.reshape(rows * cols, dim)


def _chunk(x):
    return x.reshape(x.shape[0], x.shape[1], x.shape[2] // CHUNK, CHUNK, *x.shape[3:])


def _tri():
    return jnp.tril(jnp.ones((CHUNK, CHUNK), dtype=bool))


def gla_vector_scan(q, k, v, logf, s0):
    tri = _tri()
    xs = tuple(jnp.moveaxis(_chunk(a), 2, 0) for a in (q, k, v, logf))

    def step(S, inp):
        qc, kc, vc, lf = inp
        g = jnp.cumsum(lf, axis=2)
        rel = jnp.where(tri[:, :, None], g[:, :, :, None, :] - g[:, :, None, :, :], -jnp.inf)
        att = jnp.einsum("bhid,bhijd,bhjd->bhij", qc, jnp.exp(rel), kc)
        g_last = g[:, :, -1:, :]
        o = att @ vc + (qc * jnp.exp(g)) @ S
        S = jnp.exp(g_last[:, :, 0, :, None]) * S + jnp.einsum("bhjd,bhje->bhde", kc * jnp.exp(g_last - g), vc)
        return S, o

    S, o = lax.scan(step, s0, xs)
    return jnp.moveaxis(o, 0, 2).reshape(v.shape), S


def ssd_scan(q, k, v, loga, s0):
    tri = _tri()
    qc, kc, vc, la = _chunk(q), _chunk(k), _chunk(v), _chunk(loga)
    g = jnp.cumsum(la, axis=-1)
    L = jnp.exp(jnp.where(tri, g[..., :, None] - g[..., None, :], -jnp.inf))
    o_intra = jnp.einsum("bhnij,bhnje->bhnie", jnp.einsum("bhnid,bhnjd->bhnij", qc, kc) * L, vc)
    g_last = g[..., -1]
    dS = jnp.einsum("bhnjd,bhnje->bhnde", kc * jnp.exp(g_last[..., None] - g)[..., None], vc)

    def step(S, inp):
        dS_c, gl_c = inp
        return jnp.exp(gl_c)[..., None, None] * S + dS_c, S

    S_T, S_in = lax.scan(step, s0, (jnp.moveaxis(dS, 2, 0), jnp.moveaxis(g_last, 2, 0)))
    S_in = jnp.moveaxis(S_in, 0, 2)
    o = o_intra + jnp.einsum("bhnid,bhnde->bhnie", qc * jnp.exp(g)[..., None], S_in)
    return o.reshape(v.shape), S_T


def gated_delta_scan(q, k, v, loga, beta, s0):
    tri = _tri()
    strict = tri & ~jnp.eye(CHUNK, dtype=bool)
    qc, kc, vc, la, bt = _chunk(q), _chunk(k), _chunk(v), _chunk(loga), _chunk(beta)
    g = jnp.cumsum(la, axis=-1)
    L = jnp.exp(jnp.where(tri, g[..., :, None] - g[..., None, :], -jnp.inf))
    a_mat = jnp.where(strict, bt[..., :, None] * jnp.einsum("bhnid,bhnjd->bhnij", kc, kc) * L, 0.0)
    t_mat = a_mat + jnp.eye(CHUNK, dtype=a_mat.dtype)
    u = lax.linalg.triangular_solve(t_mat, bt[..., None] * vc, left_side=True, lower=True, unit_diagonal=True)
    w = lax.linalg.triangular_solve(t_mat, bt[..., None] * kc * jnp.exp(g)[..., None],
                                    left_side=True, lower=True, unit_diagonal=True)
    qk = jnp.einsum("bhnid,bhnjd->bhnij", qc, kc) * L
    g_last = g[..., -1]
    q_dec = qc * jnp.exp(g)[..., None]
    k_dec = kc * jnp.exp(g_last[..., None] - g)[..., None]
    xs = tuple(jnp.moveaxis(a, 2, 0) for a in (u, w, qk, q_dec, k_dec, g_last))

    def step(S, inp):
        u_c, w_c, qk_c, qd_c, kd_c, gl_c = inp
        v_new = u_c - w_c @ S
        o_c = qd_c @ S + qk_c @ v_new
        S = jnp.exp(gl_c)[..., None, None] * S + jnp.einsum("bhjd,bhje->bhde", kd_c, v_new)
        return S, o_c

    S, o = lax.scan(step, s0, xs)
    return jnp.moveaxis(o, 0, 2).reshape(v.shape), S


def bidirectional(scan_fn, ctx_fwd, lat_fwd, ctx_bwd, lat_bwd, s0):
    def flip(t):
        return tuple(jnp.flip(a, axis=2) for a in t)
    oc_f, sc_f = scan_fn(*ctx_fwd, s0)
    ol_f, _ = scan_fn(*lat_fwd, sc_f)
    oc_b, sc_b = scan_fn(*flip(ctx_bwd), s0)
    ol_b, _ = scan_fn(*flip(lat_bwd), sc_b)
    return oc_f + jnp.flip(oc_b, axis=2), ol_f + jnp.flip(ol_b, axis=2)


def hgrn_lower_bounds(logits):
    cum = jnp.cumsum(jax.nn.softmax(logits.astype(F32), axis=1), axis=1)
    return cum - cum[:, :1]


def hgrn2_mixer(zc, zl, lb, norm_w):
    def inputs(z, d):
        zf = z[("a_f_fwd", "a_f_bwd")[d]].astype(F32)
        lbd = lb[d]
        logf = jnp.logaddexp(jnp.log(lbd), jnp.log1p(-lbd) + jax.nn.log_sigmoid(zf))
        inp = (1.0 - lbd) * jax.nn.sigmoid(-zf)
        q = jax.nn.silu(z["a_q"].astype(F32))
        v = z["a_v"].astype(F32)
        return (to_heads(q), to_heads(inp), to_heads(v), to_heads(logf))

    b_ = zc["a_q"].shape[0]
    s0 = jnp.zeros((b_, N_HEADS, HEAD_DIM, HEAD_DIM), F32)
    oc, ol = bidirectional(gla_vector_scan, inputs(zc, 0), inputs(zl, 0), inputs(zc, 1), inputs(zl, 1), s0)

    def finish(o, z):
        return head_rms_norm(o, norm_w) * jax.nn.silu(z["a_g"].astype(F32))
    return finish(oc, zc), finish(ol, zl)


def gated_deltanet_mixer(zc, zl, conv_w, a_log, dt_bias, norm_w):
    def inputs(z):
        qkv = jax.nn.silu(conv_centred(z["b_qkv"], conv_w)).astype(F32)
        q, k, v = jnp.split(qkv, 3, axis=-1)
        q = l2norm(to_heads(q)) * HEAD_DIM ** -0.5
        k = l2norm(to_heads(k))
        v = to_heads(v)
        a = z["b_a"].astype(F32)
        bb = z["b_beta"].astype(F32)
        dirs = []
        for d in range(2):
            sl = slice(d * N_HEADS, (d + 1) * N_HEADS)
            loga = -jnp.exp(a_log[d].astype(F32)) * jax.nn.softplus(a[..., sl] + dt_bias[d])
            beta = jax.nn.sigmoid(bb[..., sl])
            dirs.append((q, k, v, jnp.swapaxes(loga, 1, 2), jnp.swapaxes(beta, 1, 2)))
        return dirs

    ctx_in, lat_in = inputs(zc), inputs(zl)
    b_ = zc["b_g"].shape[0]
    s0 = jnp.zeros((b_, N_HEADS, HEAD_DIM, HEAD_DIM), F32)
    oc, ol = bidirectional(gated_delta_scan, ctx_in[0], lat_in[0], ctx_in[1], lat_in[1], s0)

    def finish(o, z):
        return head_rms_norm(o, norm_w) * jax.nn.silu(z["b_g"].astype(F32))
    return finish(oc, zc), finish(ol, zl)


def fourier_mixer(u):
    b_, t_, _ = u.shape
    ug = u.astype(F32).reshape(b_, t_, FOURIER_GROUPS, BRANCH_W // FOURIER_GROUPS)
    return jnp.fft.fft2(ug, axes=(1, 3), norm="ortho").real.reshape(b_, t_, BRANCH_W)


def ssd_mixer(zc, zl, conv_w, conv_b, a_log, dt_bias, d_skip, norm_w):
    def group_heads(t):
        b_, t_, _ = t.shape
        t = jnp.repeat(t.reshape(b_, t_, SSD_GROUPS, SSD_STATE), N_HEADS // SSD_GROUPS, axis=2)
        return t.transpose(0, 2, 1, 3)

    def inputs(z):
        xbc = jax.nn.silu(conv_centred(z["d_xbc"], conv_w) + conv_b).astype(F32)
        xs, bs, cs = jnp.split(xbc, [BRANCH_W, BRANCH_W + SSD_GROUPS * SSD_STATE], axis=-1)
        xh, bh, ch = to_heads(xs), group_heads(bs), group_heads(cs)
        dt_raw = z["d_dt"].astype(F32)
        dirs = []
        for d in range(2):
            dt = jax.nn.softplus(dt_raw[..., d * N_HEADS:(d + 1) * N_HEADS] + dt_bias[d])
            dt = jnp.swapaxes(dt, 1, 2)
            loga = -jnp.exp(a_log[d].astype(F32))[:, None] * dt
            dirs.append((ch, bh, xh * dt[..., None], loga))
        return xh, dirs

    xh_c, ctx_in = inputs(zc)
    xh_l, lat_in = inputs(zl)
    b_ = xh_c.shape[0]
    s0 = jnp.zeros((b_, N_HEADS, SSD_STATE, HEAD_DIM), F32)
    oc, ol = bidirectional(ssd_scan, ctx_in[0], lat_in[0], ctx_in[1], lat_in[1], s0)

    def finish(o, xh, z):
        y = from_heads(o + d_skip[:, None, None] * xh) * jax.nn.silu(z["d_z"].astype(F32))
        return _rms(y) * norm_w
    return finish(oc, xh_c, zc), finish(ol, xh_l, zl)


def _flat(a):
    return a.reshape(-1, a.shape[-1])


def token_mixer(hc, hl, w_in, lb, hgrn_norm_w, gdn_conv_w, gdn_a_log, gdn_dt_bias, gdn_norm_w,
                ssd_conv_w, ssd_conv_b, ssd_a_log, ssd_dt_bias, ssd_d, ssd_norm_w,
                w_gate, b_gate, w_branch, w_out, with_ctx):
    zc = split_columns(project(_flat(hc), w_in).reshape(hc.shape[0], hc.shape[1], -1))
    zl = split_columns(project(_flat(hl), w_in).reshape(hl.shape[0], hl.shape[1], -1))
    a_c, a_l = hgrn2_mixer(zc, zl, lb, hgrn_norm_w)
    b_c, b_l = gated_deltanet_mixer(zc, zl, gdn_conv_w, gdn_a_log, gdn_dt_bias, gdn_norm_w)
    d_c, d_l = ssd_mixer(zc, zl, ssd_conv_w, ssd_conv_b, ssd_a_log, ssd_dt_bias, ssd_d, ssd_norm_w)

    def merged(h, outs):
        y = merge_branches(_flat(h), tuple(_flat(o) for o in outs), w_gate, b_gate, w_branch, w_out)
        return y.reshape(h.shape)

    y_l = merged(hl, (a_l, b_l, fourier_mixer(zl["c_u"]), d_l))
    y_c = merged(hc, (a_c, b_c, fourier_mixer(zc["c_u"]), d_c)) if with_ctx else None
    return y_c, y_l


def expert_choice_ffn(h, w_router, w_ff_gate, w_ff_up, w_ff_down):
    b_, t_, _ = h.shape
    cap = EC_CAPACITY_FACTOR * t_ // N_EXPERTS
    aff = jax.nn.softmax(jnp.einsum("btd,de->bte", h, w_router).astype(F32), axis=-1)
    weight, idx = lax.top_k(jnp.swapaxes(aff, 1, 2), cap)
    bidx = jnp.arange(b_)[:, None, None]
    xe = h[bidx, idx]
    ye = expert_swiglu(xe, w_ff_gate, w_ff_up, w_ff_down) * weight[..., None].astype(h.dtype)
    return jnp.zeros_like(h).at[bidx, idx].add(ye)


def kernel(x, c, ctx, c_ctx, ada_w, ada_b, w_in, hgrn_lb_logits, hgrn_norm_w,
           gdn_conv_w, gdn_a_log, gdn_dt_bias, gdn_norm_w,
           ssd_conv_w, ssd_conv_b, ssd_a_log, ssd_dt_bias, ssd_d, ssd_norm_w,
           w_gate, b_gate, w_branch, w_out, ln1_g, ln1_b,
           w_router, w_ff_gate, w_ff_up, w_ff_down, ln2_g, ln2_b):
    n_lat = x.shape[1]
    rows = n_lat // GRID_W
    xl = x + sincos_grid(rows, GRID_W, D_MODEL).astype(x.dtype)
    xc = ctx
    lb_all = hgrn_lower_bounds(hgrn_lb_logits)
    for l in range(DEPTH):
        with_ctx = l < DEPTH - 1
        w_in_l = jnp.pad(w_in[l], ((0, 0), (0, IN_COLS_PAD - IN_COLS))).astype(BF16)
        w_gate_l, w_branch_l, w_out_l = w_gate[l].astype(BF16), w_branch[l].astype(BF16), w_out[l].astype(BF16)
        b_gate_l = b_gate[l][:, None, :]
        ffw = (w_ff_gate[l].astype(BF16), w_ff_up[l].astype(BF16), w_ff_down[l].astype(BF16))
        mod_l = jax.nn.silu(c) @ ada_w[l] + ada_b[l]
        mod_c = jax.nn.silu(c_ctx) @ ada_w[l] + ada_b[l]
        sh1_l, sc1_l, g1_l, sh2_l, sc2_l, g2_l = jnp.split(mod_l[:, None, :], 6, axis=-1)
        sh1_c, sc1_c, g1_c, sh2_c, sc2_c, g2_c = jnp.split(mod_c, 6, axis=-1)
        hl = modulate(xl, sh1_l, sc1_l)
        hc = modulate(xc, sh1_c, sc1_c)
        y_c, y_l = token_mixer(hc, hl, w_in_l, lb_all[:, l], hgrn_norm_w[l],
                               gdn_conv_w[l], gdn_a_log[l], gdn_dt_bias[l], gdn_norm_w[l],
                               ssd_conv_w[l], ssd_conv_b[l], ssd_a_log[l], ssd_dt_bias[l], ssd_d[l], ssd_norm_w[l],
                               w_gate_l, b_gate_l, w_branch_l, w_out_l, with_ctx)
        xl = post_norm(DEEPNORM_ALPHA * xl + g1_l * y_l, ln1_g[l], ln1_b[l])
        hl2 = modulate(xl, sh2_l, sc2_l)
        xl = post_norm(DEEPNORM_ALPHA * xl + g2_l * expert_choice_ffn(hl2, w_router[l], *ffw), ln2_g[l], ln2_b[l])
        if with_ctx:
            xc = post_norm(DEEPNORM_ALPHA * xc + g1_c * y_c, ln1_g[l], ln1_b[l])
            hc2 = modulate(xc, sh2_c, sc2_c)
            xc = post_norm(DEEPNORM_ALPHA * xc + g2_c * expert_choice_ffn(hc2, w_router[l], *ffw), ln2_g[l], ln2_b[l])
    return xl
```

```python
import functools
import math

import jax
import jax.numpy as jnp
import numpy as np
from jax import lax
from jax.experimental import pallas as pl
from jax.experimental.pallas import tpu as pltpu

D_MODEL = 1024
DEPTH = 4
GRID_W = 64
N_BRANCH = 4
BRANCH_W = D_MODEL // N_BRANCH
HEAD_DIM = 64
N_HEADS = BRANCH_W // HEAD_DIM
FOURIER_GROUP_W = HEAD_DIM
CHUNK = 64
CONV_K = 5
SSD_STATE = 64
SSD_GROUPS = 2
N_EXPERTS = 16
EC_CAPACITY_FACTOR = 2
DEEPNORM_ALPHA = (2.0 * DEPTH) ** 0.25
NORM_EPS = 1e-6

LANE = 128
SUBLANE = 8
MXU_ROWS = 256
VMEM_LIMIT = 48 * 1024 * 1024
FNET_VMEM_LIMIT = 56 * 1024 * 1024
NEG_BIG = -1e30

F32 = jnp.float32
BF16 = jnp.bfloat16

XBC_W = BRANCH_W + 2 * SSD_GROUPS * SSD_STATE
Z_ORDER = ("b_qkv", "a_q", "a_f_fwd", "a_f_bwd", "a_v", "a_g", "b_g", "c_u", "d_xbc", "d_z", "b_a", "b_beta", "d_dt")
REF_SPLITS = (
    ("a_q", BRANCH_W), ("a_f_fwd", BRANCH_W), ("a_f_bwd", BRANCH_W), ("a_v", BRANCH_W), ("a_g", BRANCH_W),
    ("b_qkv", 3 * BRANCH_W), ("b_g", BRANCH_W), ("b_a", 2 * N_HEADS), ("b_beta", 2 * N_HEADS),
    ("c_u", BRANCH_W), ("d_xbc", XBC_W), ("d_z", BRANCH_W), ("d_dt", 2 * N_HEADS),
)
Z_COLS = 3 * BRANCH_W + 5 * BRANCH_W + 2 * BRANCH_W + XBC_W + BRANCH_W + LANE
GDN_BLOCKS = (0, 8, 26)
HGRN_BLOCKS = (3, 4, 5, 6, 7)
FNET_BLOCK = 9
SSD_BLOCKS = (5, 12, 26)
DT_LANE0 = 4 * N_HEADS


def permute_w_in(w_in):
    start, spans = 0, {}
    for name, size in REF_SPLITS:
        spans[name] = (start, start + size)
        start += size
    parts = [w_in[:, spans[n][0]:spans[n][1]] for n in Z_ORDER]
    used = sum(p.shape[1] for p in parts)
    return jnp.concatenate(parts + [jnp.zeros((w_in.shape[0], Z_COLS - used), w_in.dtype)], axis=1)


def _bdot(a, b):
    return jnp.dot(a.astype(BF16), b.astype(BF16), preferred_element_type=F32)


def _bdot_nt(a, b):
    return lax.dot_general(a.astype(BF16), b.astype(BF16), (((1,), (1,)), ((), ())), preferred_element_type=F32)


def _bdot_tn(a, b):
    return lax.dot_general(a.astype(BF16), b.astype(BF16), (((0,), (0,)), ((), ())), preferred_element_type=F32)


def _split(x, terms):
    out = []
    for _ in range(terms):
        p = x.astype(BF16)
        out.append(p)
        x = x - p.astype(F32)
    return out


def _dot_exact_lhs(m, x, terms=3):
    mb = m.astype(BF16)
    return sum(jnp.dot(mb, p, preferred_element_type=F32) for p in _split(x, terms))


def _dot_exact_rhs(x, m, terms=2):
    mb = m.astype(BF16)
    return sum(jnp.dot(p, mb, preferred_element_type=F32) for p in _split(x, terms))


def _dot_x3(a, b):
    ah, al = _split(a, 2)
    bh, bl = _split(b, 2)
    d = lambda p, q: jnp.dot(p, q, preferred_element_type=F32)
    return d(ah, bh) + d(al, bh) + d(ah, bl)


def _iota2(n, m):
    return lax.broadcasted_iota(jnp.int32, (n, m), 0), lax.broadcasted_iota(jnp.int32, (n, m), 1)


def _order_masks(d):
    i, j = _iota2(CHUNK, CHUNK)
    return ((j <= i), (j < i)) if d == 0 else ((j >= i), (j > i))


def _sigmoid(x):
    return 1.0 / (1.0 + jnp.exp(-x))


def _softplus(x):
    return jnp.maximum(x, 0.0) + jnp.log(1.0 + jnp.exp(-jnp.abs(x)))


def _log_sigmoid(x):
    return jnp.minimum(x, 0.0) - jnp.log(1.0 + jnp.exp(-jnp.abs(x)))


def _logaddexp(a, b):
    m = jnp.maximum(a, b)
    return m + jnp.log(jnp.exp(a - m) + jnp.exp(b - m))


def _heads(x):
    return [x[:, h * HEAD_DIM:(h + 1) * HEAD_DIM] for h in range(N_HEADS)]


def _conv(x_ref, w_ref, n, n_chunks):
    t = n_chunks * CHUNK
    start = pl.multiple_of(n * CHUNK, CHUNK)
    cur = x_ref[pl.ds(start, CHUNK), :]
    prev_start = pl.multiple_of(jnp.maximum(start - SUBLANE, 0), SUBLANE)
    next_start = pl.multiple_of(jnp.minimum(start + CHUNK, t - SUBLANE), SUBLANE)
    prev = x_ref[pl.ds(prev_start, SUBLANE), :] * (n > 0).astype(F32)
    nxt = x_ref[pl.ds(next_start, SUBLANE), :] * (n < n_chunks - 1).astype(F32)
    ext = jnp.concatenate([prev, cur, nxt], axis=0)
    pad = CONV_K // 2
    acc = None
    for k in range(CONV_K):
        off = SUBLANE + k - pad
        term = ext[off:off + CHUNK, :] * w_ref[k:k + 1, :]
        acc = term if acc is None else acc + term
    return acc


def _backward_chunk(step, nc_c, n_tot):
    return jnp.where(step < nc_c, nc_c - 1 - step, n_tot - 1 - (step - nc_c))


def _chunk_rows(n):
    return pl.ds(pl.multiple_of(n * CHUNK, CHUNK), CHUNK)


def _head_norm_gate(o_s, gate_ref, out_ref, normw, bdm, n_chunks, base):
    def body(n, carry):
        rows = _chunk_rows(n)
        o = o_s[pl.ds(pl.multiple_of(base + n * CHUNK, CHUNK), CHUNK), :]
        ms = _dot_exact_rhs(o * o, bdm) * (1.0 / HEAD_DIM)
        gt = gate_ref[rows, :]
        out_ref[rows, :] = o * lax.rsqrt(ms + NORM_EPS) * normw[...] * (gt * _sigmoid(gt))
        return carry
    lax.fori_loop(0, n_chunks, body, 0)


def head_block_ones():
    i = np.arange(BRANCH_W)
    return jnp.asarray((i[:, None] // HEAD_DIM) == (i[None, :] // HEAD_DIM), BF16)


def _zspec(t, width, blk, **kw):
    return pl.BlockSpec((None, t, width), lambda i: (i, 0, blk), **kw)


def _cspec(shape, **kw):
    return pl.BlockSpec(shape, lambda *_: (0,) * len(shape), **kw)


def _mixer_out(b, tc, tl):
    specs = [pl.BlockSpec((None, tc, BRANCH_W), lambda i: (i, 0, 0)), pl.BlockSpec((None, tl, BRANCH_W), lambda i: (i, 0, 0))]
    shapes = [jax.ShapeDtypeStruct((b, tc, BRANCH_W), F32), jax.ShapeDtypeStruct((b, tl, BRANCH_W), F32)]
    return specs, shapes


_MIXER_PARAMS = pltpu.CompilerParams(dimension_semantics=("parallel",), vmem_limit_bytes=VMEM_LIMIT)


def _unit_lower_inverses(mats):
    i, j = _iota2(CHUNK, CHUNK)
    eye = (i == j).astype(F32)
    same4 = (i // 4) == (j // 4)
    d4 = [jnp.where(same4, a, 0.0) for a in mats]
    sq = [_bdot(d, d) for d in d4]
    xs = [eye - d for d in d4]
    xs = [x + _bdot(x, q) for x, q in zip(xs, sq)]
    s = 4
    while s < CHUNK:
        sel = ((i // (2 * s)) == (j // (2 * s))) & ((i // s) != (j // s))
        ox = [_bdot(jnp.where(sel, a, 0.0), x) for a, x in zip(mats, xs)]
        xs = [x - _bdot(x, y) for x, y in zip(xs, ox)]
        s *= 2
    return xs


def _gdn_body(nc_c, nc_l, qkv_c, qkv_l, gate_c, gate_l, sm_c, sm_l, convw, prow, normw, bd,
              out_c, out_l, u_s, w_s, qk_s, qd_s, kd_s, gl_s, o_s, st_s):
    w = BRANCH_W
    bdm = bd[...]
    lane = lax.broadcasted_iota(jnp.int32, (CHUNK, LANE), 1)
    ii, jj = _iota2(CHUNK, CHUNK)
    incl_lower = (jj <= ii).astype(F32)
    masks = [_order_masks(d) for d in range(2)]
    dh = [(d, h) for d in range(2) for h in range(N_HEADS)]
    n_tot = nc_c + nc_l

    def prep(x_ref, s_ref, n_chunks, base):
        def body(n, carry):
            y = _conv(x_ref, convw, n, n_chunks)
            y = y * _sigmoid(y)
            q, k, v = y[:, :w], y[:, w:2 * w], y[:, 2 * w:]
            q = q * lax.rsqrt(_dot_exact_rhs(q * q, bdm) + NORM_EPS) * HEAD_DIM ** -0.5
            k = k * lax.rsqrt(_dot_exact_rhs(k * k, bdm) + NORM_EPS)
            sm = s_ref[_chunk_rows(n), :]
            la = jnp.where(lane < 2 * N_HEADS, -jnp.exp(prow[0:1, :]) * _softplus(sm + prow[1:2, :]), 0.0)
            beta_all = _sigmoid(sm)
            prefix = _dot_exact_lhs(incl_lower, la)
            total = jnp.sum(la, axis=0, keepdims=True)
            g_all = jnp.where(lane < N_HEADS, prefix, total - prefix + la)
            g_t = g_all.T
            qh, kh, vh = _heads(q), _heads(k), _heads(v)
            qkk = [_bdot_nt(jnp.concatenate([qh[h], kh[h]], axis=0), kh[h]) for h in range(N_HEADS)]
            g_col = [g_all[:, d * N_HEADS + h:d * N_HEADS + h + 1] for d, h in dh]
            beta = [beta_all[:, (2 + d) * N_HEADS + h:(2 + d) * N_HEADS + h + 1] for d, h in dh]
            gl = [total[:, d * N_HEADS + h:d * N_HEADS + h + 1] for d, h in dh]
            decay = [jnp.exp(jnp.where(masks[d][0], g_col[x] - g_t[x:x + 1, :], NEG_BIG)) for x, (d, h) in enumerate(dh)]
            qk = [qkk[h][:CHUNK] * decay[x] for x, (d, h) in enumerate(dh)]
            a = [jnp.where(masks[d][1], beta[x] * qkk[h][CHUNK:] * decay[x], 0.0) for x, (d, h) in enumerate(dh)]
            tinv = _unit_lower_inverses(a)
            eg = [jnp.exp(g) for g in g_col]
            rhs = [jnp.concatenate([beta[x] * vh[h], beta[x] * kh[h] * eg[x]], axis=1) for x, (d, h) in enumerate(dh)]
            uw = [_bdot(t, r) for t, r in zip(tinv, rhs)]
            resid = [r - y0 - _dot_x3(m, y0) for r, y0, m in zip(rhs, uw, a)]
            uw = [y0 + _bdot(t, r) for y0, t, r in zip(uw, tinv, resid)]
            qd = [qh[h] * eg[x] for x, (d, h) in enumerate(dh)]
            kd = [kh[h] * jnp.exp(gl[x] - g_col[x]) for x, (d, h) in enumerate(dh)]
            rows = pl.ds(pl.multiple_of(base + n * CHUNK, CHUNK), CHUNK)
            for d in range(2):
                sel = slice(d * N_HEADS, (d + 1) * N_HEADS)
                u_s[d, rows, :] = jnp.concatenate([y1[:, :HEAD_DIM] for y1 in uw[sel]], axis=1)
                w_s[d, rows, :] = jnp.concatenate([y1[:, HEAD_DIM:] for y1 in uw[sel]], axis=1).astype(BF16)
                qk_s[d, rows, :] = jnp.concatenate(qk[sel], axis=1).astype(BF16)
                qd_s[d, rows, :] = jnp.concatenate(qd[sel], axis=1).astype(BF16)
                kd_s[d, rows, :] = jnp.concatenate(kd[sel], axis=1).astype(BF16)
            gl_s[pl.ds(base // CHUNK + n, 1), :] = jnp.exp(total)
            return carry
        lax.fori_loop(0, n_chunks, body, 0)

    prep(qkv_c, sm_c, nc_c, 0)
    prep(qkv_l, sm_l, nc_l, nc_c * CHUNK)

    st_s[...] = jnp.zeros_like(st_s)
    o_s[...] = jnp.zeros_like(o_s)

    def scan_body(step, carry):
        n_dir = (step, _backward_chunk(step, nc_c, n_tot))
        rows = [_chunk_rows(n) for n in n_dir]
        u = [t for d in range(2) for t in _heads(u_s[d, rows[d], :])]
        wm = [t for d in range(2) for t in _heads(w_s[d, rows[d], :])]
        qk = [t for d in range(2) for t in _heads(qk_s[d, rows[d], :])]
        qd = [t for d in range(2) for t in _heads(qd_s[d, rows[d], :])]
        kd = [t for d in range(2) for t in _heads(kd_s[d, rows[d], :])]
        egl = [gl_s[pl.ds(n_dir[d], 1), :][:, d * N_HEADS + h:d * N_HEADS + h + 1] for d, h in dh]
        s_prev = [st_s[x] for x in range(len(dh))]
        v_new = [u[x] - _bdot(wm[x], s_prev[x]) for x in range(len(dh))]
        o = [_bdot(qd[x], s_prev[x]) + _bdot(qk[x], v_new[x]) for x in range(len(dh))]
        for x in range(len(dh)):
            st_s[x] = egl[x] * s_prev[x] + _bdot_tn(kd[x], v_new[x])
        for d in range(2):
            o_s[rows[d], :] = o_s[rows[d], :] + jnp.concatenate(o[d * N_HEADS:(d + 1) * N_HEADS], axis=1)
        return carry
    lax.fori_loop(0, n_tot, scan_body, 0)

    _head_norm_gate(o_s, gate_c, out_c, normw, bdm, nc_c, 0)
    _head_norm_gate(o_s, gate_l, out_l, normw, bdm, nc_l, nc_c * CHUNK)


def gdn_mixer(zc, zl, conv_w, a_log, dt_bias, norm_w):
    b, tc, _ = zc.shape
    tl = zl.shape[1]
    nc_c, nc_l = tc // CHUNK, tl // CHUNK
    w = BRANCH_W
    qkv_blk, gate_blk, sm_blk = GDN_BLOCKS
    prow = jnp.zeros((SUBLANE, LANE), F32)
    prow = prow.at[0, :2 * N_HEADS].set(a_log.reshape(-1)).at[1, :2 * N_HEADS].set(dt_bias.reshape(-1))
    out_specs, out_shape = _mixer_out(b, tc, tl)
    return pl.pallas_call(
        functools.partial(_gdn_body, nc_c, nc_l),
        grid=(b,),
        in_specs=[_zspec(tc, 3 * w, qkv_blk), _zspec(tl, 3 * w, qkv_blk, pipeline_mode=pl.Buffered(1)),
                  _zspec(tc, w, gate_blk), _zspec(tl, w, gate_blk),
                  _zspec(tc, LANE, sm_blk), _zspec(tl, LANE, sm_blk),
                  _cspec((SUBLANE, 3 * w)), _cspec((SUBLANE, LANE)), _cspec((1, w)), _cspec((w, w))],
        out_specs=out_specs,
        out_shape=out_shape,
        scratch_shapes=[pltpu.VMEM((2, tc + tl, w), F32)] + [pltpu.VMEM((2, tc + tl, w), BF16)] * 4 + [
            pltpu.VMEM((nc_c + nc_l, LANE), F32), pltpu.VMEM((tc + tl, w), F32),
            pltpu.VMEM((2 * N_HEADS, HEAD_DIM, HEAD_DIM), F32)],
        compiler_params=_MIXER_PARAMS,
        name="gdn_mixer",
    )(zc, zl, zc, zl, zc, zl,
      jnp.pad(conv_w, ((0, SUBLANE - CONV_K), (0, 0))), prow, norm_w[None, :], head_block_ones())


GLA_LEVELS = (32, 16, 8, 4, 2, 1)


def _gla_level_tables(d):
    n_lv = len(GLA_LEVELS)
    i, t = _iota2(n_lv * CHUNK, CHUNK)
    sel = jnp.zeros((n_lv * CHUNK, CHUNK), F32)
    r, c = _iota2(CHUNK, CHUNK)
    masks = []
    for x, s in enumerate(GLA_LEVELS):
        row = i - x * CHUNK
        bound = 2 * s * (row // (2 * s)) + s - 1 + d
        sel = jnp.where((i // CHUNK == x) & (t == bound), 1.0, sel)
        same = (r // (2 * s)) == (c // (2 * s))
        r_hi, c_hi = (r % (2 * s)) >= s, (c % (2 * s)) >= s
        masks.append(same & (r_hi & ~c_hi if d == 0 else ~r_hi & c_hi))
    return sel, masks


def _hgrn_body(nc_c, nc_l, q_c, q_l, ff_c, ff_l, fb_c, fb_l, v_c, v_l, gate_c, gate_l, lbrow, normw, bd,
               out_c, out_l, att_s, qd_s, kd_s, v_s, gl_s, o_s, st_s):
    bdm = bd[...]
    ii, jj = _iota2(CHUNK, CHUNK)
    incl_lower = (jj <= ii).astype(F32)
    eye = ii == jj
    tables = [_gla_level_tables(d) for d in range(2)]
    n_dh = 2 * N_HEADS
    n_tot = nc_c + nc_l

    def prep(q_ref, f_refs, v_ref, n_chunks, base):
        def body(n, carry):
            rin = _chunk_rows(n)
            rows = pl.ds(pl.multiple_of(base + n * CHUNK, CHUNK), CHUNK)
            zq = q_ref[rin, :]
            q = zq * _sigmoid(zq)
            v_s[rows, :] = v_ref[rin, :].astype(BF16)
            for d in range(2):
                zf = f_refs[d][rin, :]
                log_lb, log1m_lb, one_m_lb = lbrow[3 * d:3 * d + 1, :], lbrow[3 * d + 1:3 * d + 2, :], lbrow[3 * d + 2:3 * d + 3, :]
                lf = _logaddexp(log_lb, log1m_lb + _log_sigmoid(zf))
                k = one_m_lb * _sigmoid(-zf)
                prefix = _dot_exact_lhs(incl_lower, lf)
                total = jnp.sum(lf, axis=0, keepdims=True)
                g = prefix if d == 0 else total - prefix + lf
                sel, masks = tables[d]
                c_all = _dot_exact_lhs(sel, g)
                diag = _dot_exact_rhs(q * k, bdm)
                acc = [jnp.where(eye, diag[:, h * HEAD_DIM:h * HEAD_DIM + 1], 0.0) for h in range(N_HEADS)]
                for x in range(len(GLA_LEVELS)):
                    c = c_all[x * CHUNK:(x + 1) * CHUNK, :]
                    qt = _heads(q * jnp.exp(jnp.minimum(g - c, 0.0)))
                    kt = _heads(k * jnp.exp(jnp.minimum(c - g, 0.0)))
                    acc = [a + jnp.where(masks[x], _bdot_nt(qt[h], kt[h]), 0.0) for h, a in enumerate(acc)]
                att_s[d, rows, :] = jnp.concatenate(acc, axis=1).astype(BF16)
                qd_s[d, rows, :] = (q * jnp.exp(g)).astype(BF16)
                kd_s[d, rows, :] = (k * jnp.exp(total - g)).astype(BF16)
                gl_s[d, pl.ds(base // CHUNK + n, 1), :] = jnp.exp(total)
            return carry
        lax.fori_loop(0, n_chunks, body, 0)

    prep(q_c, (ff_c, fb_c), v_c, nc_c, 0)
    prep(q_l, (ff_l, fb_l), v_l, nc_l, nc_c * CHUNK)

    st_s[...] = jnp.zeros_like(st_s)
    o_s[...] = jnp.zeros_like(o_s)

    def scan_body(step, carry):
        n_dir = (step, _backward_chunk(step, nc_c, n_tot))
        rows = [_chunk_rows(n) for n in n_dir]
        att = [t for d in range(2) for t in _heads(att_s[d, rows[d], :])]
        qd = [t for d in range(2) for t in _heads(qd_s[d, rows[d], :])]
        kd = [t for d in range(2) for t in _heads(kd_s[d, rows[d], :])]
        v = [t for d in range(2) for t in _heads(v_s[rows[d], :])]
        egl = [t for d in range(2) for t in _heads(gl_s[d, pl.ds(n_dir[d], 1), :])]
        s_prev = [st_s[x] for x in range(n_dh)]
        o = [_bdot(att[x], v[x]) + _bdot_nt(qd[x], s_prev[x]) for x in range(n_dh)]
        for x in range(n_dh):
            st_s[x] = egl[x] * s_prev[x] + _bdot_tn(v[x], kd[x])
        for d in range(2):
            o_s[rows[d], :] = o_s[rows[d], :] + jnp.concatenate(o[d * N_HEADS:(d + 1) * N_HEADS], axis=1)
        return carry
    lax.fori_loop(0, n_tot, scan_body, 0)

    _head_norm_gate(o_s, gate_c, out_c, normw, bdm, nc_c, 0)
    _head_norm_gate(o_s, gate_l, out_l, normw, bdm, nc_l, nc_c * CHUNK)


def hgrn_mixer(zc, zl, lb, norm_w):
    b, tc, _ = zc.shape
    tl = zl.shape[1]
    nc_c, nc_l = tc // CHUNK, tl // CHUNK
    w = BRANCH_W
    lbrow = jnp.zeros((SUBLANE, w), F32)
    for d in range(2):
        lbrow = lbrow.at[3 * d].set(jnp.log(lb[d])).at[3 * d + 1].set(jnp.log1p(-lb[d])).at[3 * d + 2].set(1.0 - lb[d])
    in_specs, args = [], []
    for blk in HGRN_BLOCKS:
        in_specs += [_zspec(tc, w, blk), _zspec(tl, w, blk, pipeline_mode=pl.Buffered(1))]
        args += [zc, zl]
    out_specs, out_shape = _mixer_out(b, tc, tl)
    return pl.pallas_call(
        functools.partial(_hgrn_body, nc_c, nc_l),
        grid=(b,),
        in_specs=in_specs + [_cspec((SUBLANE, w)), _cspec((1, w)), _cspec((w, w))],
        out_specs=out_specs,
        out_shape=out_shape,
        scratch_shapes=[pltpu.VMEM((2, tc + tl, w), BF16)] * 3 + [
            pltpu.VMEM((tc + tl, w), BF16), pltpu.VMEM((2, nc_c + nc_l, w), F32), pltpu.VMEM((tc + tl, w), F32),
            pltpu.VMEM((2 * N_HEADS, HEAD_DIM, HEAD_DIM), F32)],
        compiler_params=_MIXER_PARAMS,
        name="hgrn_mixer",
    )(*args, lbrow, norm_w[None, :], head_block_ones())


def _ssd_body(nc_c, nc_l, xbc_c, xbc_l, z_c, z_l, sm_c, sm_l, convw, convb, prow, dskip, normw,
              out_c, out_l, att_s, v_s, qd_s, kd_s, x_s, gl_s, o_s, st_s):
    w = BRANCH_W
    gw = SSD_GROUPS * SSD_STATE
    lane = lax.broadcasted_iota(jnp.int32, (CHUNK, LANE), 1)
    ii, jj = _iota2(CHUNK, CHUNK)
    incl_lower = (jj <= ii).astype(F32)
    masks = [_order_masks(d) for d in range(2)]
    dh = [(d, h) for d in range(2) for h in range(N_HEADS)]
    n_tot = nc_c + nc_l
    heads_per_group = N_HEADS // SSD_GROUPS

    def prep(x_ref, s_ref, n_chunks, base):
        def body(n, carry):
            rows = pl.ds(pl.multiple_of(base + n * CHUNK, CHUNK), CHUNK)
            y = _conv(x_ref, convw, n, n_chunks) + convb[...]
            y = y * _sigmoid(y)
            xs, bs, cs = y[:, :w], y[:, w:w + gw], y[:, w + gw:]
            x_s[rows, :] = xs
            sm = s_ref[_chunk_rows(n), :]
            dt_all = _softplus(sm + prow[1:2, :])
            in_dt = (lane >= DT_LANE0) & (lane < DT_LANE0 + 2 * N_HEADS)
            la = jnp.where(in_dt, -jnp.exp(prow[0:1, :]) * dt_all, 0.0)
            prefix = _dot_exact_lhs(incl_lower, la)
            total = jnp.sum(la, axis=0, keepdims=True)
            g_all = jnp.where(lane < DT_LANE0 + N_HEADS, prefix, total - prefix + la)
            g_t = g_all.T
            xh = _heads(xs)
            bg = [bs[:, g * SSD_STATE:(g + 1) * SSD_STATE] for g in range(SSD_GROUPS)]
            cg = [cs[:, g * SSD_STATE:(g + 1) * SSD_STATE] for g in range(SSD_GROUPS)]
            cb = [_bdot_nt(cg[g], bg[g]) for g in range(SSD_GROUPS)]
            att, v, qd, kd = [], [], [], []
            for d, h in dh:
                l = DT_LANE0 + d * N_HEADS + h
                grp = h // heads_per_group
                g_col = g_all[:, l:l + 1]
                decay = jnp.exp(jnp.where(masks[d][0], g_col - g_t[l:l + 1, :], NEG_BIG))
                att.append(cb[grp] * decay)
                v.append(xh[h] * dt_all[:, l:l + 1])
                qd.append(cg[grp] * jnp.exp(g_col))
                kd.append(bg[grp] * jnp.exp(total[:, l:l + 1] - g_col))
            for d in range(2):
                sel = slice(d * N_HEADS, (d + 1) * N_HEADS)
                att_s[d, rows, :] = jnp.concatenate(att[sel], axis=1).astype(BF16)
                v_s[d, rows, :] = jnp.concatenate(v[sel], axis=1).astype(BF16)
                qd_s[d, rows, :] = jnp.concatenate(qd[sel], axis=1).astype(BF16)
                kd_s[d, rows, :] = jnp.concatenate(kd[sel], axis=1).astype(BF16)
            gl_s[pl.ds(base // CHUNK + n, 1), :] = jnp.exp(total)
            return carry
        lax.fori_loop(0, n_chunks, body, 0)

    prep(xbc_c, sm_c, nc_c, 0)
    prep(xbc_l, sm_l, nc_l, nc_c * CHUNK)

    st_s[...] = jnp.zeros_like(st_s)
    o_s[...] = jnp.zeros_like(o_s)

    def scan_body(step, carry):
        n_dir = (step, _backward_chunk(step, nc_c, n_tot))
        rows = [_chunk_rows(n) for n in n_dir]
        att = [t for d in range(2) for t in _heads(att_s[d, rows[d], :])]
        v = [t for d in range(2) for t in _heads(v_s[d, rows[d], :])]
        qd = [t for d in range(2) for t in _heads(qd_s[d, rows[d], :])]
        kd = [t for d in range(2) for t in _heads(kd_s[d, rows[d], :])]
        egl = [gl_s[pl.ds(n_dir[d], 1), :][:, DT_LANE0 + d * N_HEADS + h:DT_LANE0 + d * N_HEADS + h + 1] for d, h in dh]
        s_prev = [st_s[x] for x in range(len(dh))]
        o = [_bdot(att[x], v[x]) + _bdot(qd[x], s_prev[x]) for x in range(len(dh))]
        for x in range(len(dh)):
            st_s[x] = egl[x] * s_prev[x] + _bdot_tn(kd[x], v[x])
        for d in range(2):
            o_s[rows[d], :] = o_s[rows[d], :] + jnp.concatenate(o[d * N_HEADS:(d + 1) * N_HEADS], axis=1)
        return carry
    lax.fori_loop(0, n_tot, scan_body, 0)

    def finish(z_ref, out_ref, n_chunks, base):
        def body(n, carry):
            rows_in = pl.ds(pl.multiple_of(base + n * CHUNK, CHUNK), CHUNK)
            rows = _chunk_rows(n)
            zt = z_ref[rows, :]
            y = (o_s[rows_in, :] + dskip[...] * x_s[rows_in, :]) * (zt * _sigmoid(zt))
            ms = jnp.sum(y * y, axis=1, keepdims=True) * (1.0 / w)
            out_ref[rows, :] = y * lax.rsqrt(ms + NORM_EPS) * normw[...]
            return carry
        lax.fori_loop(0, n_chunks, body, 0)

    finish(z_c, out_c, nc_c, 0)
    finish(z_l, out_l, nc_l, nc_c * CHUNK)


def ssd_mixer(zc, zl, conv_w, conv_b, a_log, dt_bias, d_skip, norm_w):
    b, tc, _ = zc.shape
    tl = zl.shape[1]
    nc_c, nc_l = tc // CHUNK, tl // CHUNK
    w = BRANCH_W
    xbc_blk, z_blk, sm_blk = SSD_BLOCKS
    prow = jnp.zeros((SUBLANE, LANE), F32)
    prow = prow.at[0, DT_LANE0:DT_LANE0 + 2 * N_HEADS].set(a_log.reshape(-1))
    prow = prow.at[1, DT_LANE0:DT_LANE0 + 2 * N_HEADS].set(dt_bias.reshape(-1))
    out_specs, out_shape = _mixer_out(b, tc, tl)
    return pl.pallas_call(
        functools.partial(_ssd_body, nc_c, nc_l),
        grid=(b,),
        in_specs=[_zspec(tc, XBC_W, xbc_blk), _zspec(tl, XBC_W, xbc_blk, pipeline_mode=pl.Buffered(1)),
                  _zspec(tc, w, z_blk), _zspec(tl, w, z_blk),
                  _zspec(tc, LANE, sm_blk), _zspec(tl, LANE, sm_blk),
                  _cspec((SUBLANE, XBC_W)), _cspec((1, XBC_W)), _cspec((SUBLANE, LANE)), _cspec((1, w)), _cspec((1, w))],
        out_specs=out_specs,
        out_shape=out_shape,
        scratch_shapes=[pltpu.VMEM((2, tc + tl, w), BF16)] * 4 + [
            pltpu.VMEM((tc + tl, w), F32), pltpu.VMEM((nc_c + nc_l, LANE), F32), pltpu.VMEM((tc + tl, w), F32),
            pltpu.VMEM((2 * N_HEADS, SSD_STATE, HEAD_DIM), F32)],
        compiler_params=_MIXER_PARAMS,
        name="ssd_mixer",
    )(zc, zl, zc, zl, zc, zl,
      jnp.pad(conv_w, ((0, SUBLANE - CONV_K), (0, 0))), conv_b[None, :], prow,
      jnp.repeat(d_skip, HEAD_DIM)[None, :], norm_w[None, :])


def _fnet_body(t, u_ref, ch_ref, cl_ref, sh_ref, sl_ref, gch_ref, gcl_ref, gsh_ref, gsl_ref, out_ref,
               ph_s, pl_s, qh_s, ql_s):
    d = lambda a, b: jnp.dot(a, b, preferred_element_type=F32)

    def channel_dft(n, carry):
        rows = pl.ds(pl.multiple_of(n * MXU_ROWS, MXU_ROWS), MXU_ROWS)
        uh, ul = _split(u_ref[rows, :], 2)
        p = d(uh, gch_ref[...]) + d(ul, gch_ref[...]) + d(uh, gcl_ref[...])
        q = d(uh, gsh_ref[...]) + d(ul, gsh_ref[...]) + d(uh, gsl_ref[...])
        ph_s[rows, :], pl_s[rows, :] = _split(p, 2)
        qh_s[rows, :], ql_s[rows, :] = _split(q, 2)
        return carry
    lax.fori_loop(0, t // MXU_ROWS, channel_dft, 0)

    def sequence_dft(n, carry):
        rows = pl.ds(pl.multiple_of(n * MXU_ROWS, MXU_ROWS), MXU_ROWS)
        ch, cl, sh, sl = ch_ref[rows, :], cl_ref[rows, :], sh_ref[rows, :], sl_ref[rows, :]
        re = d(ch, ph_s[...]) + d(cl, ph_s[...]) + d(ch, pl_s[...])
        im = d(sh, qh_s[...]) + d(sl, qh_s[...]) + d(sh, ql_s[...])
        out_ref[rows, :] = re - im
        return carry
    lax.fori_loop(0, t // MXU_ROWS, sequence_dft, 0)


def _dft_tables(n, scale):
    j = lax.broadcasted_iota(jnp.int32, (n, n), 0)
    k = lax.broadcasted_iota(jnp.int32, (n, n), 1)
    ang = ((j * k) % n).astype(F32) * (2.0 * math.pi / n)
    out = []
    for tab in (jnp.cos(ang) * scale, jnp.sin(ang) * scale):
        hi = tab.astype(BF16)
        out += [hi, (tab - hi.astype(F32)).astype(BF16)]
    return out


def fnet_tables(t):
    seq = _dft_tables(t, t ** -0.5)
    grp = _dft_tables(FOURIER_GROUP_W, FOURIER_GROUP_W ** -0.5)
    n_grp = BRANCH_W // FOURIER_GROUP_W
    grp = [jnp.kron(jnp.eye(n_grp, dtype=F32), g.astype(F32)).astype(BF16) for g in grp]
    return seq + grp


def fnet_mixer(z, tables):
    b, t, _ = z.shape
    w = BRANCH_W
    const = functools.partial(_cspec, pipeline_mode=pl.Buffered(1))
    return pl.pallas_call(
        functools.partial(_fnet_body, t),
        grid=(b,),
        in_specs=[_zspec(t, w, FNET_BLOCK)] + [const((t, t))] * 4 + [const((w, w))] * 4,
        out_specs=pl.BlockSpec((None, t, w), lambda i: (i, 0, 0)),
        out_shape=jax.ShapeDtypeStruct((b, t, w), F32),
        scratch_shapes=[pltpu.VMEM((t, w), BF16)] * 4,
        compiler_params=pltpu.CompilerParams(dimension_semantics=("parallel",), vmem_limit_bytes=FNET_VMEM_LIMIT),
        name="fnet_mixer",
    )(z, *tables)


def _proj_body(a_ref, w_ref, o_ref):
    o_ref[...] = jnp.dot(a_ref[...].astype(BF16), w_ref[...], preferred_element_type=F32)


def project(a, w):
    m, k = a.shape
    n = w.shape[1]
    tm = MXU_ROWS
    return pl.pallas_call(
        _proj_body,
        grid=(m // tm,),
        in_specs=[pl.BlockSpec((tm, k), lambda i: (i, 0)), _cspec((k, n), pipeline_mode=pl.Buffered(1))],
        out_specs=pl.BlockSpec((tm, n), lambda i: (i, 0)),
        out_shape=jax.ShapeDtypeStruct((m, n), F32),
        compiler_params=_MIXER_PARAMS,
        name="project",
    )(a, w)


def _merge_body(h_ref, oa_ref, ob_ref, oc_ref, od_ref, wg_ref, bg_ref, wb_ref, wo_ref, y_ref):
    h = h_ref[...].astype(BF16)
    acc = None
    for g, o_ref in enumerate((oa_ref, ob_ref, oc_ref, od_ref)):
        gate = jax.nn.sigmoid(jnp.dot(h, wg_ref[g], preferred_element_type=F32) + bg_ref[g])
        t = gate * jnp.dot(o_ref[...].astype(BF16), wb_ref[g], preferred_element_type=F32)
        acc = t if acc is None else acc + t
    y_ref[...] = jnp.dot(acc.astype(BF16), wo_ref[...], preferred_element_type=F32)


def merge_branches(h, outs, w_gate, b_gate, w_branch, w_out):
    m, d = h.shape
    w = outs[0].shape[1]
    tm = MXU_ROWS
    row = lambda width: pl.BlockSpec((tm, width), lambda i: (i, 0))
    const = functools.partial(_cspec, pipeline_mode=pl.Buffered(1))
    return pl.pallas_call(
        _merge_body,
        grid=(m // tm,),
        in_specs=[row(d), row(w), row(w), row(w), row(w),
                  const(w_gate.shape), const(b_gate.shape), const(w_branch.shape), const(w_out.shape)],
        out_specs=row(d),
        out_shape=jax.ShapeDtypeStruct((m, d), F32),
        compiler_params=_MIXER_PARAMS,
        name="merge",
    )(h, *outs, w_gate, b_gate, w_branch, w_out)


def _expert_body(x_ref, wg_ref, wu_ref, wd_ref, y_ref):
    bb, _, cap, d = x_ref.shape
    x = x_ref[...].reshape(bb * cap, d).astype(BF16)
    gate = jnp.dot(x, wg_ref[0], preferred_element_type=F32)
    up = jnp.dot(x, wu_ref[0], preferred_element_type=F32)
    hid = (gate * jax.nn.sigmoid(gate) * up).astype(BF16)
    y_ref[...] = jnp.dot(hid, wd_ref[0], preferred_element_type=F32).reshape(y_ref.shape)


def expert_swiglu(xe, w_ff_gate, w_ff_up, w_ff_down):
    b, e, cap, d = xe.shape
    f = w_ff_gate.shape[-1]
    bb = min(b, max(1, MXU_ROWS // cap))
    x_spec = pl.BlockSpec((bb, 1, cap, d), lambda ei, bi: (bi, ei, 0, 0))
    w_spec = lambda shape: pl.BlockSpec((1,) + shape, lambda ei, bi: (ei, 0, 0))
    return pl.pallas_call(
        _expert_body,
        grid=(e, b // bb),
        in_specs=[x_spec, w_spec((d, f)), w_spec((d, f)), w_spec((f, d))],
        out_specs=x_spec,
        out_shape=jax.ShapeDtypeStruct(xe.shape, F32),
        compiler_params=pltpu.CompilerParams(dimension_semantics=("parallel", "parallel"),
                                             vmem_limit_bytes=VMEM_LIMIT),
        name="expert_swiglu",
    )(xe, w_ff_gate, w_ff_up, w_ff_down)


def _layer_norm(x):
    xf = x.astype(F32)
    mu = jnp.mean(xf, axis=-1, keepdims=True)
    var = jnp.mean(jnp.square(xf - mu), axis=-1, keepdims=True)
    return (xf - mu) * lax.rsqrt(var + NORM_EPS)


def post_norm(x, g, b):
    return (_layer_norm(x) * g + b).astype(x.dtype)


def modulate(x, shift, scale):
    return (_layer_norm(x) * (1.0 + scale) + shift).astype(x.dtype)


def sincos_grid(rows, cols, dim):
    quarter = dim // 4
    omega = 1.0 / (10000.0 ** (jnp.arange(quarter, dtype=F32) / quarter))
    er = jnp.arange(rows, dtype=F32)[:, None] * omega
    ec = jnp.arange(cols, dtype=F32)[:, None] * omega
    er = jnp.concatenate([jnp.sin(er), jnp.cos(er)], axis=-1)
    ec = jnp.concatenate([jnp.sin(ec), jnp.cos(ec)], axis=-1)
    emb = jnp.concatenate([jnp.broadcast_to(er[:, None, :], (rows, cols, dim // 2)),
                           jnp.broadcast_to(ec[None, :, :], (rows, cols, dim // 2))], axis=-1)
    return emb.reshape(rows * cols, dim)


def hgrn_lower_bounds(logits):
    cum = jnp.cumsum(jax.nn.softmax(logits.astype(F32), axis=1), axis=1)
    return cum - cum[:, :1]


def _flat(a):
    return a.reshape(-1, a.shape[-1])


def token_mixer(hc, hl, p, fnet_tabs, with_ctx):
    zc = project(_flat(hc), p["w_in"]).reshape(hc.shape[0], hc.shape[1], Z_COLS)
    zl = project(_flat(hl), p["w_in"]).reshape(hl.shape[0], hl.shape[1], Z_COLS)
    a_c, a_l = hgrn_mixer(zc, zl, p["lb"], p["hgrn_norm_w"])
    b_c, b_l = gdn_mixer(zc, zl, p["gdn_conv_w"], p["gdn_a_log"], p["gdn_dt_bias"], p["gdn_norm_w"])
    d_c, d_l = ssd_mixer(zc, zl, p["ssd_conv_w"], p["ssd_conv_b"], p["ssd_a_log"], p["ssd_dt_bias"],
                         p["ssd_d"], p["ssd_norm_w"])

    def merged(h, outs):
        y = merge_branches(_flat(h), tuple(_flat(o) for o in outs), p["w_gate"], p["b_gate"], p["w_branch"], p["w_out"])
        return y.reshape(h.shape)

    y_l = merged(hl, (a_l, b_l, fnet_mixer(zl, fnet_tabs[1]), d_l))
    y_c = merged(hc, (a_c, b_c, fnet_mixer(zc, fnet_tabs[0]), d_c)) if with_ctx else None
    return y_c, y_l


def expert_choice_ffn(h, w_router, w_ff_gate, w_ff_up, w_ff_down):
    b_, t_, _ = h.shape
    cap = EC_CAPACITY_FACTOR * t_ // N_EXPERTS
    aff = jax.nn.softmax(jnp.einsum("btd,de->bte", h, w_router).astype(F32), axis=-1)
    weight, idx = lax.top_k(jnp.swapaxes(aff, 1, 2), cap)
    bidx = jnp.arange(b_)[:, None, None]
    xe = h[bidx, idx]
    ye = expert_swiglu(xe, w_ff_gate, w_ff_up, w_ff_down) * weight[..., None].astype(h.dtype)
    return jnp.zeros_like(h).at[bidx, idx].add(ye)


def kernel(x, c, ctx, c_ctx, ada_w, ada_b, w_in, hgrn_lb_logits, hgrn_norm_w,
           gdn_conv_w, gdn_a_log, gdn_dt_bias, gdn_norm_w,
           ssd_conv_w, ssd_conv_b, ssd_a_log, ssd_dt_bias, ssd_d, ssd_norm_w,
           w_gate, b_gate, w_branch, w_out, ln1_g, ln1_b,
           w_router, w_ff_gate, w_ff_up, w_ff_down, ln2_g, ln2_b):
    n_lat = x.shape[1]
    rows = n_lat // GRID_W
    xl = x + sincos_grid(rows, GRID_W, D_MODEL).astype(x.dtype)
    xc = ctx
    lb_all = hgrn_lower_bounds(hgrn_lb_logits)
    fnet_tabs = (fnet_tables(ctx.shape[1]), fnet_tables(n_lat))
    for l in range(DEPTH):
        with_ctx = l < DEPTH - 1
        p = {
            "w_in": permute_w_in(w_in[l]).astype(BF16), "lb": lb_all[:, l], "hgrn_norm_w": hgrn_norm_w[l],
            "gdn_conv_w": gdn_conv_w[l], "gdn_a_log": gdn_a_log[l], "gdn_dt_bias": gdn_dt_bias[l],
            "gdn_norm_w": gdn_norm_w[l], "ssd_conv_w": ssd_conv_w[l], "ssd_conv_b": ssd_conv_b[l],
            "ssd_a_log": ssd_a_log[l], "ssd_dt_bias": ssd_dt_bias[l], "ssd_d": ssd_d[l], "ssd_norm_w": ssd_norm_w[l],
            "w_gate": w_gate[l].astype(BF16), "b_gate": b_gate[l][:, None, :],
            "w_branch": w_branch[l].astype(BF16), "w_out": w_out[l].astype(BF16),
        }
        ffw = (w_ff_gate[l].astype(BF16), w_ff_up[l].astype(BF16), w_ff_down[l].astype(BF16))
        mod_l = jax.nn.silu(c) @ ada_w[l] + ada_b[l]
        mod_c = jax.nn.silu(c_ctx) @ ada_w[l] + ada_b[l]
        sh1_l, sc1_l, g1_l, sh2_l, sc2_l, g2_l = jnp.split(mod_l[:, None, :], 6, axis=-1)
        sh1_c, sc1_c, g1_c, sh2_c, sc2_c, g2_c = jnp.split(mod_c, 6, axis=-1)
        hl = modulate(xl, sh1_l, sc1_l)
        hc = modulate(xc, sh1_c, sc1_c)
        y_c, y_l = token_mixer(hc, hl, p, fnet_tabs, with_ctx)
        xl = post_norm(DEEPNORM_ALPHA * xl + g1_l * y_l, ln1_g[l], ln1_b[l])
        hl2 = modulate(xl, sh2_l, sc2_l)
        xl = post_norm(DEEPNORM_ALPHA * xl + g2_l * expert_choice_ffn(hl2, w_router[l], *ffw), ln2_g[l], ln2_b[l])
        if with_ctx:
            xc = post_norm(DEEPNORM_ALPHA * xc + g1_c * y_c, ln1_g[l], ln1_b[l])
            hc2 = modulate(xc, sh2_c, sc2_c)
            xc = post_norm(DEEPNORM_ALPHA * xc + g2_c * expert_choice_ffn(hc2, w_router[l], *ffw), ln2_g[l], ln2_b[l])
    return xl
```

```python
import functools
import math

import jax
import jax.numpy as jnp
import numpy as np
from jax import lax
from jax.experimental import pallas as pl
from jax.experimental.pallas import tpu as pltpu

D_MODEL = 1024
DEPTH = 4
GRID_W = 64
N_BRANCH = 4
BRANCH_W = D_MODEL // N_BRANCH
HEAD_DIM = 64
N_HEADS = BRANCH_W // HEAD_DIM
FOURIER_GROUP_W = HEAD_DIM
CHUNK = 64
CONV_K = 5
SSD_STATE = 64
SSD_GROUPS = 2
N_EXPERTS = 16
EC_CAPACITY_FACTOR = 2
DEEPNORM_ALPHA = (2.0 * DEPTH) ** 0.25
NORM_EPS = 1e-6

LANE = 128
SUBLANE = 8
MXU_ROWS = 256
VMEM_LIMIT = 48 * 1024 * 1024
FNET_VMEM_LIMIT = 56 * 1024 * 1024
NEG_BIG = -1e30

F32 = jnp.float32
BF16 = jnp.bfloat16

XBC_W = BRANCH_W + 2 * SSD_GROUPS * SSD_STATE
Z_ORDER = ("b_qkv", "a_q", "a_f_fwd", "a_f_bwd", "a_v", "a_g", "b_g", "c_u", "d_xbc", "d_z", "b_a", "b_beta", "d_dt")
REF_SPLITS = (
    ("a_q", BRANCH_W), ("a_f_fwd", BRANCH_W), ("a_f_bwd", BRANCH_W), ("a_v", BRANCH_W), ("a_g", BRANCH_W),
    ("b_qkv", 3 * BRANCH_W), ("b_g", BRANCH_W), ("b_a", 2 * N_HEADS), ("b_beta", 2 * N_HEADS),
    ("c_u", BRANCH_W), ("d_xbc", XBC_W), ("d_z", BRANCH_W), ("d_dt", 2 * N_HEADS),
)
Z_COLS = 3 * BRANCH_W + 5 * BRANCH_W + 2 * BRANCH_W + XBC_W + BRANCH_W + LANE
GDN_BLOCKS = (0, 8, 26)
HGRN_BLOCKS = (3, 4, 5, 6, 7)
FNET_BLOCK = 9
SSD_BLOCKS = (5, 12, 26)
DT_LANE0 = 4 * N_HEADS


def permute_w_in(w_in):
    start, spans = 0, {}
    for name, size in REF_SPLITS:
        spans[name] = (start, start + size)
        start += size
    parts = [w_in[:, spans[n][0]:spans[n][1]] for n in Z_ORDER]
    used = sum(p.shape[1] for p in parts)
    return jnp.concatenate(parts + [jnp.zeros((w_in.shape[0], Z_COLS - used), w_in.dtype)], axis=1)


def _bdot(a, b):
    return jnp.dot(a.astype(BF16), b.astype(BF16), preferred_element_type=F32)


def _bdot_nt(a, b):
    return lax.dot_general(a.astype(BF16), b.astype(BF16), (((1,), (1,)), ((), ())), preferred_element_type=F32)


def _bdot_tn(a, b):
    return lax.dot_general(a.astype(BF16), b.astype(BF16), (((0,), (0,)), ((), ())), preferred_element_type=F32)


def _split(x, terms):
    out = []
    for _ in range(terms):
        p = x.astype(BF16)
        out.append(p)
        x = x - p.astype(F32)
    return out


def _dot_exact_lhs(m, x, terms=3):
    mb = m.astype(BF16)
    return sum(jnp.dot(mb, p, preferred_element_type=F32) for p in _split(x, terms))


def _dot_exact_rhs(x, m, terms=2):
    mb = m.astype(BF16)
    return sum(jnp.dot(p, mb, preferred_element_type=F32) for p in _split(x, terms))


def _dot_x3(a, b):
    ah, al = _split(a, 2)
    bh, bl = _split(b, 2)
    d = lambda p, q: jnp.dot(p, q, preferred_element_type=F32)
    return d(ah, bh) + d(al, bh) + d(ah, bl)


def _iota2(n, m):
    return lax.broadcasted_iota(jnp.int32, (n, m), 0), lax.broadcasted_iota(jnp.int32, (n, m), 1)


def _order_masks(d):
    i, j = _iota2(CHUNK, CHUNK)
    return ((j <= i), (j < i)) if d == 0 else ((j >= i), (j > i))


def _sigmoid(x):
    return 1.0 / (1.0 + jnp.exp(-x))


def _softplus(x):
    return jnp.maximum(x, 0.0) + jnp.log(1.0 + jnp.exp(-jnp.abs(x)))


def _log_sigmoid(x):
    return jnp.minimum(x, 0.0) - jnp.log(1.0 + jnp.exp(-jnp.abs(x)))


def _logaddexp(a, b):
    m = jnp.maximum(a, b)
    return m + jnp.log(jnp.exp(a - m) + jnp.exp(b - m))


def _heads(x):
    return [x[:, h * HEAD_DIM:(h + 1) * HEAD_DIM] for h in range(N_HEADS)]


def _conv(x_ref, w_ref, n, n_chunks):
    t = n_chunks * CHUNK
    start = pl.multiple_of(n * CHUNK, CHUNK)
    cur = x_ref[pl.ds(start, CHUNK), :]
    prev_start = pl.multiple_of(jnp.maximum(start - SUBLANE, 0), SUBLANE)
    next_start = pl.multiple_of(jnp.minimum(start + CHUNK, t - SUBLANE), SUBLANE)
    prev = x_ref[pl.ds(prev_start, SUBLANE), :] * jnp.where(n > 0, 1.0, 0.0)
    nxt = x_ref[pl.ds(next_start, SUBLANE), :] * jnp.where(n < n_chunks - 1, 1.0, 0.0)
    ext = jnp.concatenate([prev, cur, nxt], axis=0)
    pad = CONV_K // 2
    acc = None
    for k in range(CONV_K):
        off = SUBLANE + k - pad
        term = ext[off:off + CHUNK, :] * w_ref[k:k + 1, :]
        acc = term if acc is None else acc + term
    return acc


PREP_UNROLL = 2


def _interleave(progs):
    live = list(progs)
    while live:
        nxt = []
        for p in live:
            try:
                next(p)
                nxt.append(p)
            except StopIteration:
                pass
        live = nxt


def _chunk_loop(prog, n_chunks):
    def body(i, carry):
        _interleave([prog(i * PREP_UNROLL + j) for j in range(PREP_UNROLL)])
        return carry
    lax.fori_loop(0, n_chunks // PREP_UNROLL, body, 0)


def _backward_chunk(step, nc_c, n_tot):
    return jnp.where(step < nc_c, nc_c - 1 - step, n_tot - 1 - (step - nc_c))


def _chunk_rows(n):
    return pl.ds(pl.multiple_of(n * CHUNK, CHUNK), CHUNK)


def _head_norm_gate(o_s, gate_ref, out_ref, normw, bdm, n_chunks, base):
    def body(n, carry):
        rows = pl.ds(pl.multiple_of(n * MXU_ROWS, MXU_ROWS), MXU_ROWS)
        o = o_s[pl.ds(pl.multiple_of(base + n * MXU_ROWS, MXU_ROWS), MXU_ROWS), :]
        ms = _dot_exact_rhs(o * o, bdm) * (1.0 / HEAD_DIM)
        gt = gate_ref[rows, :]
        out_ref[rows, :] = o * lax.rsqrt(ms + NORM_EPS) * normw[...] * (gt * _sigmoid(gt))
        return carry
    lax.fori_loop(0, n_chunks * CHUNK // MXU_ROWS, body, 0)


def head_block_ones():
    i = np.arange(BRANCH_W)
    return jnp.asarray((i[:, None] // HEAD_DIM) == (i[None, :] // HEAD_DIM), BF16)


def _zspec(t, width, blk, **kw):
    return pl.BlockSpec((None, t, width), lambda i: (i, 0, blk), **kw)


def _cspec(shape, **kw):
    return pl.BlockSpec(shape, lambda *_: (0,) * len(shape), **kw)


def _mixer_out(b, tc, tl):
    assert tc % MXU_ROWS == 0 and tl % MXU_ROWS == 0 and MXU_ROWS % (PREP_UNROLL * CHUNK) == 0
    specs = [pl.BlockSpec((None, tc, BRANCH_W), lambda i: (i, 0, 0)), pl.BlockSpec((None, tl, BRANCH_W), lambda i: (i, 0, 0))]
    shapes = [jax.ShapeDtypeStruct((b, tc, BRANCH_W), F32), jax.ShapeDtypeStruct((b, tl, BRANCH_W), F32)]
    return specs, shapes


_MIXER_PARAMS = pltpu.CompilerParams(dimension_semantics=("parallel",), vmem_limit_bytes=VMEM_LIMIT)


def _unit_lower_inverses(mats):
    i, j = _iota2(CHUNK, CHUNK)
    eye = (i == j).astype(F32)
    same4 = (i // 4) == (j // 4)
    d4 = [jnp.where(same4, a, 0.0) for a in mats]
    sq = [_bdot(d, d) for d in d4]
    xs = [eye - d for d in d4]
    yield
    xs = [x + _bdot(x, q) for x, q in zip(xs, sq)]
    yield
    s = 4
    while s < CHUNK:
        sel = ((i // (2 * s)) == (j // (2 * s))) & ((i // s) != (j // s))
        ox = [_bdot(jnp.where(sel, a, 0.0), x) for a, x in zip(mats, xs)]
        yield
        xs = [x - _bdot(x, y) for x, y in zip(xs, ox)]
        yield
        s *= 2
    return xs


def _gdn_body(nc_c, nc_l, qkv_c, qkv_l, gate_c, gate_l, sm_c, sm_l, convw, prow, normw, bd,
              out_c, out_l, u_s, w_s, qk_s, qd_s, kd_s, gl_s, o_s, st_s):
    w = BRANCH_W
    bdm = bd[...]
    lane = lax.broadcasted_iota(jnp.int32, (CHUNK, LANE), 1)
    ii, jj = _iota2(CHUNK, CHUNK)
    incl_lower = (jj <= ii).astype(F32)
    masks = [_order_masks(d) for d in range(2)]
    dh = [(d, h) for d in range(2) for h in range(N_HEADS)]
    n_tot = nc_c + nc_l

    def prep(x_ref, s_ref, n_chunks, base):
        def prog(n):
            y = _conv(x_ref, convw, n, n_chunks)
            y = y * _sigmoid(y)
            q, k, v = y[:, :w], y[:, w:2 * w], y[:, 2 * w:]
            qss, kss = _dot_exact_rhs(q * q, bdm), _dot_exact_rhs(k * k, bdm)
            yield
            q = q * lax.rsqrt(qss + NORM_EPS) * HEAD_DIM ** -0.5
            k = k * lax.rsqrt(kss + NORM_EPS)
            sm = s_ref[_chunk_rows(n), :]
            la = jnp.where(lane < 2 * N_HEADS, -jnp.exp(prow[0:1, :]) * _softplus(sm + prow[1:2, :]), 0.0)
            beta_all = _sigmoid(sm)
            prefix = _dot_exact_lhs(incl_lower, la)
            total = jnp.sum(la, axis=0, keepdims=True)
            g_all = jnp.where(lane < N_HEADS, prefix, total - prefix + la)
            g_t = g_all.T
            qh, kh, vh = _heads(q), _heads(k), _heads(v)
            qkk = [_bdot_nt(jnp.concatenate([qh[h], kh[h]], axis=0), kh[h]) for h in range(N_HEADS)]
            yield
            g_col = [g_all[:, d * N_HEADS + h:d * N_HEADS + h + 1] for d, h in dh]
            beta = [beta_all[:, (2 + d) * N_HEADS + h:(2 + d) * N_HEADS + h + 1] for d, h in dh]
            gl = [total[:, d * N_HEADS + h:d * N_HEADS + h + 1] for d, h in dh]
            decay = [jnp.exp(jnp.where(masks[d][0], g_col[x] - g_t[x:x + 1, :], NEG_BIG)) for x, (d, h) in enumerate(dh)]
            qk = [qkk[h][:CHUNK] * decay[x] for x, (d, h) in enumerate(dh)]
            a = [jnp.where(masks[d][1], beta[x] * qkk[h][CHUNK:] * decay[x], 0.0) for x, (d, h) in enumerate(dh)]
            tinv = yield from _unit_lower_inverses(a)
            eg = [jnp.exp(g) for g in g_col]
            rhs = [jnp.concatenate([beta[x] * vh[h], beta[x] * kh[h] * eg[x]], axis=1) for x, (d, h) in enumerate(dh)]
            uw = [_bdot(t, r) for t, r in zip(tinv, rhs)]
            yield
            resid = [r - y0 - _dot_x3(m, y0) for r, y0, m in zip(rhs, uw, a)]
            yield
            uw = [y0 + _bdot(t, r) for y0, t, r in zip(uw, tinv, resid)]
            yield
            qd = [qh[h] * eg[x] for x, (d, h) in enumerate(dh)]
            kd = [kh[h] * jnp.exp(gl[x] - g_col[x]) for x, (d, h) in enumerate(dh)]
            rows = pl.ds(pl.multiple_of(base + n * CHUNK, CHUNK), CHUNK)
            for d in range(2):
                sel = slice(d * N_HEADS, (d + 1) * N_HEADS)
                u_s[d, rows, :] = jnp.concatenate([y1[:, :HEAD_DIM] for y1 in uw[sel]], axis=1)
                w_s[d, rows, :] = jnp.concatenate([y1[:, HEAD_DIM:] for y1 in uw[sel]], axis=1).astype(BF16)
                qk_s[d, rows, :] = jnp.concatenate(qk[sel], axis=1).astype(BF16)
                qd_s[d, rows, :] = jnp.concatenate(qd[sel], axis=1).astype(BF16)
                kd_s[d, rows, :] = jnp.concatenate(kd[sel], axis=1).astype(BF16)
            gl_s[pl.ds(base // CHUNK + n, 1), :] = jnp.exp(total)
            yield
        _chunk_loop(prog, n_chunks)

    prep(qkv_c, sm_c, nc_c, 0)
    prep(qkv_l, sm_l, nc_l, nc_c * CHUNK)

    st_s[...] = jnp.zeros_like(st_s)
    o_s[...] = jnp.zeros_like(o_s)

    def scan_body(step, carry):
        n_dir = (step, _backward_chunk(step, nc_c, n_tot))
        rows = [_chunk_rows(n) for n in n_dir]
        u = [t for d in range(2) for t in _heads(u_s[d, rows[d], :])]
        wm = [t for d in range(2) for t in _heads(w_s[d, rows[d], :])]
        qk = [t for d in range(2) for t in _heads(qk_s[d, rows[d], :])]
        qd = [t for d in range(2) for t in _heads(qd_s[d, rows[d], :])]
        kd = [t for d in range(2) for t in _heads(kd_s[d, rows[d], :])]
        egl = [gl_s[pl.ds(n_dir[d], 1), :][:, d * N_HEADS + h:d * N_HEADS + h + 1] for d, h in dh]
        s_prev = [st_s[x] for x in range(len(dh))]
        v_new = [u[x] - _bdot(wm[x], s_prev[x]) for x in range(len(dh))]
        o = [_bdot(qd[x], s_prev[x]) + _bdot(qk[x], v_new[x]) for x in range(len(dh))]
        for x in range(len(dh)):
            st_s[x] = egl[x] * s_prev[x] + _bdot_tn(kd[x], v_new[x])
        for d in range(2):
            o_s[rows[d], :] = o_s[rows[d], :] + jnp.concatenate(o[d * N_HEADS:(d + 1) * N_HEADS], axis=1)
        return carry
    lax.fori_loop(0, n_tot, scan_body, 0)

    _head_norm_gate(o_s, gate_c, out_c, normw, bdm, nc_c, 0)
    _head_norm_gate(o_s, gate_l, out_l, normw, bdm, nc_l, nc_c * CHUNK)


def gdn_mixer(zc, zl, conv_w, a_log, dt_bias, norm_w):
    b, tc, _ = zc.shape
    tl = zl.shape[1]
    nc_c, nc_l = tc // CHUNK, tl // CHUNK
    w = BRANCH_W
    qkv_blk, gate_blk, sm_blk = GDN_BLOCKS
    prow = jnp.zeros((SUBLANE, LANE), F32)
    prow = prow.at[0, :2 * N_HEADS].set(a_log.reshape(-1)).at[1, :2 * N_HEADS].set(dt_bias.reshape(-1))
    out_specs, out_shape = _mixer_out(b, tc, tl)
    return pl.pallas_call(
        functools.partial(_gdn_body, nc_c, nc_l),
        grid=(b,),
        in_specs=[_zspec(tc, 3 * w, qkv_blk), _zspec(tl, 3 * w, qkv_blk, pipeline_mode=pl.Buffered(1)),
                  _zspec(tc, w, gate_blk), _zspec(tl, w, gate_blk),
                  _zspec(tc, LANE, sm_blk), _zspec(tl, LANE, sm_blk),
                  _cspec((SUBLANE, 3 * w)), _cspec((SUBLANE, LANE)), _cspec((1, w)), _cspec((w, w))],
        out_specs=out_specs,
        out_shape=out_shape,
        scratch_shapes=[pltpu.VMEM((2, tc + tl, w), F32)] + [pltpu.VMEM((2, tc + tl, w), BF16)] * 4 + [
            pltpu.VMEM((nc_c + nc_l, LANE), F32), pltpu.VMEM((tc + tl, w), F32),
            pltpu.VMEM((2 * N_HEADS, HEAD_DIM, HEAD_DIM), F32)],
        compiler_params=_MIXER_PARAMS,
        name="gdn_mixer",
    )(zc, zl, zc, zl, zc, zl,
      jnp.pad(conv_w, ((0, SUBLANE - CONV_K), (0, 0))), prow, norm_w[None, :], head_block_ones())


GLA_LEVELS = (32, 16, 8, 4, 2, 1)


def _gla_level_tables(d):
    n_lv = len(GLA_LEVELS)
    i, t = _iota2(n_lv * CHUNK, CHUNK)
    sel = jnp.zeros((n_lv * CHUNK, CHUNK), F32)
    r, c = _iota2(CHUNK, CHUNK)
    masks = []
    for x, s in enumerate(GLA_LEVELS):
        row = i - x * CHUNK
        bound = 2 * s * (row // (2 * s)) + s - 1 + d
        sel = jnp.where((i // CHUNK == x) & (t == bound), 1.0, sel)
        same = (r // (2 * s)) == (c // (2 * s))
        r_hi, c_hi = (r % (2 * s)) >= s, (c % (2 * s)) >= s
        masks.append(same & (r_hi & ~c_hi if d == 0 else ~r_hi & c_hi))
    return sel, masks


def _hgrn_body(nc_c, nc_l, q_c, q_l, ff_c, ff_l, fb_c, fb_l, v_c, v_l, gate_c, gate_l, lbrow, normw, bd,
               out_c, out_l, att_s, qd_s, kd_s, v_s, gl_s, o_s, st_s):
    bdm = bd[...]
    ii, jj = _iota2(CHUNK, CHUNK)
    incl_lower = (jj <= ii).astype(F32)
    eye = ii == jj
    tables = [_gla_level_tables(d) for d in range(2)]
    n_dh = 2 * N_HEADS
    n_tot = nc_c + nc_l

    def prep(q_ref, f_refs, v_ref, n_chunks, base):
        def prog(n):
            rin = _chunk_rows(n)
            rows = pl.ds(pl.multiple_of(base + n * CHUNK, CHUNK), CHUNK)
            zq = q_ref[rin, :]
            q = zq * _sigmoid(zq)
            v_s[rows, :] = v_ref[rin, :].astype(BF16)
            for d in range(2):
                zf = f_refs[d][rin, :]
                log_lb, log1m_lb, one_m_lb = lbrow[3 * d:3 * d + 1, :], lbrow[3 * d + 1:3 * d + 2, :], lbrow[3 * d + 2:3 * d + 3, :]
                lf = _logaddexp(log_lb, log1m_lb + _log_sigmoid(zf))
                k = one_m_lb * _sigmoid(-zf)
                prefix = _dot_exact_lhs(incl_lower, lf)
                diag = _dot_exact_rhs(q * k, bdm)
                yield
                total = jnp.sum(lf, axis=0, keepdims=True)
                g = prefix if d == 0 else total - prefix + lf
                sel, masks = tables[d]
                c_all = _dot_exact_lhs(sel, g)
                yield
                acc = [jnp.where(eye, diag[:, h * HEAD_DIM:h * HEAD_DIM + 1], 0.0) for h in range(N_HEADS)]
                for x in range(len(GLA_LEVELS)):
                    c = c_all[x * CHUNK:(x + 1) * CHUNK, :]
                    qt = _heads(q * jnp.exp(jnp.minimum(g - c, 0.0)))
                    kt = _heads(k * jnp.exp(jnp.minimum(c - g, 0.0)))
                    acc = [a + jnp.where(masks[x], _bdot_nt(qt[h], kt[h]), 0.0) for h, a in enumerate(acc)]
                    yield
                att_s[d, rows, :] = jnp.concatenate(acc, axis=1).astype(BF16)
                qd_s[d, rows, :] = (q * jnp.exp(g)).astype(BF16)
                kd_s[d, rows, :] = (k * jnp.exp(total - g)).astype(BF16)
                gl_s[d, pl.ds(base // CHUNK + n, 1), :] = jnp.exp(total)
        _chunk_loop(prog, n_chunks)

    prep(q_c, (ff_c, fb_c), v_c, nc_c, 0)
    prep(q_l, (ff_l, fb_l), v_l, nc_l, nc_c * CHUNK)

    st_s[...] = jnp.zeros_like(st_s)
    o_s[...] = jnp.zeros_like(o_s)

    def scan_body(step, carry):
        n_dir = (step, _backward_chunk(step, nc_c, n_tot))
        rows = [_chunk_rows(n) for n in n_dir]
        att = [t for d in range(2) for t in _heads(att_s[d, rows[d], :])]
        qd = [t for d in range(2) for t in _heads(qd_s[d, rows[d], :])]
        kd = [t for d in range(2) for t in _heads(kd_s[d, rows[d], :])]
        v = [t for d in range(2) for t in _heads(v_s[rows[d], :])]
        egl = [t for d in range(2) for t in _heads(gl_s[d, pl.ds(n_dir[d], 1), :])]
        s_prev = [st_s[x] for x in range(n_dh)]
        o = [_bdot(att[x], v[x]) + _bdot_nt(qd[x], s_prev[x]) for x in range(n_dh)]
        for x in range(n_dh):
            st_s[x] = egl[x] * s_prev[x] + _bdot_tn(v[x], kd[x])
        for d in range(2):
            o_s[rows[d], :] = o_s[rows[d], :] + jnp.concatenate(o[d * N_HEADS:(d + 1) * N_HEADS], axis=1)
        return carry
    lax.fori_loop(0, n_tot, scan_body, 0)

    _head_norm_gate(o_s, gate_c, out_c, normw, bdm, nc_c, 0)
    _head_norm_gate(o_s, gate_l, out_l, normw, bdm, nc_l, nc_c * CHUNK)


def hgrn_mixer(zc, zl, lb, norm_w):
    b, tc, _ = zc.shape
    tl = zl.shape[1]
    nc_c, nc_l = tc // CHUNK, tl // CHUNK
    w = BRANCH_W
    lbrow = jnp.zeros((SUBLANE, w), F32)
    for d in range(2):
        lbrow = lbrow.at[3 * d].set(jnp.log(lb[d])).at[3 * d + 1].set(jnp.log1p(-lb[d])).at[3 * d + 2].set(1.0 - lb[d])
    in_specs, args = [], []
    for blk in HGRN_BLOCKS:
        in_specs += [_zspec(tc, w, blk), _zspec(tl, w, blk, pipeline_mode=pl.Buffered(1))]
        args += [zc, zl]
    out_specs, out_shape = _mixer_out(b, tc, tl)
    return pl.pallas_call(
        functools.partial(_hgrn_body, nc_c, nc_l),
        grid=(b,),
        in_specs=in_specs + [_cspec((SUBLANE, w)), _cspec((1, w)), _cspec((w, w))],
        out_specs=out_specs,
        out_shape=out_shape,
        scratch_shapes=[pltpu.VMEM((2, tc + tl, w), BF16)] * 3 + [
            pltpu.VMEM((tc + tl, w), BF16), pltpu.VMEM((2, nc_c + nc_l, w), F32), pltpu.VMEM((tc + tl, w), F32),
            pltpu.VMEM((2 * N_HEADS, HEAD_DIM, HEAD_DIM), F32)],
        compiler_params=_MIXER_PARAMS,
        name="hgrn_mixer",
    )(*args, lbrow, norm_w[None, :], head_block_ones())


def _ssd_body(nc_c, nc_l, xbc_c, xbc_l, z_c, z_l, sm_c, sm_l, convw, convb, prow, dskip, normw,
              out_c, out_l, att_s, v_s, qd_s, kd_s, x_s, gl_s, o_s, st_s):
    w = BRANCH_W
    gw = SSD_GROUPS * SSD_STATE
    lane = lax.broadcasted_iota(jnp.int32, (CHUNK, LANE), 1)
    ii, jj = _iota2(CHUNK, CHUNK)
    incl_lower = (jj <= ii).astype(F32)
    masks = [_order_masks(d) for d in range(2)]
    dh = [(d, h) for d in range(2) for h in range(N_HEADS)]
    n_tot = nc_c + nc_l
    heads_per_group = N_HEADS // SSD_GROUPS

    def prep(x_ref, s_ref, n_chunks, base):
        def prog(n):
            rows = pl.ds(pl.multiple_of(base + n * CHUNK, CHUNK), CHUNK)
            y = _conv(x_ref, convw, n, n_chunks) + convb[...]
            y = y * _sigmoid(y)
            xs, bs, cs = y[:, :w], y[:, w:w + gw], y[:, w + gw:]
            x_s[rows, :] = xs
            sm = s_ref[_chunk_rows(n), :]
            dt_all = _softplus(sm + prow[1:2, :])
            in_dt = (lane >= DT_LANE0) & (lane < DT_LANE0 + 2 * N_HEADS)
            la = jnp.where(in_dt, -jnp.exp(prow[0:1, :]) * dt_all, 0.0)
            prefix = _dot_exact_lhs(incl_lower, la)
            total = jnp.sum(la, axis=0, keepdims=True)
            g_all = jnp.where(lane < DT_LANE0 + N_HEADS, prefix, total - prefix + la)
            g_t = g_all.T
            xh = _heads(xs)
            bg = [bs[:, g * SSD_STATE:(g + 1) * SSD_STATE] for g in range(SSD_GROUPS)]
            cg = [cs[:, g * SSD_STATE:(g + 1) * SSD_STATE] for g in range(SSD_GROUPS)]
            cb = [_bdot_nt(cg[g], bg[g]) for g in range(SSD_GROUPS)]
            yield
            att, v, qd, kd = [], [], [], []
            for d, h in dh:
                l = DT_LANE0 + d * N_HEADS + h
                grp = h // heads_per_group
                g_col = g_all[:, l:l + 1]
                decay = jnp.exp(jnp.where(masks[d][0], g_col - g_t[l:l + 1, :], NEG_BIG))
                att.append(cb[grp] * decay)
                v.append(xh[h] * dt_all[:, l:l + 1])
                qd.append(cg[grp] * jnp.exp(g_col))
                kd.append(bg[grp] * jnp.exp(total[:, l:l + 1] - g_col))
            for d in range(2):
                sel = slice(d * N_HEADS, (d + 1) * N_HEADS)
                att_s[d, rows, :] = jnp.concatenate(att[sel], axis=1).astype(BF16)
                v_s[d, rows, :] = jnp.concatenate(v[sel], axis=1).astype(BF16)
                qd_s[d, rows, :] = jnp.concatenate(qd[sel], axis=1).astype(BF16)
                kd_s[d, rows, :] = jnp.concatenate(kd[sel], axis=1).astype(BF16)
            gl_s[pl.ds(base // CHUNK + n, 1), :] = jnp.exp(total)
            yield
        _chunk_loop(prog, n_chunks)

    prep(xbc_c, sm_c, nc_c, 0)
    prep(xbc_l, sm_l, nc_l, nc_c * CHUNK)

    st_s[...] = jnp.zeros_like(st_s)
    o_s[...] = jnp.zeros_like(o_s)

    def scan_body(step, carry):
        n_dir = (step, _backward_chunk(step, nc_c, n_tot))
        rows = [_chunk_rows(n) for n in n_dir]
        att = [t for d in range(2) for t in _heads(att_s[d, rows[d], :])]
        v = [t for d in range(2) for t in _heads(v_s[d, rows[d], :])]
        qd = [t for d in range(2) for t in _heads(qd_s[d, rows[d], :])]
        kd = [t for d in range(2) for t in _heads(kd_s[d, rows[d], :])]
        egl = [gl_s[pl.ds(n_dir[d], 1), :][:, DT_LANE0 + d * N_HEADS + h:DT_LANE0 + d * N_HEADS + h + 1] for d, h in dh]
        s_prev = [st_s[x] for x in range(len(dh))]
        o = [_bdot(att[x], v[x]) + _bdot(qd[x], s_prev[x]) for x in range(len(dh))]
        for x in range(len(dh)):
            st_s[x] = egl[x] * s_prev[x] + _bdot_tn(kd[x], v[x])
        for d in range(2):
            o_s[rows[d], :] = o_s[rows[d], :] + jnp.concatenate(o[d * N_HEADS:(d + 1) * N_HEADS], axis=1)
        return carry
    lax.fori_loop(0, n_tot, scan_body, 0)

    def finish(z_ref, out_ref, n_chunks, base):
        def body(n, carry):
            rows_in = pl.ds(pl.multiple_of(base + n * MXU_ROWS, MXU_ROWS), MXU_ROWS)
            rows = pl.ds(pl.multiple_of(n * MXU_ROWS, MXU_ROWS), MXU_ROWS)
            zt = z_ref[rows, :]
            y = (o_s[rows_in, :] + dskip[...] * x_s[rows_in, :]) * (zt * _sigmoid(zt))
            ms = jnp.sum(y * y, axis=1, keepdims=True) * (1.0 / w)
            out_ref[rows, :] = y * lax.rsqrt(ms + NORM_EPS) * normw[...]
            return carry
        lax.fori_loop(0, n_chunks * CHUNK // MXU_ROWS, body, 0)

    finish(z_c, out_c, nc_c, 0)
    finish(z_l, out_l, nc_l, nc_c * CHUNK)


def ssd_mixer(zc, zl, conv_w, conv_b, a_log, dt_bias, d_skip, norm_w):
    b, tc, _ = zc.shape
    tl = zl.shape[1]
    nc_c, nc_l = tc // CHUNK, tl // CHUNK
    w = BRANCH_W
    xbc_blk, z_blk, sm_blk = SSD_BLOCKS
    prow = jnp.zeros((SUBLANE, LANE), F32)
    prow = prow.at[0, DT_LANE0:DT_LANE0 + 2 * N_HEADS].set(a_log.reshape(-1))
    prow = prow.at[1, DT_LANE0:DT_LANE0 + 2 * N_HEADS].set(dt_bias.reshape(-1))
    out_specs, out_shape = _mixer_out(b, tc, tl)
    return pl.pallas_call(
        functools.partial(_ssd_body, nc_c, nc_l),
        grid=(b,),
        in_specs=[_zspec(tc, XBC_W, xbc_blk), _zspec(tl, XBC_W, xbc_blk, pipeline_mode=pl.Buffered(1)),
                  _zspec(tc, w, z_blk), _zspec(tl, w, z_blk),
                  _zspec(tc, LANE, sm_blk), _zspec(tl, LANE, sm_blk),
                  _cspec((SUBLANE, XBC_W)), _cspec((1, XBC_W)), _cspec((SUBLANE, LANE)), _cspec((1, w)), _cspec((1, w))],
        out_specs=out_specs,
        out_shape=out_shape,
        scratch_shapes=[pltpu.VMEM((2, tc + tl, w), BF16)] * 4 + [
            pltpu.VMEM((tc + tl, w), F32), pltpu.VMEM((nc_c + nc_l, LANE), F32), pltpu.VMEM((tc + tl, w), F32),
            pltpu.VMEM((2 * N_HEADS, SSD_STATE, HEAD_DIM), F32)],
        compiler_params=_MIXER_PARAMS,
        name="ssd_mixer",
    )(zc, zl, zc, zl, zc, zl,
      jnp.pad(conv_w, ((0, SUBLANE - CONV_K), (0, 0))), conv_b[None, :], prow,
      jnp.repeat(d_skip, HEAD_DIM)[None, :], norm_w[None, :])


def _fnet_body(t, u_ref, ch_ref, cl_ref, sh_ref, sl_ref, gch_ref, gcl_ref, gsh_ref, gsl_ref, out_ref,
               ph_s, pl_s, qh_s, ql_s):
    d = lambda a, b: jnp.dot(a, b, preferred_element_type=F32)

    def channel_dft(n, carry):
        rows = pl.ds(pl.multiple_of(n * MXU_ROWS, MXU_ROWS), MXU_ROWS)
        uh, ul = _split(u_ref[rows, :], 2)
        p = d(uh, gch_ref[...]) + d(ul, gch_ref[...]) + d(uh, gcl_ref[...])
        q = d(uh, gsh_ref[...]) + d(ul, gsh_ref[...]) + d(uh, gsl_ref[...])
        ph_s[rows, :], pl_s[rows, :] = _split(p, 2)
        qh_s[rows, :], ql_s[rows, :] = _split(q, 2)
        return carry
    lax.fori_loop(0, t // MXU_ROWS, channel_dft, 0)

    def sequence_dft(n, carry):
        rows = pl.ds(pl.multiple_of(n * MXU_ROWS, MXU_ROWS), MXU_ROWS)
        ch, cl, sh, sl = ch_ref[rows, :], cl_ref[rows, :], sh_ref[rows, :], sl_ref[rows, :]
        re = d(ch, ph_s[...]) + d(cl, ph_s[...]) + d(ch, pl_s[...])
        im = d(sh, qh_s[...]) + d(sl, qh_s[...]) + d(sh, ql_s[...])
        out_ref[rows, :] = re - im
        return carry
    lax.fori_loop(0, t // MXU_ROWS, sequence_dft, 0)


def _dft_tables(n, scale):
    j = lax.broadcasted_iota(jnp.int32, (n, n), 0)
    k = lax.broadcasted_iota(jnp.int32, (n, n), 1)
    ang = ((j * k) % n).astype(F32) * (2.0 * math.pi / n)
    out = []
    for tab in (jnp.cos(ang) * scale, jnp.sin(ang) * scale):
        hi = tab.astype(BF16)
        out += [hi, (tab - hi.astype(F32)).astype(BF16)]
    return out


def fnet_tables(t):
    seq = _dft_tables(t, t ** -0.5)
    grp = _dft_tables(FOURIER_GROUP_W, FOURIER_GROUP_W ** -0.5)
    n_grp = BRANCH_W // FOURIER_GROUP_W
    grp = [jnp.kron(jnp.eye(n_grp, dtype=F32), g.astype(F32)).astype(BF16) for g in grp]
    return seq + grp


def fnet_mixer(z, tables):
    b, t, _ = z.shape
    w = BRANCH_W
    const = functools.partial(_cspec, pipeline_mode=pl.Buffered(1))
    return pl.pallas_call(
        functools.partial(_fnet_body, t),
        grid=(b,),
        in_specs=[_zspec(t, w, FNET_BLOCK)] + [const((t, t))] * 4 + [const((w, w))] * 4,
        out_specs=pl.BlockSpec((None, t, w), lambda i: (i, 0, 0)),
        out_shape=jax.ShapeDtypeStruct((b, t, w), F32),
        scratch_shapes=[pltpu.VMEM((t, w), BF16)] * 4,
        compiler_params=pltpu.CompilerParams(dimension_semantics=("parallel",), vmem_limit_bytes=FNET_VMEM_LIMIT),
        name="fnet_mixer",
    )(z, *tables)


def _ln_rows(x):
    mu = jnp.mean(x, axis=-1, keepdims=True)
    xc = x - mu
    return xc * lax.rsqrt(jnp.mean(xc * xc, axis=-1, keepdims=True) + NORM_EPS)


_ROW_CONST = functools.partial(_cspec, pipeline_mode=pl.Buffered(1))


def _row_spec(width):
    return pl.BlockSpec((MXU_ROWS, width), lambda i: (i, 0))


def _sample_spec(t, d):
    return pl.BlockSpec((None, 1, d), lambda i: (i // (t // MXU_ROWS), 0, 0))


def _adaln_body(c_ref, w_ref, b_ref, o_ref):
    cc = c_ref[...]
    act = (cc * _sigmoid(cc)).astype(BF16)
    o_ref[...] = jnp.dot(act, w_ref[...].astype(BF16), preferred_element_type=F32) + b_ref[...]


def adaln_modulation(cond, ada_w, ada_b):
    r, d = cond.shape
    n_layers, _, n = ada_w.shape
    tn = n // 4
    return pl.pallas_call(
        _adaln_body,
        grid=(n_layers, n // tn),
        in_specs=[pl.BlockSpec((r, d), lambda l, j: (0, 0)), pl.BlockSpec((None, d, tn), lambda l, j: (l, 0, j)),
                  pl.BlockSpec((None, 1, tn), lambda l, j: (l, 0, j))],
        out_specs=pl.BlockSpec((None, r, tn), lambda l, j: (l, 0, j)),
        out_shape=jax.ShapeDtypeStruct((n_layers, r, n), F32),
        compiler_params=pltpu.CompilerParams(dimension_semantics=("parallel", "parallel"), vmem_limit_bytes=VMEM_LIMIT),
        name="adaln",
    )(cond, ada_w, ada_b[:, None, :])


def _modproj_body(x_ref, sh_ref, sc_ref, w_ref, z_ref, h_ref):
    h = (_ln_rows(x_ref[...]) * (1.0 + sc_ref[...]) + sh_ref[...]).astype(BF16)
    h_ref[...] = h
    z_ref[...] = jnp.dot(h, w_ref[...], preferred_element_type=F32)


def modulated_project(x, shift, scale, w, t):
    m, d = x.shape
    n = w.shape[1]
    return pl.pallas_call(
        _modproj_body,
        grid=(m // MXU_ROWS,),
        in_specs=[_row_spec(d), _sample_spec(t, d), _sample_spec(t, d), _ROW_CONST((d, n))],
        out_specs=[_row_spec(n), _row_spec(d)],
        out_shape=[jax.ShapeDtypeStruct((m, n), F32), jax.ShapeDtypeStruct((m, d), BF16)],
        compiler_params=_MIXER_PARAMS,
        name="modulated_project",
    )(x, shift, scale, w)


def _merge_body(h_ref, oa_ref, ob_ref, oc_ref, od_ref, x_ref, gate1_ref, sh2_ref, sc2_ref, lng_ref, lnb_ref,
                wg_ref, bg_ref, wb_ref, wo_ref, wr_ref, xo_ref, h2_ref, aff_ref):
    h = h_ref[...]
    acc = None
    for g, o_ref in enumerate((oa_ref, ob_ref, oc_ref, od_ref)):
        gate = _sigmoid(jnp.dot(h, wg_ref[g], preferred_element_type=F32) + bg_ref[g])
        t = gate * jnp.dot(o_ref[...].astype(BF16), wb_ref[g], preferred_element_type=F32)
        acc = t if acc is None else acc + t
    y = jnp.dot(acc.astype(BF16), wo_ref[...], preferred_element_type=F32)
    x_new = _ln_rows(DEEPNORM_ALPHA * x_ref[...] + gate1_ref[...] * y) * lng_ref[...] + lnb_ref[...]
    xo_ref[...] = x_new
    h2 = (_ln_rows(x_new) * (1.0 + sc2_ref[...]) + sh2_ref[...]).astype(BF16)
    h2_ref[...] = h2
    logits = jnp.dot(h2, wr_ref[...], preferred_element_type=F32)
    valid = lax.broadcasted_iota(jnp.int32, logits.shape, 1) < N_EXPERTS
    logits = jnp.where(valid, logits, NEG_BIG)
    e = jnp.exp(logits - jnp.max(logits, axis=-1, keepdims=True))
    aff_ref[...] = e / jnp.sum(e, axis=-1, keepdims=True)


def merge_and_norm(h, outs, x, mod, ln_g, ln_b, p, t):
    m, d = x.shape
    w = outs[0].shape[1]
    gate1, shift2, scale2 = mod
    vec = _ROW_CONST((1, d))
    return pl.pallas_call(
        _merge_body,
        grid=(m // MXU_ROWS,),
        in_specs=[_row_spec(d)] + [_row_spec(w)] * N_BRANCH + [_row_spec(d)] + [_sample_spec(t, d)] * 3 + [vec, vec,
                  _ROW_CONST(p["w_gate"].shape), _ROW_CONST(p["b_gate"].shape), _ROW_CONST(p["w_branch"].shape),
                  _ROW_CONST(p["w_out"].shape), _ROW_CONST(p["w_router"].shape)],
        out_specs=[_row_spec(d), _row_spec(d), _row_spec(LANE)],
        out_shape=[jax.ShapeDtypeStruct((m, d), F32), jax.ShapeDtypeStruct((m, d), BF16),
                   jax.ShapeDtypeStruct((m, LANE), F32)],
        compiler_params=_MIXER_PARAMS,
        name="merge_and_norm",
    )(h, *outs, x, gate1, shift2, scale2, ln_g[None, :], ln_b[None, :],
      p["w_gate"], p["b_gate"], p["w_branch"], p["w_out"], p["w_router"])


def _expert_body(x_ref, wg_ref, wu_ref, wd_ref, y_ref):
    bb, _, cap, d = x_ref.shape
    x = x_ref[...].reshape(bb * cap, d).astype(BF16)
    gate = jnp.dot(x, wg_ref[0], preferred_element_type=F32)
    up = jnp.dot(x, wu_ref[0], preferred_element_type=F32)
    hid = (gate * jax.nn.sigmoid(gate) * up).astype(BF16)
    y_ref[...] = jnp.dot(hid, wd_ref[0], preferred_element_type=F32).reshape(y_ref.shape)


def expert_swiglu(xe, w_ff_gate, w_ff_up, w_ff_down):
    b, e, cap, d = xe.shape
    f = w_ff_gate.shape[-1]
    bb = min(b, max(1, MXU_ROWS // cap))
    x_spec = pl.BlockSpec((bb, 1, cap, d), lambda ei, bi: (bi, ei, 0, 0))
    w_spec = lambda shape: pl.BlockSpec((1,) + shape, lambda ei, bi: (ei, 0, 0))
    return pl.pallas_call(
        _expert_body,
        grid=(e, b // bb),
        in_specs=[x_spec, w_spec((d, f)), w_spec((d, f)), w_spec((f, d))],
        out_specs=x_spec,
        out_shape=jax.ShapeDtypeStruct(xe.shape, F32),
        compiler_params=pltpu.CompilerParams(dimension_semantics=("parallel", "parallel"),
                                             vmem_limit_bytes=VMEM_LIMIT),
        name="expert_swiglu",
    )(xe, w_ff_gate, w_ff_up, w_ff_down)


def _layer_norm(x):
    xf = x.astype(F32)
    mu = jnp.mean(xf, axis=-1, keepdims=True)
    var = jnp.mean(jnp.square(xf - mu), axis=-1, keepdims=True)
    return (xf - mu) * lax.rsqrt(var + NORM_EPS)


def post_norm(x, g, b):
    return (_layer_norm(x) * g + b).astype(x.dtype)


def modulate(x, shift, scale):
    return (_layer_norm(x) * (1.0 + scale) + shift).astype(x.dtype)


def sincos_grid(rows, cols, dim):
    quarter = dim // 4
    omega = 1.0 / (10000.0 ** (jnp.arange(quarter, dtype=F32) / quarter))
    er = jnp.arange(rows, dtype=F32)[:, None] * omega
    ec = jnp.arange(cols, dtype=F32)[:, None] * omega
    er = jnp.concatenate([jnp.sin(er), jnp.cos(er)], axis=-1)
    ec = jnp.concatenate([jnp.sin(ec), jnp.cos(ec)], axis=-1)
    emb = jnp.concatenate([jnp.broadcast_to(er[:, None, :], (rows, cols, dim // 2)),
                           jnp.broadcast_to(ec[None, :, :], (rows, cols, dim // 2))], axis=-1)
    return emb.reshape(rows * cols, dim)


def hgrn_lower_bounds(logits):
    cum = jnp.cumsum(jax.nn.softmax(logits.astype(F32), axis=1), axis=1)
    return cum - cum[:, :1]


def _flat(a):
    return a.reshape(-1, a.shape[-1])


def token_mixer(xc, xl, mod_c, mod_l, p, fnet_tabs, b, with_ctx):
    tc, tl = xc.shape[0] // b, xl.shape[0] // b
    zc, hc = modulated_project(xc, mod_c[0], mod_c[1], p["w_in"], tc)
    zl, hl = modulated_project(xl, mod_l[0], mod_l[1], p["w_in"], tl)
    zc, zl = zc.reshape(b, tc, Z_COLS), zl.reshape(b, tl, Z_COLS)
    a_c, a_l = hgrn_mixer(zc, zl, p["lb"], p["hgrn_norm_w"])
    b_c, b_l = gdn_mixer(zc, zl, p["gdn_conv_w"], p["gdn_a_log"], p["gdn_dt_bias"], p["gdn_norm_w"])
    d_c, d_l = ssd_mixer(zc, zl, p["ssd_conv_w"], p["ssd_conv_b"], p["ssd_a_log"], p["ssd_dt_bias"],
                         p["ssd_d"], p["ssd_norm_w"])

    def merged(h, outs, x, mod, t):
        return merge_and_norm(h, tuple(_flat(o) for o in outs), x, mod[2:5], p["ln1_g"], p["ln1_b"], p, t)

    out_l = merged(hl, (a_l, b_l, fnet_mixer(zl, fnet_tabs[1]), d_l), xl, mod_l, tl)
    out_c = merged(hc, (a_c, b_c, fnet_mixer(zc, fnet_tabs[0]), d_c), xc, mod_c, tc) if with_ctx else None
    return out_c, out_l


def expert_choice_ffn(h, aff, b, w_ff_gate, w_ff_up, w_ff_down):
    t_ = h.shape[0] // b
    h = h.reshape(b, t_, -1)
    cap = EC_CAPACITY_FACTOR * t_ // N_EXPERTS
    aff = aff[:, :N_EXPERTS].reshape(b, t_, N_EXPERTS)
    weight, idx = lax.top_k(jnp.swapaxes(aff, 1, 2), cap)
    bidx = jnp.arange(b)[:, None, None]
    xe = h[bidx, idx]
    ye = expert_swiglu(xe, w_ff_gate, w_ff_up, w_ff_down) * weight[..., None]
    return jnp.zeros(h.shape, F32).at[bidx, idx].add(ye)


def kernel(x, c, ctx, c_ctx, ada_w, ada_b, w_in, hgrn_lb_logits, hgrn_norm_w,
           gdn_conv_w, gdn_a_log, gdn_dt_bias, gdn_norm_w,
           ssd_conv_w, ssd_conv_b, ssd_a_log, ssd_dt_bias, ssd_d, ssd_norm_w,
           w_gate, b_gate, w_branch, w_out, ln1_g, ln1_b,
           w_router, w_ff_gate, w_ff_up, w_ff_down, ln2_g, ln2_b):
    b, n_lat, d = x.shape
    n_ctx = ctx.shape[1]
    rows = n_lat // GRID_W
    xl = _flat(x + sincos_grid(rows, GRID_W, D_MODEL).astype(x.dtype))
    xc = _flat(ctx)
    lb_all = hgrn_lower_bounds(hgrn_lb_logits)
    fnet_tabs = (fnet_tables(n_ctx), fnet_tables(n_lat))
    cond = jnp.concatenate([c, c_ctx[None, :], jnp.zeros((-(b + 1) % SUBLANE, d), c.dtype)], axis=0)
    mod_all = adaln_modulation(cond, ada_w, ada_b)
    for l in range(DEPTH):
        with_ctx = l < DEPTH - 1
        mod_l = [m[:, None, :] for m in jnp.split(mod_all[l, :b], 6, axis=-1)]
        mod_c = [jnp.broadcast_to(m[None, :, :], (b, 1, d)) for m in jnp.split(mod_all[l, b:b + 1], 6, axis=-1)]
        p = {
            "ln1_g": ln1_g[l], "ln1_b": ln1_b[l],
            "w_router": jnp.pad(w_router[l], ((0, 0), (0, LANE - N_EXPERTS))).astype(BF16),
            "w_in": permute_w_in(w_in[l]).astype(BF16), "lb": lb_all[:, l], "hgrn_norm_w": hgrn_norm_w[l],
            "gdn_conv_w": gdn_conv_w[l], "gdn_a_log": gdn_a_log[l], "gdn_dt_bias": gdn_dt_bias[l],
            "gdn_norm_w": gdn_norm_w[l], "ssd_conv_w": ssd_conv_w[l], "ssd_conv_b": ssd_conv_b[l],
            "ssd_a_log": ssd_a_log[l], "ssd_dt_bias": ssd_dt_bias[l], "ssd_d": ssd_d[l], "ssd_norm_w": ssd_norm_w[l],
            "w_gate": w_gate[l].astype(BF16), "b_gate": b_gate[l][:, None, :],
            "w_branch": w_branch[l].astype(BF16), "w_out": w_out[l].astype(BF16),
        }
        ffw = (w_ff_gate[l].astype(BF16), w_ff_up[l].astype(BF16), w_ff_down[l].astype(BF16))
        out_c, out_l = token_mixer(xc, xl, mod_c, mod_l, p, fnet_tabs, b, with_ctx)

        def channel_mixer(out, gate2, t):
            x_mid, h2, aff = out
            moe = expert_choice_ffn(h2, aff, b, *ffw)
            x_new = post_norm(DEEPNORM_ALPHA * x_mid.reshape(b, t, d) + gate2 * moe, ln2_g[l], ln2_b[l])
            return _flat(x_new)

        xl = channel_mixer(out_l, mod_l[5], n_lat)
        if with_ctx:
            xc = channel_mixer(out_c, mod_c[5], n_ctx)
    return xl.reshape(b, n_lat, d)
```

```python
import functools
import math

import jax
import jax.numpy as jnp
import numpy as np
from jax import lax
from jax.experimental import pallas as pl
from jax.experimental.pallas import tpu as pltpu

D_MODEL = 1024
DEPTH = 4
GRID_W = 64
N_BRANCH = 4
BRANCH_W = D_MODEL // N_BRANCH
HEAD_DIM = 64
N_HEADS = BRANCH_W // HEAD_DIM
FOURIER_GROUP_W = HEAD_DIM
CHUNK = 64
CONV_K = 5
SSD_STATE = 64
SSD_GROUPS = 2
N_EXPERTS = 16
EC_CAPACITY_FACTOR = 2
DEEPNORM_ALPHA = (2.0 * DEPTH) ** 0.25
NORM_EPS = 1e-6

LANE = 128
SUBLANE = 8
MXU_ROWS = 256
VMEM_LIMIT = 48 * 1024 * 1024
FNET_VMEM_LIMIT = 56 * 1024 * 1024
NEG_BIG = -1e30

F32 = jnp.float32
BF16 = jnp.bfloat16

XBC_W = BRANCH_W + 2 * SSD_GROUPS * SSD_STATE
Z_ORDER = ("b_qkv", "a_q", "a_f_fwd", "a_f_bwd", "a_v", "a_g", "b_g", "c_u", "d_xbc", "d_z", "b_a", "b_beta", "d_dt")
REF_SPLITS = (
    ("a_q", BRANCH_W), ("a_f_fwd", BRANCH_W), ("a_f_bwd", BRANCH_W), ("a_v", BRANCH_W), ("a_g", BRANCH_W),
    ("b_qkv", 3 * BRANCH_W), ("b_g", BRANCH_W), ("b_a", 2 * N_HEADS), ("b_beta", 2 * N_HEADS),
    ("c_u", BRANCH_W), ("d_xbc", XBC_W), ("d_z", BRANCH_W), ("d_dt", 2 * N_HEADS),
)
Z_COLS = 3 * BRANCH_W + 5 * BRANCH_W + 2 * BRANCH_W + XBC_W + BRANCH_W + LANE
GDN_BLOCKS = (0, 8, 26)
HGRN_BLOCKS = (3, 4, 5, 6, 7)
FNET_BLOCK = 9
SSD_BLOCKS = (5, 12, 26)
DT_LANE0 = 4 * N_HEADS


def permute_w_in(w_in):
    start, spans = 0, {}
    for name, size in REF_SPLITS:
        spans[name] = (start, start + size)
        start += size
    parts = [w_in[:, spans[n][0]:spans[n][1]] for n in Z_ORDER]
    used = sum(p.shape[1] for p in parts)
    return jnp.concatenate(parts + [jnp.zeros((w_in.shape[0], Z_COLS - used), w_in.dtype)], axis=1)


def _bdot(a, b):
    return jnp.dot(a.astype(BF16), b.astype(BF16), preferred_element_type=F32)


def _bdot_nt(a, b):
    return lax.dot_general(a.astype(BF16), b.astype(BF16), (((1,), (1,)), ((), ())), preferred_element_type=F32)


def _bdot_tn(a, b):
    return lax.dot_general(a.astype(BF16), b.astype(BF16), (((0,), (0,)), ((), ())), preferred_element_type=F32)


def _split(x, terms):
    out = []
    for _ in range(terms):
        p = x.astype(BF16)
        out.append(p)
        x = x - p.astype(F32)
    return out


def _dot_exact_lhs(m, x, terms=3):
    mb = m.astype(BF16)
    return sum(jnp.dot(mb, p, preferred_element_type=F32) for p in _split(x, terms))


def _dot_exact_rhs(x, m, terms=2):
    mb = m.astype(BF16)
    return sum(jnp.dot(p, mb, preferred_element_type=F32) for p in _split(x, terms))


def _dot_x3(a, b):
    ah, al = _split(a, 2)
    bh, bl = _split(b, 2)
    d = lambda p, q: jnp.dot(p, q, preferred_element_type=F32)
    return d(ah, bh) + d(al, bh) + d(ah, bl)


def _iota2(n, m):
    return lax.broadcasted_iota(jnp.int32, (n, m), 0), lax.broadcasted_iota(jnp.int32, (n, m), 1)


def _order_masks(d):
    i, j = _iota2(CHUNK, CHUNK)
    return ((j <= i), (j < i)) if d == 0 else ((j >= i), (j > i))


def _sigmoid(x):
    return 1.0 / (1.0 + jnp.exp(-x))


def _softplus(x):
    return jnp.maximum(x, 0.0) + jnp.log(1.0 + jnp.exp(-jnp.abs(x)))


def _log_sigmoid(x):
    return jnp.minimum(x, 0.0) - jnp.log(1.0 + jnp.exp(-jnp.abs(x)))


def _logaddexp(a, b):
    m = jnp.maximum(a, b)
    return m + jnp.log(jnp.exp(a - m) + jnp.exp(b - m))


def _heads(x):
    return [x[:, h * HEAD_DIM:(h + 1) * HEAD_DIM] for h in range(N_HEADS)]


def _conv(x_ref, w_ref, n, n_chunks):
    t = n_chunks * CHUNK
    start = pl.multiple_of(n * CHUNK, CHUNK)
    cur = x_ref[pl.ds(start, CHUNK), :]
    prev_start = pl.multiple_of(jnp.maximum(start - SUBLANE, 0), SUBLANE)
    next_start = pl.multiple_of(jnp.minimum(start + CHUNK, t - SUBLANE), SUBLANE)
    prev = x_ref[pl.ds(prev_start, SUBLANE), :] * jnp.where(n > 0, 1.0, 0.0)
    nxt = x_ref[pl.ds(next_start, SUBLANE), :] * jnp.where(n < n_chunks - 1, 1.0, 0.0)
    ext = jnp.concatenate([prev, cur, nxt], axis=0)
    pad = CONV_K // 2
    acc = None
    n_ext = CHUNK + 2 * SUBLANE
    for k in range(CONV_K):
        shifted = ext if k == pad else pltpu.roll(ext, (pad - k) % n_ext, 0)
        term = shifted[SUBLANE:SUBLANE + CHUNK, :] * w_ref[k:k + 1, :]
        acc = term if acc is None else acc + term
    return acc


PREP_UNROLL = 2


def _interleave(progs):
    live = list(progs)
    while live:
        nxt = []
        for p in live:
            try:
                next(p)
                nxt.append(p)
            except StopIteration:
                pass
        live = nxt


def _chunk_loop(prog, n_chunks):
    def body(i, carry):
        _interleave([prog(i * PREP_UNROLL + j) for j in range(PREP_UNROLL)])
        return carry
    lax.fori_loop(0, n_chunks // PREP_UNROLL, body, 0)


def _backward_chunk(step, nc_c, n_tot):
    return jnp.where(step < nc_c, nc_c - 1 - step, n_tot - 1 - (step - nc_c))


def _chunk_rows(n):
    return pl.ds(pl.multiple_of(n * CHUNK, CHUNK), CHUNK)


def _head_norm_gate(o_s, gate_ref, out_ref, normw, bdm, n_chunks, base):
    def body(n, carry):
        rows = pl.ds(pl.multiple_of(n * MXU_ROWS, MXU_ROWS), MXU_ROWS)
        o = o_s[pl.ds(pl.multiple_of(base + n * MXU_ROWS, MXU_ROWS), MXU_ROWS), :]
        ms = _dot_exact_rhs(o * o, bdm) * (1.0 / HEAD_DIM)
        gt = gate_ref[rows, :]
        out_ref[rows, :] = o * lax.rsqrt(ms + NORM_EPS) * normw[...] * (gt * _sigmoid(gt))
        return carry
    lax.fori_loop(0, n_chunks * CHUNK // MXU_ROWS, body, 0)


def head_block_ones():
    i = np.arange(BRANCH_W)
    return jnp.asarray((i[:, None] // HEAD_DIM) == (i[None, :] // HEAD_DIM), BF16)


def _zspec(t, width, blk, **kw):
    return pl.BlockSpec((None, t, width), lambda i: (i, 0, blk), **kw)


def _cspec(shape, **kw):
    return pl.BlockSpec(shape, lambda *_: (0,) * len(shape), **kw)


def _mixer_out(b, tc, tl):
    assert tc % MXU_ROWS == 0 and tl % MXU_ROWS == 0 and MXU_ROWS % (PREP_UNROLL * CHUNK) == 0
    specs = [pl.BlockSpec((None, tc, BRANCH_W), lambda i: (i, 0, 0)), pl.BlockSpec((None, tl, BRANCH_W), lambda i: (i, 0, 0))]
    shapes = [jax.ShapeDtypeStruct((b, tc, BRANCH_W), F32), jax.ShapeDtypeStruct((b, tl, BRANCH_W), F32)]
    return specs, shapes


_MIXER_PARAMS = pltpu.CompilerParams(dimension_semantics=("parallel",), vmem_limit_bytes=VMEM_LIMIT)


def _unit_lower_inverses(mats):
    i, j = _iota2(CHUNK, CHUNK)
    eye = (i == j).astype(F32)
    same4 = (i // 4) == (j // 4)
    d4 = [jnp.where(same4, a, 0.0) for a in mats]
    sq = [_bdot(d, d) for d in d4]
    xs = [eye - d for d in d4]
    yield
    xs = [x + _bdot(x, q) for x, q in zip(xs, sq)]
    yield
    s = 4
    while s < CHUNK:
        sel = ((i // (2 * s)) == (j // (2 * s))) & ((i // s) != (j // s))
        ox = [_bdot(jnp.where(sel, a, 0.0), x) for a, x in zip(mats, xs)]
        yield
        xs = [x - _bdot(x, y) for x, y in zip(xs, ox)]
        yield
        s *= 2
    return xs


def _gdn_body(nc_c, nc_l, qkv_c, qkv_l, gate_c, gate_l, sm_c, sm_l, convw, prow, normw, bd,
              out_c, out_l, u_s, w_s, qk_s, qd_s, kd_s, gl_s, o_s, st_s):
    w = BRANCH_W
    bdm = bd[...]
    lane = lax.broadcasted_iota(jnp.int32, (CHUNK, LANE), 1)
    ii, jj = _iota2(CHUNK, CHUNK)
    incl_lower = (jj <= ii).astype(F32)
    masks = [_order_masks(d) for d in range(2)]
    dh = [(d, h) for d in range(2) for h in range(N_HEADS)]
    n_tot = nc_c + nc_l

    def prep(x_ref, s_ref, n_chunks, base):
        def prog(n):
            y = _conv(x_ref, convw, n, n_chunks)
            y = y * _sigmoid(y)
            q, k, v = y[:, :w], y[:, w:2 * w], y[:, 2 * w:]
            qss, kss = _dot_exact_rhs(q * q, bdm), _dot_exact_rhs(k * k, bdm)
            yield
            q = q * lax.rsqrt(qss + NORM_EPS) * HEAD_DIM ** -0.5
            k = k * lax.rsqrt(kss + NORM_EPS)
            sm = s_ref[_chunk_rows(n), :]
            la = jnp.where(lane < 2 * N_HEADS, -jnp.exp(prow[0:1, :]) * _softplus(sm + prow[1:2, :]), 0.0)
            beta_all = _sigmoid(sm)
            prefix = _dot_exact_lhs(incl_lower, la)
            total = jnp.sum(la, axis=0, keepdims=True)
            g_all = jnp.where(lane < N_HEADS, prefix, total - prefix + la)
            g_t = g_all.T
            qh, kh, vh = _heads(q), _heads(k), _heads(v)
            qkk = [_bdot_nt(jnp.concatenate([qh[h], kh[h]], axis=0), kh[h]) for h in range(N_HEADS)]
            yield
            g_col = [g_all[:, d * N_HEADS + h:d * N_HEADS + h + 1] for d, h in dh]
            beta = [beta_all[:, (2 + d) * N_HEADS + h:(2 + d) * N_HEADS + h + 1] for d, h in dh]
            gl = [total[:, d * N_HEADS + h:d * N_HEADS + h + 1] for d, h in dh]
            decay = [jnp.exp(jnp.where(masks[d][0], g_col[x] - g_t[x:x + 1, :], NEG_BIG)) for x, (d, h) in enumerate(dh)]
            qk = [qkk[h][:CHUNK] * decay[x] for x, (d, h) in enumerate(dh)]
            a = [jnp.where(masks[d][1], beta[x] * qkk[h][CHUNK:] * decay[x], 0.0) for x, (d, h) in enumerate(dh)]
            tinv = yield from _unit_lower_inverses(a)
            eg = [jnp.exp(g) for g in g_col]
            rhs = [jnp.concatenate([beta[x] * vh[h], beta[x] * kh[h] * eg[x]], axis=1) for x, (d, h) in enumerate(dh)]
            uw = [_bdot(t, r) for t, r in zip(tinv, rhs)]
            yield
            resid = [r - y0 - _dot_x3(m, y0) for r, y0, m in zip(rhs, uw, a)]
            yield
            uw = [y0 + _bdot(t, r) for y0, t, r in zip(uw, tinv, resid)]
            yield
            qd = [qh[h] * eg[x] for x, (d, h) in enumerate(dh)]
            kd = [kh[h] * jnp.exp(gl[x] - g_col[x]) for x, (d, h) in enumerate(dh)]
            rows = pl.ds(pl.multiple_of(base + n * CHUNK, CHUNK), CHUNK)
            for d in range(2):
                sel = slice(d * N_HEADS, (d + 1) * N_HEADS)
                u_s[d, rows, :] = jnp.concatenate([y1[:, :HEAD_DIM] for y1 in uw[sel]], axis=1)
                w_s[d, rows, :] = jnp.concatenate([y1[:, HEAD_DIM:] for y1 in uw[sel]], axis=1).astype(BF16)
                qk_s[d, rows, :] = jnp.concatenate(qk[sel], axis=1).astype(BF16)
                qd_s[d, rows, :] = jnp.concatenate(qd[sel], axis=1).astype(BF16)
                kd_s[d, rows, :] = jnp.concatenate(kd[sel], axis=1).astype(BF16)
            gl_s[pl.ds(base // CHUNK + n, 1), :] = jnp.exp(total)
            yield
        _chunk_loop(prog, n_chunks)

    prep(qkv_c, sm_c, nc_c, 0)
    prep(qkv_l, sm_l, nc_l, nc_c * CHUNK)

    st_s[...] = jnp.zeros_like(st_s)
    o_s[...] = jnp.zeros_like(o_s)

    def scan_body(step, carry):
        n_dir = (step, _backward_chunk(step, nc_c, n_tot))
        rows = [_chunk_rows(n) for n in n_dir]
        u = [t for d in range(2) for t in _heads(u_s[d, rows[d], :])]
        wm = [t for d in range(2) for t in _heads(w_s[d, rows[d], :])]
        qk = [t for d in range(2) for t in _heads(qk_s[d, rows[d], :])]
        qd = [t for d in range(2) for t in _heads(qd_s[d, rows[d], :])]
        kd = [t for d in range(2) for t in _heads(kd_s[d, rows[d], :])]
        egl = [gl_s[pl.ds(n_dir[d], 1), :][:, d * N_HEADS + h:d * N_HEADS + h + 1] for d, h in dh]
        s_prev = [st_s[x] for x in range(len(dh))]
        v_new = [u[x] - _bdot(wm[x], s_prev[x]) for x in range(len(dh))]
        o = [_bdot(qd[x], s_prev[x]) + _bdot(qk[x], v_new[x]) for x in range(len(dh))]
        for x in range(len(dh)):
            st_s[x] = egl[x] * s_prev[x] + _bdot_tn(kd[x], v_new[x])
        for d in range(2):
            o_s[rows[d], :] = o_s[rows[d], :] + jnp.concatenate(o[d * N_HEADS:(d + 1) * N_HEADS], axis=1)
        return carry
    lax.fori_loop(0, n_tot, scan_body, 0)

    _head_norm_gate(o_s, gate_c, out_c, normw, bdm, nc_c, 0)
    _head_norm_gate(o_s, gate_l, out_l, normw, bdm, nc_l, nc_c * CHUNK)


def gdn_mixer(zc, zl, conv_w, a_log, dt_bias, norm_w):
    b, tc, _ = zc.shape
    tl = zl.shape[1]
    nc_c, nc_l = tc // CHUNK, tl // CHUNK
    w = BRANCH_W
    qkv_blk, gate_blk, sm_blk = GDN_BLOCKS
    prow = jnp.zeros((SUBLANE, LANE), F32)
    prow = prow.at[0, :2 * N_HEADS].set(a_log.reshape(-1)).at[1, :2 * N_HEADS].set(dt_bias.reshape(-1))
    out_specs, out_shape = _mixer_out(b, tc, tl)
    return pl.pallas_call(
        functools.partial(_gdn_body, nc_c, nc_l),
        grid=(b,),
        in_specs=[_zspec(tc, 3 * w, qkv_blk), _zspec(tl, 3 * w, qkv_blk, pipeline_mode=pl.Buffered(1)),
                  _zspec(tc, w, gate_blk), _zspec(tl, w, gate_blk),
                  _zspec(tc, LANE, sm_blk), _zspec(tl, LANE, sm_blk),
                  _cspec((SUBLANE, 3 * w)), _cspec((SUBLANE, LANE)), _cspec((1, w)), _cspec((w, w))],
        out_specs=out_specs,
        out_shape=out_shape,
        scratch_shapes=[pltpu.VMEM((2, tc + tl, w), F32)] + [pltpu.VMEM((2, tc + tl, w), BF16)] * 4 + [
            pltpu.VMEM((nc_c + nc_l, LANE), F32), pltpu.VMEM((tc + tl, w), F32),
            pltpu.VMEM((2 * N_HEADS, HEAD_DIM, HEAD_DIM), F32)],
        compiler_params=_MIXER_PARAMS,
        name="gdn_mixer",
    )(zc, zl, zc, zl, zc, zl,
      jnp.pad(conv_w, ((0, SUBLANE - CONV_K), (0, 0))), prow, norm_w[None, :], head_block_ones())


GLA_LEVELS = (32, 16, 8, 4, 2, 1)


def _gla_level_tables(d):
    n_lv = len(GLA_LEVELS)
    i, t = _iota2(n_lv * CHUNK, CHUNK)
    sel = jnp.zeros((n_lv * CHUNK, CHUNK), F32)
    r, c = _iota2(CHUNK, CHUNK)
    masks = []
    for x, s in enumerate(GLA_LEVELS):
        row = i - x * CHUNK
        bound = 2 * s * (row // (2 * s)) + s - 1 + d
        sel = jnp.where((i // CHUNK == x) & (t == bound), 1.0, sel)
        same = (r // (2 * s)) == (c // (2 * s))
        r_hi, c_hi = (r % (2 * s)) >= s, (c % (2 * s)) >= s
        masks.append(same & (r_hi & ~c_hi if d == 0 else ~r_hi & c_hi))
    return sel, masks


def _hgrn_body(nc_c, nc_l, q_c, q_l, ff_c, ff_l, fb_c, fb_l, v_c, v_l, gate_c, gate_l, lbrow, normw, bd,
               out_c, out_l, att_s, qd_s, kd_s, v_s, gl_s, o_s, st_s):
    bdm = bd[...]
    ii, jj = _iota2(CHUNK, CHUNK)
    incl_lower = (jj <= ii).astype(F32)
    eye = ii == jj
    tables = [_gla_level_tables(d) for d in range(2)]
    n_dh = 2 * N_HEADS
    n_tot = nc_c + nc_l

    def prep(q_ref, f_refs, v_ref, n_chunks, base):
        def prog(n):
            rin = _chunk_rows(n)
            rows = pl.ds(pl.multiple_of(base + n * CHUNK, CHUNK), CHUNK)
            zq = q_ref[rin, :]
            q = zq * _sigmoid(zq)
            v_s[rows, :] = v_ref[rin, :].astype(BF16)
            for d in range(2):
                zf = f_refs[d][rin, :]
                log_lb, log1m_lb, one_m_lb = lbrow[3 * d:3 * d + 1, :], lbrow[3 * d + 1:3 * d + 2, :], lbrow[3 * d + 2:3 * d + 3, :]
                lf = _logaddexp(log_lb, log1m_lb + _log_sigmoid(zf))
                k = one_m_lb * _sigmoid(-zf)
                prefix = _dot_exact_lhs(incl_lower, lf)
                diag = _dot_exact_rhs(q * k, bdm)
                yield
                total = jnp.sum(lf, axis=0, keepdims=True)
                g = prefix if d == 0 else total - prefix + lf
                sel, masks = tables[d]
                c_all = _dot_exact_lhs(sel, g)
                yield
                acc = [jnp.where(eye, diag[:, h * HEAD_DIM:h * HEAD_DIM + 1], 0.0) for h in range(N_HEADS)]
                for x in range(len(GLA_LEVELS)):
                    c = c_all[x * CHUNK:(x + 1) * CHUNK, :]
                    qt = _heads(q * jnp.exp(jnp.minimum(g - c, 0.0)))
                    kt = _heads(k * jnp.exp(jnp.minimum(c - g, 0.0)))
                    acc = [a + jnp.where(masks[x], _bdot_nt(qt[h], kt[h]), 0.0) for h, a in enumerate(acc)]
                    yield
                att_s[d, rows, :] = jnp.concatenate(acc, axis=1).astype(BF16)
                qd_s[d, rows, :] = (q * jnp.exp(g)).astype(BF16)
                kd_s[d, rows, :] = (k * jnp.exp(total - g)).astype(BF16)
                gl_s[d, pl.ds(base // CHUNK + n, 1), :] = jnp.exp(total)
        _chunk_loop(prog, n_chunks)

    prep(q_c, (ff_c, fb_c), v_c, nc_c, 0)
    prep(q_l, (ff_l, fb_l), v_l, nc_l, nc_c * CHUNK)

    st_s[...] = jnp.zeros_like(st_s)
    o_s[...] = jnp.zeros_like(o_s)

    def scan_body(step, carry):
        n_dir = (step, _backward_chunk(step, nc_c, n_tot))
        rows = [_chunk_rows(n) for n in n_dir]
        att = [t for d in range(2) for t in _heads(att_s[d, rows[d], :])]
        qd = [t for d in range(2) for t in _heads(qd_s[d, rows[d], :])]
        kd = [t for d in range(2) for t in _heads(kd_s[d, rows[d], :])]
        v = [t for d in range(2) for t in _heads(v_s[rows[d], :])]
        egl = [t for d in range(2) for t in _heads(gl_s[d, pl.ds(n_dir[d], 1), :])]
        s_prev = [st_s[x] for x in range(n_dh)]
        o = [_bdot(att[x], v[x]) + _bdot_nt(qd[x], s_prev[x]) for x in range(n_dh)]
        for x in range(n_dh):
            st_s[x] = egl[x] * s_prev[x] + _bdot_tn(v[x], kd[x])
        for d in range(2):
            o_s[rows[d], :] = o_s[rows[d], :] + jnp.concatenate(o[d * N_HEADS:(d + 1) * N_HEADS], axis=1)
        return carry
    lax.fori_loop(0, n_tot, scan_body, 0)

    _head_norm_gate(o_s, gate_c, out_c, normw, bdm, nc_c, 0)
    _head_norm_gate(o_s, gate_l, out_l, normw, bdm, nc_l, nc_c * CHUNK)


def hgrn_mixer(zc, zl, lb, norm_w):
    b, tc, _ = zc.shape
    tl = zl.shape[1]
    nc_c, nc_l = tc // CHUNK, tl // CHUNK
    w = BRANCH_W
    lbrow = jnp.zeros((SUBLANE, w), F32)
    for d in range(2):
        lbrow = lbrow.at[3 * d].set(jnp.log(lb[d])).at[3 * d + 1].set(jnp.log1p(-lb[d])).at[3 * d + 2].set(1.0 - lb[d])
    in_specs, args = [], []
    for blk in HGRN_BLOCKS:
        in_specs += [_zspec(tc, w, blk), _zspec(tl, w, blk, pipeline_mode=pl.Buffered(1))]
        args += [zc, zl]
    out_specs, out_shape = _mixer_out(b, tc, tl)
    return pl.pallas_call(
        functools.partial(_hgrn_body, nc_c, nc_l),
        grid=(b,),
        in_specs=in_specs + [_cspec((SUBLANE, w)), _cspec((1, w)), _cspec((w, w))],
        out_specs=out_specs,
        out_shape=out_shape,
        scratch_shapes=[pltpu.VMEM((2, tc + tl, w), BF16)] * 3 + [
            pltpu.VMEM((tc + tl, w), BF16), pltpu.VMEM((2, nc_c + nc_l, w), F32), pltpu.VMEM((tc + tl, w), F32),
            pltpu.VMEM((2 * N_HEADS, HEAD_DIM, HEAD_DIM), F32)],
        compiler_params=_MIXER_PARAMS,
        name="hgrn_mixer",
    )(*args, lbrow, norm_w[None, :], head_block_ones())


def _ssd_body(nc_c, nc_l, xbc_c, xbc_l, z_c, z_l, sm_c, sm_l, convw, convb, prow, dskip, normw,
              out_c, out_l, att_s, v_s, qd_s, kd_s, x_s, gl_s, o_s, st_s):
    w = BRANCH_W
    gw = SSD_GROUPS * SSD_STATE
    lane = lax.broadcasted_iota(jnp.int32, (CHUNK, LANE), 1)
    ii, jj = _iota2(CHUNK, CHUNK)
    incl_lower = (jj <= ii).astype(F32)
    masks = [_order_masks(d) for d in range(2)]
    dh = [(d, h) for d in range(2) for h in range(N_HEADS)]
    n_tot = nc_c + nc_l
    heads_per_group = N_HEADS // SSD_GROUPS

    def prep(x_ref, s_ref, n_chunks, base):
        def prog(n):
            rows = pl.ds(pl.multiple_of(base + n * CHUNK, CHUNK), CHUNK)
            y = _conv(x_ref, convw, n, n_chunks) + convb[...]
            y = y * _sigmoid(y)
            xs, bs, cs = y[:, :w], y[:, w:w + gw], y[:, w + gw:]
            x_s[rows, :] = xs
            sm = s_ref[_chunk_rows(n), :]
            dt_all = _softplus(sm + prow[1:2, :])
            in_dt = (lane >= DT_LANE0) & (lane < DT_LANE0 + 2 * N_HEADS)
            la = jnp.where(in_dt, -jnp.exp(prow[0:1, :]) * dt_all, 0.0)
            prefix = _dot_exact_lhs(incl_lower, la)
            total = jnp.sum(la, axis=0, keepdims=True)
            g_all = jnp.where(lane < DT_LANE0 + N_HEADS, prefix, total - prefix + la)
            g_t = g_all.T
            xh = _heads(xs)
            bg = [bs[:, g * SSD_STATE:(g + 1) * SSD_STATE] for g in range(SSD_GROUPS)]
            cg = [cs[:, g * SSD_STATE:(g + 1) * SSD_STATE] for g in range(SSD_GROUPS)]
            cb = [_bdot_nt(cg[g], bg[g]) for g in range(SSD_GROUPS)]
            yield
            att, v, qd, kd = [], [], [], []
            for d, h in dh:
                l = DT_LANE0 + d * N_HEADS + h
                grp = h // heads_per_group
                g_col = g_all[:, l:l + 1]
                decay = jnp.exp(jnp.where(masks[d][0], g_col - g_t[l:l + 1, :], NEG_BIG))
                att.append(cb[grp] * decay)
                v.append(xh[h] * dt_all[:, l:l + 1])
                qd.append(cg[grp] * jnp.exp(g_col))
                kd.append(bg[grp] * jnp.exp(total[:, l:l + 1] - g_col))
            for d in range(2):
                sel = slice(d * N_HEADS, (d + 1) * N_HEADS)
                att_s[d, rows, :] = jnp.concatenate(att[sel], axis=1).astype(BF16)
                v_s[d, rows, :] = jnp.concatenate(v[sel], axis=1).astype(BF16)
                qd_s[d, rows, :] = jnp.concatenate(qd[sel], axis=1).astype(BF16)
                kd_s[d, rows, :] = jnp.concatenate(kd[sel], axis=1).astype(BF16)
            gl_s[pl.ds(base // CHUNK + n, 1), :] = jnp.exp(total)
            yield
        _chunk_loop(prog, n_chunks)

    prep(xbc_c, sm_c, nc_c, 0)
    prep(xbc_l, sm_l, nc_l, nc_c * CHUNK)

    st_s[...] = jnp.zeros_like(st_s)
    o_s[...] = jnp.zeros_like(o_s)

    def scan_body(step, carry):
        n_dir = (step, _backward_chunk(step, nc_c, n_tot))
        rows = [_chunk_rows(n) for n in n_dir]
        att = [t for d in range(2) for t in _heads(att_s[d, rows[d], :])]
        v = [t for d in range(2) for t in _heads(v_s[d, rows[d], :])]
        qd = [t for d in range(2) for t in _heads(qd_s[d, rows[d], :])]
        kd = [t for d in range(2) for t in _heads(kd_s[d, rows[d], :])]
        egl = [gl_s[pl.ds(n_dir[d], 1), :][:, DT_LANE0 + d * N_HEADS + h:DT_LANE0 + d * N_HEADS + h + 1] for d, h in dh]
        s_prev = [st_s[x] for x in range(len(dh))]
        o = [_bdot(att[x], v[x]) + _bdot(qd[x], s_prev[x]) for x in range(len(dh))]
        for x in range(len(dh)):
            st_s[x] = egl[x] * s_prev[x] + _bdot_tn(kd[x], v[x])
        for d in range(2):
            o_s[rows[d], :] = o_s[rows[d], :] + jnp.concatenate(o[d * N_HEADS:(d + 1) * N_HEADS], axis=1)
        return carry
    lax.fori_loop(0, n_tot, scan_body, 0)

    def finish(z_ref, out_ref, n_chunks, base):
        def body(n, carry):
            rows_in = pl.ds(pl.multiple_of(base + n * MXU_ROWS, MXU_ROWS), MXU_ROWS)
            rows = pl.ds(pl.multiple_of(n * MXU_ROWS, MXU_ROWS), MXU_ROWS)
            zt = z_ref[rows, :]
            y = (o_s[rows_in, :] + dskip[...] * x_s[rows_in, :]) * (zt * _sigmoid(zt))
            ms = jnp.sum(y * y, axis=1, keepdims=True) * (1.0 / w)
            out_ref[rows, :] = y * lax.rsqrt(ms + NORM_EPS) * normw[...]
            return carry
        lax.fori_loop(0, n_chunks * CHUNK // MXU_ROWS, body, 0)

    finish(z_c, out_c, nc_c, 0)
    finish(z_l, out_l, nc_l, nc_c * CHUNK)


def ssd_mixer(zc, zl, conv_w, conv_b, a_log, dt_bias, d_skip, norm_w):
    b, tc, _ = zc.shape
    tl = zl.shape[1]
    nc_c, nc_l = tc // CHUNK, tl // CHUNK
    w = BRANCH_W
    xbc_blk, z_blk, sm_blk = SSD_BLOCKS
    prow = jnp.zeros((SUBLANE, LANE), F32)
    prow = prow.at[0, DT_LANE0:DT_LANE0 + 2 * N_HEADS].set(a_log.reshape(-1))
    prow = prow.at[1, DT_LANE0:DT_LANE0 + 2 * N_HEADS].set(dt_bias.reshape(-1))
    out_specs, out_shape = _mixer_out(b, tc, tl)
    return pl.pallas_call(
        functools.partial(_ssd_body, nc_c, nc_l),
        grid=(b,),
        in_specs=[_zspec(tc, XBC_W, xbc_blk), _zspec(tl, XBC_W, xbc_blk, pipeline_mode=pl.Buffered(1)),
                  _zspec(tc, w, z_blk), _zspec(tl, w, z_blk),
                  _zspec(tc, LANE, sm_blk), _zspec(tl, LANE, sm_blk),
                  _cspec((SUBLANE, XBC_W)), _cspec((1, XBC_W)), _cspec((SUBLANE, LANE)), _cspec((1, w)), _cspec((1, w))],
        out_specs=out_specs,
        out_shape=out_shape,
        scratch_shapes=[pltpu.VMEM((2, tc + tl, w), BF16)] * 4 + [
            pltpu.VMEM((tc + tl, w), F32), pltpu.VMEM((nc_c + nc_l, LANE), F32), pltpu.VMEM((tc + tl, w), F32),
            pltpu.VMEM((2 * N_HEADS, SSD_STATE, HEAD_DIM), F32)],
        compiler_params=_MIXER_PARAMS,
        name="ssd_mixer",
    )(zc, zl, zc, zl, zc, zl,
      jnp.pad(conv_w, ((0, SUBLANE - CONV_K), (0, 0))), conv_b[None, :], prow,
      jnp.repeat(d_skip, HEAD_DIM)[None, :], norm_w[None, :])


def _fnet_body(t, u_ref, ch_ref, cl_ref, sh_ref, sl_ref, gch_ref, gcl_ref, gsh_ref, gsl_ref, out_ref,
               ph_s, pl_s, qh_s, ql_s):
    d = lambda a, b: jnp.dot(a, b, preferred_element_type=F32)

    def channel_dft(n, carry):
        rows = pl.ds(pl.multiple_of(n * MXU_ROWS, MXU_ROWS), MXU_ROWS)
        uh, ul = _split(u_ref[rows, :], 2)
        p = d(uh, gch_ref[...]) + d(ul, gch_ref[...]) + d(uh, gcl_ref[...])
        q = d(uh, gsh_ref[...]) + d(ul, gsh_ref[...]) + d(uh, gsl_ref[...])
        ph_s[rows, :], pl_s[rows, :] = _split(p, 2)
        qh_s[rows, :], ql_s[rows, :] = _split(q, 2)
        return carry
    lax.fori_loop(0, t // MXU_ROWS, channel_dft, 0)

    def sequence_dft(n, carry):
        rows = pl.ds(pl.multiple_of(n * MXU_ROWS, MXU_ROWS), MXU_ROWS)
        ch, cl, sh, sl = ch_ref[rows, :], cl_ref[rows, :], sh_ref[rows, :], sl_ref[rows, :]
        re = d(ch, ph_s[...]) + d(cl, ph_s[...]) + d(ch, pl_s[...])
        im = d(sh, qh_s[...]) + d(sl, qh_s[...]) + d(sh, ql_s[...])
        out_ref[rows, :] = re - im
        return carry
    lax.fori_loop(0, t // MXU_ROWS, sequence_dft, 0)


def _dft_tables(n, scale):
    j = lax.broadcasted_iota(jnp.int32, (n, n), 0)
    k = lax.broadcasted_iota(jnp.int32, (n, n), 1)
    ang = ((j * k) % n).astype(F32) * (2.0 * math.pi / n)
    out = []
    for tab in (jnp.cos(ang) * scale, jnp.sin(ang) * scale):
        hi = tab.astype(BF16)
        out += [hi, (tab - hi.astype(F32)).astype(BF16)]
    return out


def fnet_tables(t):
    seq = _dft_tables(t, t ** -0.5)
    grp = _dft_tables(FOURIER_GROUP_W, FOURIER_GROUP_W ** -0.5)
    n_grp = BRANCH_W // FOURIER_GROUP_W
    grp = [jnp.kron(jnp.eye(n_grp, dtype=F32), g.astype(F32)).astype(BF16) for g in grp]
    return seq + grp


def fnet_mixer(z, tables):
    b, t, _ = z.shape
    w = BRANCH_W
    const = functools.partial(_cspec, pipeline_mode=pl.Buffered(1))
    return pl.pallas_call(
        functools.partial(_fnet_body, t),
        grid=(b,),
        in_specs=[_zspec(t, w, FNET_BLOCK)] + [const((t, t))] * 4 + [const((w, w))] * 4,
        out_specs=pl.BlockSpec((None, t, w), lambda i: (i, 0, 0)),
        out_shape=jax.ShapeDtypeStruct((b, t, w), F32),
        scratch_shapes=[pltpu.VMEM((t, w), BF16)] * 4,
        compiler_params=pltpu.CompilerParams(dimension_semantics=("parallel",), vmem_limit_bytes=FNET_VMEM_LIMIT),
        name="fnet_mixer",
    )(z, *tables)


def _ln_rows(x):
    mu = jnp.mean(x, axis=-1, keepdims=True)
    xc = x - mu
    return xc * lax.rsqrt(jnp.mean(xc * xc, axis=-1, keepdims=True) + NORM_EPS)


_ROW_CONST = functools.partial(_cspec, pipeline_mode=pl.Buffered(1))


def _row_spec(width):
    return pl.BlockSpec((MXU_ROWS, width), lambda i: (i, 0))


def _sample_spec(t, d):
    return pl.BlockSpec((None, 1, d), lambda i: (i // (t // MXU_ROWS), 0, 0))


def _adaln_body(c_ref, w_ref, b_ref, o_ref):
    cc = c_ref[...]
    act = (cc * _sigmoid(cc)).astype(BF16)
    o_ref[...] = jnp.dot(act, w_ref[...].astype(BF16), preferred_element_type=F32) + b_ref[...]


def adaln_modulation(cond, ada_w, ada_b):
    r, d = cond.shape
    n_layers, _, n = ada_w.shape
    tn = n // 4
    return pl.pallas_call(
        _adaln_body,
        grid=(n_layers, n // tn),
        in_specs=[pl.BlockSpec((r, d), lambda l, j: (0, 0)), pl.BlockSpec((None, d, tn), lambda l, j: (l, 0, j)),
                  pl.BlockSpec((None, 1, tn), lambda l, j: (l, 0, j))],
        out_specs=pl.BlockSpec((None, r, tn), lambda l, j: (l, 0, j)),
        out_shape=jax.ShapeDtypeStruct((n_layers, r, n), F32),
        compiler_params=pltpu.CompilerParams(dimension_semantics=("parallel", "parallel"), vmem_limit_bytes=VMEM_LIMIT),
        name="adaln",
    )(cond, ada_w, ada_b[:, None, :])


def _modproj_body(x_ref, sh_ref, sc_ref, w_ref, z_ref, h_ref):
    h = (_ln_rows(x_ref[...]) * (1.0 + sc_ref[...]) + sh_ref[...]).astype(BF16)
    h_ref[...] = h
    z_ref[...] = jnp.dot(h, w_ref[...], preferred_element_type=F32)


def modulated_project(x, shift, scale, w, t):
    m, d = x.shape
    n = w.shape[1]
    return pl.pallas_call(
        _modproj_body,
        grid=(m // MXU_ROWS,),
        in_specs=[_row_spec(d), _sample_spec(t, d), _sample_spec(t, d), _ROW_CONST((d, n))],
        out_specs=[_row_spec(n), _row_spec(d)],
        out_shape=[jax.ShapeDtypeStruct((m, n), F32), jax.ShapeDtypeStruct((m, d), BF16)],
        compiler_params=_MIXER_PARAMS,
        name="modulated_project",
    )(x, shift, scale, w)


def _merge_body(h_ref, oa_ref, ob_ref, oc_ref, od_ref, x_ref, gate1_ref, sh2_ref, sc2_ref, lng_ref, lnb_ref,
                wg_ref, bg_ref, wb_ref, wo_ref, wr_ref, xo_ref, h2_ref, aff_ref):
    h = h_ref[...]
    acc = None
    for g, o_ref in enumerate((oa_ref, ob_ref, oc_ref, od_ref)):
        gate = _sigmoid(jnp.dot(h, wg_ref[g], preferred_element_type=F32) + bg_ref[g])
        t = gate * jnp.dot(o_ref[...].astype(BF16), wb_ref[g], preferred_element_type=F32)
        acc = t if acc is None else acc + t
    y = jnp.dot(acc.astype(BF16), wo_ref[...], preferred_element_type=F32)
    x_new = _ln_rows(DEEPNORM_ALPHA * x_ref[...] + gate1_ref[...] * y) * lng_ref[...] + lnb_ref[...]
    xo_ref[...] = x_new
    h2 = (_ln_rows(x_new) * (1.0 + sc2_ref[...]) + sh2_ref[...]).astype(BF16)
    h2_ref[...] = h2
    logits = jnp.dot(h2, wr_ref[...], preferred_element_type=F32)
    valid = lax.broadcasted_iota(jnp.int32, logits.shape, 1) < N_EXPERTS
    logits = jnp.where(valid, logits, NEG_BIG)
    e = jnp.exp(logits - jnp.max(logits, axis=-1, keepdims=True))
    aff_ref[...] = e / jnp.sum(e, axis=-1, keepdims=True)


def merge_and_norm(h, outs, x, mod, ln_g, ln_b, p, t):
    m, d = x.shape
    w = outs[0].shape[1]
    gate1, shift2, scale2 = mod
    vec = _ROW_CONST((1, d))
    return pl.pallas_call(
        _merge_body,
        grid=(m // MXU_ROWS,),
        in_specs=[_row_spec(d)] + [_row_spec(w)] * N_BRANCH + [_row_spec(d)] + [_sample_spec(t, d)] * 3 + [vec, vec,
                  _ROW_CONST(p["w_gate"].shape), _ROW_CONST(p["b_gate"].shape), _ROW_CONST(p["w_branch"].shape),
                  _ROW_CONST(p["w_out"].shape), _ROW_CONST(p["w_router"].shape)],
        out_specs=[_row_spec(d), _row_spec(d), _row_spec(LANE)],
        out_shape=[jax.ShapeDtypeStruct((m, d), F32), jax.ShapeDtypeStruct((m, d), BF16),
                   jax.ShapeDtypeStruct((m, LANE), F32)],
        compiler_params=_MIXER_PARAMS,
        name="merge_and_norm",
    )(h, *outs, x, gate1, shift2, scale2, ln_g[None, :], ln_b[None, :],
      p["w_gate"], p["b_gate"], p["w_branch"], p["w_out"], p["w_router"])


def _expert_body(x_ref, wt_ref, wg_ref, wu_ref, wd_ref, yh_ref, yl_ref, wg_s, wu_s, wd_s):
    @pl.when(pl.program_id(1) == 0)
    def _():
        wg_s[...] = wg_ref[...].astype(BF16)
        wu_s[...] = wu_ref[...].astype(BF16)
        wd_s[...] = wd_ref[...].astype(BF16)

    bb, cap, d = x_ref.shape
    x = x_ref[...].reshape(bb * cap, d)
    gate = jnp.dot(x, wg_s[...], preferred_element_type=F32)
    up = jnp.dot(x, wu_s[...], preferred_element_type=F32)
    hid = (gate * _sigmoid(gate) * up).astype(BF16)
    y = jnp.dot(hid, wd_s[...], preferred_element_type=F32) * wt_ref[...].reshape(bb * cap, 1)
    yh, yl = _split(y, 2)
    yh_ref[...] = yh.reshape(yh_ref.shape)
    yl_ref[...] = yl.reshape(yl_ref.shape)


def expert_swiglu(xe, weight, w_ff_gate, w_ff_up, w_ff_down):
    b, e, cap, d = xe.shape
    f = w_ff_gate.shape[-1]
    bb = min(b, max(1, MXU_ROWS // cap))
    x_spec = pl.BlockSpec((bb, None, cap, d), lambda ei, bi: (bi, ei, 0, 0))
    wt_spec = pl.BlockSpec((bb, None, cap, 1), lambda ei, bi: (bi, ei, 0, 0))
    w_spec = lambda shape: pl.BlockSpec((None,) + shape, lambda ei, bi: (ei, 0, 0))
    return pl.pallas_call(
        _expert_body,
        grid=(e, b // bb),
        in_specs=[x_spec, wt_spec, w_spec((d, f)), w_spec((d, f)), w_spec((f, d))],
        out_specs=[x_spec, x_spec],
        out_shape=[jax.ShapeDtypeStruct(xe.shape, BF16)] * 2,
        scratch_shapes=[pltpu.VMEM((d, f), BF16), pltpu.VMEM((d, f), BF16), pltpu.VMEM((f, d), BF16)],
        compiler_params=pltpu.CompilerParams(dimension_semantics=("parallel", "arbitrary"),
                                             vmem_limit_bytes=VMEM_LIMIT),
        name="expert_swiglu",
    )(xe, weight[..., None], w_ff_gate, w_ff_up, w_ff_down)


def _combine_body(n_slots, idx_ref, yh_ref, yl_ref, x_ref, gate2_ref, lng_ref, lnb_ref, o_ref):
    tile = pl.program_id(1)
    token = tile * MXU_ROWS + lax.broadcasted_iota(jnp.int32, (MXU_ROWS, n_slots), 0)
    onehot = jnp.where(idx_ref[...] == token, 1.0, 0.0).astype(BF16)
    moe = (jnp.dot(onehot, yh_ref[...], preferred_element_type=F32)
           + jnp.dot(onehot, yl_ref[...], preferred_element_type=F32))
    o_ref[...] = _ln_rows(DEEPNORM_ALPHA * x_ref[...] + gate2_ref[...] * moe) * lng_ref[...] + lnb_ref[...]


def combine_and_norm(idx, yh, yl, x, gate2, ln_g, ln_b):
    b, t, d = x.shape
    n_slots = idx.shape[-1]
    per_sample = lambda shape: pl.BlockSpec((None,) + shape, lambda i, j: (i, 0, 0))
    tile = pl.BlockSpec((None, MXU_ROWS, d), lambda i, j: (i, j, 0))
    vec = pl.BlockSpec((1, d), lambda i, j: (0, 0))
    return pl.pallas_call(
        functools.partial(_combine_body, n_slots),
        grid=(b, t // MXU_ROWS),
        in_specs=[per_sample((1, n_slots)), per_sample((n_slots, d)), per_sample((n_slots, d)), tile,
                  per_sample((1, d)), vec, vec],
        out_specs=tile,
        out_shape=jax.ShapeDtypeStruct((b, t, d), F32),
        compiler_params=pltpu.CompilerParams(dimension_semantics=("parallel", "arbitrary"),
                                             vmem_limit_bytes=VMEM_LIMIT),
        name="combine_and_norm",
    )(idx, yh, yl, x, gate2, ln_g[None, :], ln_b[None, :])


def sincos_grid(rows, cols, dim):
    quarter = dim // 4
    omega = 1.0 / (10000.0 ** (jnp.arange(quarter, dtype=F32) / quarter))
    er = jnp.arange(rows, dtype=F32)[:, None] * omega
    ec = jnp.arange(cols, dtype=F32)[:, None] * omega
    er = jnp.concatenate([jnp.sin(er), jnp.cos(er)], axis=-1)
    ec = jnp.concatenate([jnp.sin(ec), jnp.cos(ec)], axis=-1)
    emb = jnp.concatenate([jnp.broadcast_to(er[:, None, :], (rows, cols, dim // 2)),
                           jnp.broadcast_to(ec[None, :, :], (rows, cols, dim // 2))], axis=-1)
    return emb.reshape(rows * cols, dim)


def hgrn_lower_bounds(logits):
    cum = jnp.cumsum(jax.nn.softmax(logits.astype(F32), axis=1), axis=1)
    return cum - cum[:, :1]


def _flat(a):
    return a.reshape(-1, a.shape[-1])


def token_mixer(xc, xl, mod_c, mod_l, p, fnet_tabs, b, with_ctx):
    tc, tl = xc.shape[0] // b, xl.shape[0] // b
    zc, hc = modulated_project(xc, mod_c[0], mod_c[1], p["w_in"], tc)
    zl, hl = modulated_project(xl, mod_l[0], mod_l[1], p["w_in"], tl)
    zc, zl = zc.reshape(b, tc, Z_COLS), zl.reshape(b, tl, Z_COLS)
    a_c, a_l = hgrn_mixer(zc, zl, p["lb"], p["hgrn_norm_w"])
    b_c, b_l = gdn_mixer(zc, zl, p["gdn_conv_w"], p["gdn_a_log"], p["gdn_dt_bias"], p["gdn_norm_w"])
    d_c, d_l = ssd_mixer(zc, zl, p["ssd_conv_w"], p["ssd_conv_b"], p["ssd_a_log"], p["ssd_dt_bias"],
                         p["ssd_d"], p["ssd_norm_w"])

    def merged(h, outs, x, mod, t):
        return merge_and_norm(h, tuple(_flat(o) for o in outs), x, mod[2:5], p["ln1_g"], p["ln1_b"], p, t)

    out_l = merged(hl, (a_l, b_l, fnet_mixer(zl, fnet_tabs[1]), d_l), xl, mod_l, tl)
    out_c = merged(hc, (a_c, b_c, fnet_mixer(zc, fnet_tabs[0]), d_c), xc, mod_c, tc) if with_ctx else None
    return out_c, out_l


def channel_mixer(x_mid, h, aff, b, gate2, ln_g, ln_b, w_ff_gate, w_ff_up, w_ff_down):
    t_ = h.shape[0] // b
    d = h.shape[-1]
    h = h.reshape(b, t_, d)
    cap = EC_CAPACITY_FACTOR * t_ // N_EXPERTS
    aff = aff[:, :N_EXPERTS].reshape(b, t_, N_EXPERTS)
    weight, idx = lax.top_k(jnp.swapaxes(aff, 1, 2), cap)
    xe = h[jnp.arange(b)[:, None, None], idx]
    yh, yl = expert_swiglu(xe, weight, w_ff_gate, w_ff_up, w_ff_down)
    n_slots = N_EXPERTS * cap
    x_new = combine_and_norm(idx.reshape(b, 1, n_slots), yh.reshape(b, n_slots, d), yl.reshape(b, n_slots, d),
                             x_mid.reshape(b, t_, d), gate2, ln_g, ln_b)
    return _flat(x_new)


def kernel(x, c, ctx, c_ctx, ada_w, ada_b, w_in, hgrn_lb_logits, hgrn_norm_w,
           gdn_conv_w, gdn_a_log, gdn_dt_bias, gdn_norm_w,
           ssd_conv_w, ssd_conv_b, ssd_a_log, ssd_dt_bias, ssd_d, ssd_norm_w,
           w_gate, b_gate, w_branch, w_out, ln1_g, ln1_b,
           w_router, w_ff_gate, w_ff_up, w_ff_down, ln2_g, ln2_b):
    b, n_lat, d = x.shape
    n_ctx = ctx.shape[1]
    rows = n_lat // GRID_W
    xl = _flat(x + sincos_grid(rows, GRID_W, D_MODEL).astype(x.dtype))
    xc = _flat(ctx)
    lb_all = hgrn_lower_bounds(hgrn_lb_logits)
    fnet_tabs = (fnet_tables(n_ctx), fnet_tables(n_lat))
    cond = jnp.concatenate([c, c_ctx[None, :], jnp.zeros((-(b + 1) % SUBLANE, d), c.dtype)], axis=0)
    mod_all = adaln_modulation(cond, ada_w, ada_b)
    for l in range(DEPTH):
        with_ctx = l < DEPTH - 1
        mod_l = [m[:, None, :] for m in jnp.split(mod_all[l, :b], 6, axis=-1)]
        mod_c = [jnp.broadcast_to(m[None, :, :], (b, 1, d)) for m in jnp.split(mod_all[l, b:b + 1], 6, axis=-1)]
        p = {
            "ln1_g": ln1_g[l], "ln1_b": ln1_b[l],
            "w_router": jnp.pad(w_router[l], ((0, 0), (0, LANE - N_EXPERTS))).astype(BF16),
            "w_in": permute_w_in(w_in[l]).astype(BF16), "lb": lb_all[:, l], "hgrn_norm_w": hgrn_norm_w[l],
            "gdn_conv_w": gdn_conv_w[l], "gdn_a_log": gdn_a_log[l], "gdn_dt_bias": gdn_dt_bias[l],
            "gdn_norm_w": gdn_norm_w[l], "ssd_conv_w": ssd_conv_w[l], "ssd_conv_b": ssd_conv_b[l],
            "ssd_a_log": ssd_a_log[l], "ssd_dt_bias": ssd_dt_bias[l], "ssd_d": ssd_d[l], "ssd_norm_w": ssd_norm_w[l],
            "w_gate": w_gate[l].astype(BF16), "b_gate": b_gate[l][:, None, :],
            "w_branch": w_branch[l].astype(BF16), "w_out": w_out[l].astype(BF16),
        }
        ffw = (w_ff_gate[l], w_ff_up[l], w_ff_down[l])
        out_c, out_l = token_mixer(xc, xl, mod_c, mod_l, p, fnet_tabs, b, with_ctx)
        xl = channel_mixer(*out_l, b, mod_l[5], ln2_g[l], ln2_b[l], *ffw)
        if with_ctx:
            xc = channel_mixer(*out_c, b, mod_c[5], ln2_g[l], ln2_b[l], *ffw)
    return xl.reshape(b, n_lat, d)
```

```python
import functools
import math

import jax
import jax.numpy as jnp
import numpy as np
from jax import lax
from jax.experimental import pallas as pl
from jax.experimental.pallas import tpu as pltpu

D_MODEL = 1024
DEPTH = 4
GRID_W = 64
N_BRANCH = 4
BRANCH_W = D_MODEL // N_BRANCH
HEAD_DIM = 64
N_HEADS = BRANCH_W // HEAD_DIM
N_PAIRS = N_HEADS // 2
FOURIER_GROUP_W = HEAD_DIM
CHUNK = 64
CONV_K = 5
SSD_STATE = 64
SSD_GROUPS = 2
N_EXPERTS = 16
EC_CAPACITY_FACTOR = 2
DEEPNORM_ALPHA = (2.0 * DEPTH) ** 0.25
NORM_EPS = 1e-6

LANE = 128
SUBLANE = 8
MXU_ROWS = 256
VMEM_LIMIT = 48 * 1024 * 1024
FNET_VMEM_LIMIT = 56 * 1024 * 1024
NEG_BIG = -1e30

F32 = jnp.float32
BF16 = jnp.bfloat16

XBC_W = BRANCH_W + 2 * SSD_GROUPS * SSD_STATE
Z_ORDER = ("b_qkv", "a_q", "a_f_fwd", "a_f_bwd", "a_v", "a_g", "b_g", "c_u", "d_xbc", "d_z", "b_a", "b_beta", "d_dt")
REF_SPLITS = (
    ("a_q", BRANCH_W), ("a_f_fwd", BRANCH_W), ("a_f_bwd", BRANCH_W), ("a_v", BRANCH_W), ("a_g", BRANCH_W),
    ("b_qkv", 3 * BRANCH_W), ("b_g", BRANCH_W), ("b_a", 2 * N_HEADS), ("b_beta", 2 * N_HEADS),
    ("c_u", BRANCH_W), ("d_xbc", XBC_W), ("d_z", BRANCH_W), ("d_dt", 2 * N_HEADS),
)
Z_COLS = 3 * BRANCH_W + 5 * BRANCH_W + 2 * BRANCH_W + XBC_W + BRANCH_W + LANE
GDN_BLOCKS = (0, 8, 26)
HGRN_BLOCKS = (3, 4, 5, 6, 7)
FNET_BLOCK = 9
SSD_BLOCKS = (5, 12, 26)
DT_LANE0 = 4 * N_HEADS


def permute_w_in(w_in):
    start, spans = 0, {}
    for name, size in REF_SPLITS:
        spans[name] = (start, start + size)
        start += size
    parts = [w_in[:, spans[n][0]:spans[n][1]] for n in Z_ORDER]
    used = sum(p.shape[1] for p in parts)
    return jnp.concatenate(parts + [jnp.zeros((w_in.shape[0], Z_COLS - used), w_in.dtype)], axis=1)


def _bdot(a, b):
    return jnp.dot(a.astype(BF16), b.astype(BF16), preferred_element_type=F32)


def _bdot_nt(a, b):
    return lax.dot_general(a.astype(BF16), b.astype(BF16), (((1,), (1,)), ((), ())), preferred_element_type=F32)


def _bdot_tn(a, b):
    return lax.dot_general(a.astype(BF16), b.astype(BF16), (((0,), (0,)), ((), ())), preferred_element_type=F32)


def _split(x, terms):
    out = []
    for _ in range(terms):
        p = x.astype(BF16)
        out.append(p)
        x = x - p.astype(F32)
    return out


def _dot_exact_lhs(m, x, terms=3):
    mb = m.astype(BF16)
    return sum(jnp.dot(mb, p, preferred_element_type=F32) for p in _split(x, terms))


def _dot_exact_rhs(x, m, terms=2):
    mb = m.astype(BF16)
    return sum(jnp.dot(p, mb, preferred_element_type=F32) for p in _split(x, terms))


def _dot_x3(a, b):
    ah, al = _split(a, 2)
    bh, bl = _split(b, 2)
    d = lambda p, q: jnp.dot(p, q, preferred_element_type=F32)
    return d(ah, bh) + d(al, bh) + d(ah, bl)


def _iota2(n, m):
    return lax.broadcasted_iota(jnp.int32, (n, m), 0), lax.broadcasted_iota(jnp.int32, (n, m), 1)


def _sigmoid(x):
    return 1.0 / (1.0 + jnp.exp(-x))


def _softplus(x):
    return jnp.maximum(x, 0.0) + jnp.log(1.0 + jnp.exp(-jnp.abs(x)))


def _log_sigmoid(x):
    return jnp.minimum(x, 0.0) - jnp.log(1.0 + jnp.exp(-jnp.abs(x)))


def _logaddexp(a, b):
    m = jnp.maximum(a, b)
    return m + jnp.log(jnp.exp(a - m) + jnp.exp(b - m))


def _conv(x_ref, w_ref, n, n_chunks):
    t = n_chunks * CHUNK
    start = pl.multiple_of(n * CHUNK, CHUNK)
    cur = x_ref[pl.ds(start, CHUNK), :]
    prev_start = pl.multiple_of(jnp.maximum(start - SUBLANE, 0), SUBLANE)
    next_start = pl.multiple_of(jnp.minimum(start + CHUNK, t - SUBLANE), SUBLANE)
    prev = x_ref[pl.ds(prev_start, SUBLANE), :] * jnp.where(n > 0, 1.0, 0.0)
    nxt = x_ref[pl.ds(next_start, SUBLANE), :] * jnp.where(n < n_chunks - 1, 1.0, 0.0)
    ext = jnp.concatenate([prev, cur, nxt], axis=0)
    pad = CONV_K // 2
    acc = None
    n_ext = CHUNK + 2 * SUBLANE
    for k in range(CONV_K):
        shifted = ext if k == pad else pltpu.roll(ext, (pad - k) % n_ext, 0)
        term = shifted[SUBLANE:SUBLANE + CHUNK, :] * w_ref[k:k + 1, :]
        acc = term if acc is None else acc + term
    return acc


PREP_UNROLL = 4


def _interleave(progs):
    live = list(progs)
    while live:
        nxt = []
        for p in live:
            try:
                next(p)
                nxt.append(p)
            except StopIteration:
                pass
        live = nxt


def _chunk_loop(prog, n_chunks):
    def body(i, carry):
        _interleave([prog(i * PREP_UNROLL + j) for j in range(PREP_UNROLL)])
        return carry
    lax.fori_loop(0, n_chunks // PREP_UNROLL, body, 0)


def _backward_chunk(step, nc_c, n_tot):
    return jnp.where(step < nc_c, nc_c - 1 - step, n_tot - 1 - (step - nc_c))


def _chunk_rows(n):
    return pl.ds(pl.multiple_of(n * CHUNK, CHUNK), CHUNK)


def _head_norm_gate(o_s, gate_ref, out_ref, normw, bdm, n_chunks, base):
    def body(n, carry):
        rows = pl.ds(pl.multiple_of(n * MXU_ROWS, MXU_ROWS), MXU_ROWS)
        o = o_s[pl.ds(pl.multiple_of(base + n * MXU_ROWS, MXU_ROWS), MXU_ROWS), :]
        ms = _dot_exact_rhs(o * o, bdm) * (1.0 / HEAD_DIM)
        gt = gate_ref[rows, :]
        out_ref[rows, :] = o * lax.rsqrt(ms + NORM_EPS) * normw[...] * (gt * _sigmoid(gt))
        return carry
    lax.fori_loop(0, n_chunks * CHUNK // MXU_ROWS, body, 0)


def head_block_ones():
    i = np.arange(BRANCH_W)
    return jnp.asarray((i[:, None] // HEAD_DIM) == (i[None, :] // HEAD_DIM), BF16)


def _zspec(t, width, blk, **kw):
    return pl.BlockSpec((None, t, width), lambda i: (i, 0, blk), **kw)


def _cspec(shape, **kw):
    return pl.BlockSpec(shape, lambda *_: (0,) * len(shape), **kw)


def _mixer_out(b, tc, tl):
    assert tc % MXU_ROWS == 0 and tl % MXU_ROWS == 0 and MXU_ROWS % (PREP_UNROLL * CHUNK) == 0
    specs = [pl.BlockSpec((None, tc, BRANCH_W), lambda i: (i, 0, 0)), pl.BlockSpec((None, tl, BRANCH_W), lambda i: (i, 0, 0))]
    shapes = [jax.ShapeDtypeStruct((b, tc, BRANCH_W), F32), jax.ShapeDtypeStruct((b, tl, BRANCH_W), F32)]
    return specs, shapes


_MIXER_PARAMS = pltpu.CompilerParams(dimension_semantics=("parallel",), vmem_limit_bytes=VMEM_LIMIT)


def _pair_blocks(x):
    first = (lax.broadcasted_iota(jnp.int32, x.shape, 1) % LANE) < HEAD_DIM
    zero = jnp.zeros_like(x)
    return jnp.concatenate([jnp.where(first, x, zero), jnp.where(first, zero, x)], axis=0)


def _pair_cols(x, l0):
    lane = lax.broadcasted_iota(jnp.int32, (x.shape[0], LANE), 1)
    return jnp.where(lane < HEAD_DIM, x[:, l0:l0 + 1], x[:, l0 + 1:l0 + 2])


def _pair_iota():
    i, j = _iota2(CHUNK, LANE)
    return i, j % HEAD_DIM


def _pair_order_masks(d):
    i, j = _pair_iota()
    return ((j <= i), (j < i)) if d == 0 else ((j >= i), (j > i))


def _pair_dot(x, y):
    return jnp.dot(x.astype(BF16), _pair_blocks(y.astype(BF16)), preferred_element_type=F32)


def _head_block_mask():
    r, c = _iota2(LANE, LANE)
    return (r < HEAD_DIM) == (c < HEAD_DIM)


def _pairs(x):
    return [x[:, p * LANE:(p + 1) * LANE] for p in range(N_PAIRS)]


def _unit_lower_inverses(mats):
    i, j = _pair_iota()
    eye = (i == j).astype(F32)
    same4 = (i // 4) == (j // 4)
    d4 = [jnp.where(same4, a, 0.0) for a in mats]
    sq = [_pair_dot(d, d) for d in d4]
    xs = [eye - d for d in d4]
    yield
    xs = [x + _pair_dot(x, q) for x, q in zip(xs, sq)]
    yield
    s = 4
    while s < CHUNK:
        sel = ((i // (2 * s)) == (j // (2 * s))) & ((i // s) != (j // s))
        ox = [_pair_dot(jnp.where(sel, a, 0.0), x) for a, x in zip(mats, xs)]
        yield
        xs = [x - _pair_dot(x, y) for x, y in zip(xs, ox)]
        yield
        s *= 2
    return xs


def _gdn_body(nc_c, nc_l, qkv_c, qkv_l, gate_c, gate_l, sm_c, sm_l, convw, prow, normw, bd,
              out_c, out_l, u_s, w_s, qk_s, qd_s, kd_s, gl_s, o_s, st_s):
    w = BRANCH_W
    bdm = bd[...]
    lane = lax.broadcasted_iota(jnp.int32, (CHUNK, LANE), 1)
    ii, jj = _iota2(CHUNK, CHUNK)
    incl_lower = (jj <= ii).astype(F32)
    masks = [_pair_order_masks(d) for d in range(2)]
    dp = [(d, p) for d in range(2) for p in range(N_PAIRS)]
    n_tot = nc_c + nc_l

    def prep(x_ref, s_ref, n_chunks, base):
        def prog(n):
            y = _conv(x_ref, convw, n, n_chunks)
            y = y * _sigmoid(y)
            q, k, v = y[:, :w], y[:, w:2 * w], y[:, 2 * w:]
            qss, kss = _dot_exact_rhs(q * q, bdm), _dot_exact_rhs(k * k, bdm)
            yield
            q = q * lax.rsqrt(qss + NORM_EPS) * HEAD_DIM ** -0.5
            k = k * lax.rsqrt(kss + NORM_EPS)
            sm = s_ref[_chunk_rows(n), :]
            la = jnp.where(lane < 2 * N_HEADS, -jnp.exp(prow[0:1, :]) * _softplus(sm + prow[1:2, :]), 0.0)
            beta_all = _sigmoid(sm)
            prefix = _dot_exact_lhs(incl_lower, la)
            total = jnp.sum(la, axis=0, keepdims=True)
            g_all = jnp.where(lane < N_HEADS, prefix, total - prefix + la)
            g_t = g_all.T
            qp, kp, vp = _pairs(q), _pairs(k), _pairs(v)
            qkk = [lax.dot_general(jnp.concatenate([qp[p], kp[p]], axis=0).astype(BF16), _pair_blocks(kp[p].astype(BF16)),
                                   (((1,), (1,)), ((), ())), preferred_element_type=F32) for p in range(N_PAIRS)]
            yield
            l0 = [d * N_HEADS + 2 * p for d, p in dp]
            g_col = [_pair_cols(g_all, l) for l in l0]
            g_row = [jnp.concatenate([g_t[l:l + 1, :], g_t[l + 1:l + 2, :]], axis=1) for l in l0]
            beta = [_pair_cols(beta_all, 2 * N_HEADS + l) for l in l0]
            gl = [_pair_cols(total, l) for l in l0]
            decay = [jnp.exp(jnp.where(masks[d][0], g_col[x] - g_row[x], NEG_BIG)) for x, (d, p) in enumerate(dp)]
            qk = [qkk[p][:CHUNK] * decay[x] for x, (d, p) in enumerate(dp)]
            a = [jnp.where(masks[d][1], beta[x] * qkk[p][CHUNK:] * decay[x], 0.0) for x, (d, p) in enumerate(dp)]
            tinv = yield from _unit_lower_inverses(a)
            eg = [jnp.exp(g) for g in g_col]
            rhs = [jnp.concatenate([beta[x] * vp[p], beta[x] * kp[p] * eg[x]], axis=1) for x, (d, p) in enumerate(dp)]
            uw = [_pair_dot(t, r) for t, r in zip(tinv, rhs)]
            yield
            resid = [r - y0 - _dot_x3(m, _pair_blocks(y0)) for r, y0, m in zip(rhs, uw, a)]
            yield
            uw = [y0 + _pair_dot(t, r) for y0, t, r in zip(uw, tinv, resid)]
            yield
            rows = pl.ds(pl.multiple_of(base + n * CHUNK, CHUNK), CHUNK)
            for x, (d, p) in enumerate(dp):
                cols = slice(p * LANE, (p + 1) * LANE)
                u_s[d, rows, cols] = uw[x][:, :LANE]
                w_s[d, rows, cols] = uw[x][:, LANE:].astype(BF16)
                qk_s[d, rows, cols] = qk[x].astype(BF16)
                qd_s[d, rows, cols] = (qp[p] * eg[x]).astype(BF16)
                kd_s[d, rows, cols] = (kp[p] * jnp.exp(gl[x] - g_col[x])).astype(BF16)
            gl_s[pl.ds(base // CHUNK + n, 1), :] = jnp.exp(total)
            yield
        _chunk_loop(prog, n_chunks)

    prep(qkv_c, sm_c, nc_c, 0)
    prep(qkv_l, sm_l, nc_l, nc_c * CHUNK)

    st_s[...] = jnp.zeros_like(st_s)
    o_s[...] = jnp.zeros_like(o_s)
    block = _head_block_mask()
    first_rows = lax.broadcasted_iota(jnp.int32, (LANE, 1), 0) < HEAD_DIM

    def scan_body(step, carry):
        n_dir = (step, _backward_chunk(step, nc_c, n_tot))
        rows = [_chunk_rows(n) for n in n_dir]
        egl_rows = [gl_s[pl.ds(n, 1), :] for n in n_dir]
        tiles = lambda ref, x: ref[dp[x][0], rows[dp[x][0]], dp[x][1] * LANE:(dp[x][1] + 1) * LANE]
        s_prev = [st_s[x] for x in range(len(dp))]
        v_new = [tiles(u_s, x) - jnp.dot(tiles(w_s, x), s_prev[x].astype(BF16), preferred_element_type=F32)
                 for x in range(len(dp))]
        o = [jnp.dot(tiles(qd_s, x), s_prev[x].astype(BF16), preferred_element_type=F32)
             + jnp.dot(tiles(qk_s, x), _pair_blocks(v_new[x].astype(BF16)), preferred_element_type=F32)
             for x in range(len(dp))]
        for x, (d, p) in enumerate(dp):
            l = d * N_HEADS + 2 * p
            egl = jnp.where(first_rows, egl_rows[d][:, l:l + 1], egl_rows[d][:, l + 1:l + 2])
            st_s[x] = egl * s_prev[x] + jnp.where(block, _bdot_tn(tiles(kd_s, x), v_new[x]), 0.0)
            cols = slice(p * LANE, (p + 1) * LANE)
            o_s[rows[d], cols] = o_s[rows[d], cols] + o[x]
        return carry
    lax.fori_loop(0, n_tot, scan_body, 0)

    _head_norm_gate(o_s, gate_c, out_c, normw, bdm, nc_c, 0)
    _head_norm_gate(o_s, gate_l, out_l, normw, bdm, nc_l, nc_c * CHUNK)


def gdn_mixer(zc, zl, conv_w, a_log, dt_bias, norm_w):
    b, tc, _ = zc.shape
    tl = zl.shape[1]
    nc_c, nc_l = tc // CHUNK, tl // CHUNK
    w = BRANCH_W
    qkv_blk, gate_blk, sm_blk = GDN_BLOCKS
    prow = jnp.zeros((SUBLANE, LANE), F32)
    prow = prow.at[0, :2 * N_HEADS].set(a_log.reshape(-1)).at[1, :2 * N_HEADS].set(dt_bias.reshape(-1))
    out_specs, out_shape = _mixer_out(b, tc, tl)
    return pl.pallas_call(
        functools.partial(_gdn_body, nc_c, nc_l),
        grid=(b,),
        in_specs=[_zspec(tc, 3 * w, qkv_blk), _zspec(tl, 3 * w, qkv_blk, pipeline_mode=pl.Buffered(1)),
                  _zspec(tc, w, gate_blk), _zspec(tl, w, gate_blk),
                  _zspec(tc, LANE, sm_blk), _zspec(tl, LANE, sm_blk),
                  _cspec((SUBLANE, 3 * w)), _cspec((SUBLANE, LANE)), _cspec((1, w)), _cspec((w, w))],
        out_specs=out_specs,
        out_shape=out_shape,
        scratch_shapes=[pltpu.VMEM((2, tc + tl, w), F32)] + [pltpu.VMEM((2, tc + tl, w), BF16)] * 4 + [
            pltpu.VMEM((nc_c + nc_l, LANE), F32), pltpu.VMEM((tc + tl, w), F32),
            pltpu.VMEM((2 * N_PAIRS, LANE, LANE), F32)],
        compiler_params=_MIXER_PARAMS,
        name="gdn_mixer",
    )(zc, zl, zc, zl, zc, zl,
      jnp.pad(conv_w, ((0, SUBLANE - CONV_K), (0, 0))), prow, norm_w[None, :], head_block_ones())


GLA_LEVELS = (32, 16, 8, 4, 2, 1)


def _gla_level_tables(d):
    n_lv = len(GLA_LEVELS)
    i, t = _iota2(n_lv * CHUNK, CHUNK)
    sel = jnp.zeros((n_lv * CHUNK, CHUNK), F32)
    r, c = _pair_iota()
    masks = []
    for x, s in enumerate(GLA_LEVELS):
        row = i - x * CHUNK
        bound = 2 * s * (row // (2 * s)) + s - 1 + d
        sel = jnp.where((i // CHUNK == x) & (t == bound), 1.0, sel)
        same = (r // (2 * s)) == (c // (2 * s))
        r_hi, c_hi = (r % (2 * s)) >= s, (c % (2 * s)) >= s
        masks.append(same & (r_hi & ~c_hi if d == 0 else ~r_hi & c_hi))
    return sel, masks


def _hgrn_body(nc_c, nc_l, q_c, q_l, ff_c, ff_l, fb_c, fb_l, v_c, v_l, gate_c, gate_l, lbrow, normw, bd,
               out_c, out_l, att_s, qd_s, kd_s, v_s, gl_s, o_s, st_s):
    bdm = bd[...]
    ii, jj = _iota2(CHUNK, CHUNK)
    incl_lower = (jj <= ii).astype(F32)
    pi, pj = _pair_iota()
    eye = pi == pj
    tables = [_gla_level_tables(d) for d in range(2)]
    dp = [(d, p) for d in range(2) for p in range(N_PAIRS)]
    n_tot = nc_c + nc_l

    def prep(q_ref, f_refs, v_ref, n_chunks, base):
        def prog(n):
            rin = _chunk_rows(n)
            rows = pl.ds(pl.multiple_of(base + n * CHUNK, CHUNK), CHUNK)
            zq = q_ref[rin, :]
            q = zq * _sigmoid(zq)
            v_s[rows, :] = v_ref[rin, :].astype(BF16)
            for d in range(2):
                zf = f_refs[d][rin, :]
                log_lb, log1m_lb, one_m_lb = lbrow[3 * d:3 * d + 1, :], lbrow[3 * d + 1:3 * d + 2, :], lbrow[3 * d + 2:3 * d + 3, :]
                lf = _logaddexp(log_lb, log1m_lb + _log_sigmoid(zf))
                k = one_m_lb * _sigmoid(-zf)
                prefix = _dot_exact_lhs(incl_lower, lf)
                diag = _dot_exact_rhs(q * k, bdm)
                yield
                total = jnp.sum(lf, axis=0, keepdims=True)
                g = prefix if d == 0 else total - prefix + lf
                sel, masks = tables[d]
                c_all = _dot_exact_lhs(sel, g)
                yield
                acc = [jnp.where(eye, t, 0.0) for t in _pairs(diag)]
                for x in range(len(GLA_LEVELS)):
                    c = c_all[x * CHUNK:(x + 1) * CHUNK, :]
                    qt = _pairs((q * jnp.exp(jnp.minimum(g - c, 0.0))).astype(BF16))
                    kt = _pairs((k * jnp.exp(jnp.minimum(c - g, 0.0))).astype(BF16))
                    lvl = [lax.dot_general(qt[p], _pair_blocks(kt[p]), (((1,), (1,)), ((), ())),
                                           preferred_element_type=F32) for p in range(N_PAIRS)]
                    acc = [a + jnp.where(masks[x], t, 0.0) for a, t in zip(acc, lvl)]
                    yield
                att_s[d, rows, :] = jnp.concatenate(acc, axis=1).astype(BF16)
                qd_s[d, rows, :] = (q * jnp.exp(g)).astype(BF16)
                kd_s[d, rows, :] = (k * jnp.exp(total - g)).astype(BF16)
                gl_s[d, pl.ds(base // CHUNK + n, 1), :] = jnp.exp(total)
        _chunk_loop(prog, n_chunks)

    prep(q_c, (ff_c, fb_c), v_c, nc_c, 0)
    prep(q_l, (ff_l, fb_l), v_l, nc_l, nc_c * CHUNK)

    st_s[...] = jnp.zeros_like(st_s)
    o_s[...] = jnp.zeros_like(o_s)
    block = _head_block_mask()

    def scan_body(step, carry):
        n_dir = (step, _backward_chunk(step, nc_c, n_tot))
        rows = [_chunk_rows(n) for n in n_dir]
        for x, (d, p) in enumerate(dp):
            cols = slice(p * LANE, (p + 1) * LANE)
            att, qd, kd = att_s[d, rows[d], cols], qd_s[d, rows[d], cols], kd_s[d, rows[d], cols]
            v = v_s[rows[d], cols]
            s_prev = st_s[x]
            o = (jnp.dot(att, _pair_blocks(v), preferred_element_type=F32)
                 + lax.dot_general(qd, s_prev.astype(BF16), (((1,), (1,)), ((), ())), preferred_element_type=F32))
            egl = gl_s[d, pl.ds(n_dir[d], 1), :][:, cols]
            st_s[x] = egl * s_prev + jnp.where(block, _bdot_tn(v, kd), 0.0)
            o_s[rows[d], cols] = o_s[rows[d], cols] + o
        return carry
    lax.fori_loop(0, n_tot, scan_body, 0)

    _head_norm_gate(o_s, gate_c, out_c, normw, bdm, nc_c, 0)
    _head_norm_gate(o_s, gate_l, out_l, normw, bdm, nc_l, nc_c * CHUNK)


def hgrn_mixer(zc, zl, lb, norm_w):
    b, tc, _ = zc.shape
    tl = zl.shape[1]
    nc_c, nc_l = tc // CHUNK, tl // CHUNK
    w = BRANCH_W
    lbrow = jnp.zeros((SUBLANE, w), F32)
    for d in range(2):
        lbrow = lbrow.at[3 * d].set(jnp.log(lb[d])).at[3 * d + 1].set(jnp.log1p(-lb[d])).at[3 * d + 2].set(1.0 - lb[d])
    in_specs, args = [], []
    for blk in HGRN_BLOCKS:
        in_specs += [_zspec(tc, w, blk), _zspec(tl, w, blk, pipeline_mode=pl.Buffered(1))]
        args += [zc, zl]
    out_specs, out_shape = _mixer_out(b, tc, tl)
    return pl.pallas_call(
        functools.partial(_hgrn_body, nc_c, nc_l),
        grid=(b,),
        in_specs=in_specs + [_cspec((SUBLANE, w)), _cspec((1, w)), _cspec((w, w))],
        out_specs=out_specs,
        out_shape=out_shape,
        scratch_shapes=[pltpu.VMEM((2, tc + tl, w), BF16)] * 3 + [
            pltpu.VMEM((tc + tl, w), BF16), pltpu.VMEM((2, nc_c + nc_l, w), F32), pltpu.VMEM((tc + tl, w), F32),
            pltpu.VMEM((2 * N_PAIRS, LANE, LANE), F32)],
        compiler_params=_MIXER_PARAMS,
        name="hgrn_mixer",
    )(*args, lbrow, norm_w[None, :], head_block_ones())


def _ssd_body(nc_c, nc_l, xbc_c, xbc_l, z_c, z_l, sm_c, sm_l, convw, convb, prow, dskip, normw,
              out_c, out_l, att_s, v_s, qd_s, kd_s, x_s, gl_s, o_s, st_s):
    w = BRANCH_W
    gw = SSD_GROUPS * SSD_STATE
    lane = lax.broadcasted_iota(jnp.int32, (CHUNK, LANE), 1)
    ii, jj = _iota2(CHUNK, CHUNK)
    incl_lower = (jj <= ii).astype(F32)
    masks = [_pair_order_masks(d) for d in range(2)]
    dp = [(d, p) for d in range(2) for p in range(N_PAIRS)]
    n_tot = nc_c + nc_l
    assert N_PAIRS == SSD_GROUPS

    def prep(x_ref, s_ref, n_chunks, base):
        def prog(n):
            rows = pl.ds(pl.multiple_of(base + n * CHUNK, CHUNK), CHUNK)
            y = _conv(x_ref, convw, n, n_chunks) + convb[...]
            y = y * _sigmoid(y)
            xs, bs, cs = y[:, :w], y[:, w:w + gw], y[:, w + gw:]
            x_s[rows, :] = xs
            sm = s_ref[_chunk_rows(n), :]
            dt_all = _softplus(sm + prow[1:2, :])
            in_dt = (lane >= DT_LANE0) & (lane < DT_LANE0 + 2 * N_HEADS)
            la = jnp.where(in_dt, -jnp.exp(prow[0:1, :]) * dt_all, 0.0)
            prefix = _dot_exact_lhs(incl_lower, la)
            total = jnp.sum(la, axis=0, keepdims=True)
            g_all = jnp.where(lane < DT_LANE0 + N_HEADS, prefix, total - prefix + la)
            g_t = g_all.T
            bg = [bs[:, g * SSD_STATE:(g + 1) * SSD_STATE] for g in range(SSD_GROUPS)]
            cg = [cs[:, g * SSD_STATE:(g + 1) * SSD_STATE] for g in range(SSD_GROUPS)]
            cb = [_bdot_nt(cg[g], jnp.concatenate([bg[g], bg[g]], axis=0)) for g in range(SSD_GROUPS)]
            yield
            for d, p in dp:
                l = DT_LANE0 + d * N_HEADS + 2 * p
                cols = slice(p * LANE, (p + 1) * LANE)
                g_col = _pair_cols(g_all, l)
                g_row = jnp.concatenate([g_t[l:l + 1, :], g_t[l + 1:l + 2, :]], axis=1)
                decay = jnp.exp(jnp.where(masks[d][0], g_col - g_row, NEG_BIG))
                att_s[d, rows, cols] = (cb[p] * decay).astype(BF16)
                v_s[d, rows, cols] = (xs[:, cols] * _pair_cols(dt_all, l)).astype(BF16)
                qd_s[d, rows, cols] = (jnp.concatenate([cg[p], cg[p]], axis=1) * jnp.exp(g_col)).astype(BF16)
                kd_s[d, rows, cols] = (jnp.concatenate([bg[p], bg[p]], axis=1)
                                       * jnp.exp(_pair_cols(total, l) - g_col)).astype(BF16)
            gl_s[pl.ds(base // CHUNK + n, 1), :] = jnp.exp(total)
            yield
        _chunk_loop(prog, n_chunks)

    prep(xbc_c, sm_c, nc_c, 0)
    prep(xbc_l, sm_l, nc_l, nc_c * CHUNK)

    st_s[...] = jnp.zeros_like(st_s)
    o_s[...] = jnp.zeros_like(o_s)
    block = _head_block_mask()
    first_rows = lax.broadcasted_iota(jnp.int32, (LANE, 1), 0) < SSD_STATE

    def scan_body(step, carry):
        n_dir = (step, _backward_chunk(step, nc_c, n_tot))
        rows = [_chunk_rows(n) for n in n_dir]
        egl_rows = [gl_s[pl.ds(n, 1), :] for n in n_dir]
        for x, (d, p) in enumerate(dp):
            cols = slice(p * LANE, (p + 1) * LANE)
            l = DT_LANE0 + d * N_HEADS + 2 * p
            att, v = att_s[d, rows[d], cols], v_s[d, rows[d], cols]
            qd, kd = qd_s[d, rows[d], cols], kd_s[d, rows[d], cols]
            s_prev = st_s[x]
            o = (jnp.dot(att, _pair_blocks(v), preferred_element_type=F32)
                 + jnp.dot(qd, s_prev.astype(BF16), preferred_element_type=F32))
            egl = jnp.where(first_rows, egl_rows[d][:, l:l + 1], egl_rows[d][:, l + 1:l + 2])
            st_s[x] = egl * s_prev + jnp.where(block, _bdot_tn(kd, v), 0.0)
            o_s[rows[d], cols] = o_s[rows[d], cols] + o
        return carry
    lax.fori_loop(0, n_tot, scan_body, 0)

    def finish(z_ref, out_ref, n_chunks, base):
        def body(n, carry):
            rows_in = pl.ds(pl.multiple_of(base + n * MXU_ROWS, MXU_ROWS), MXU_ROWS)
            rows = pl.ds(pl.multiple_of(n * MXU_ROWS, MXU_ROWS), MXU_ROWS)
            zt = z_ref[rows, :]
            y = (o_s[rows_in, :] + dskip[...] * x_s[rows_in, :]) * (zt * _sigmoid(zt))
            ms = jnp.sum(y * y, axis=1, keepdims=True) * (1.0 / w)
            out_ref[rows, :] = y * lax.rsqrt(ms + NORM_EPS) * normw[...]
            return carry
        lax.fori_loop(0, n_chunks * CHUNK // MXU_ROWS, body, 0)

    finish(z_c, out_c, nc_c, 0)
    finish(z_l, out_l, nc_l, nc_c * CHUNK)


def ssd_mixer(zc, zl, conv_w, conv_b, a_log, dt_bias, d_skip, norm_w):
    b, tc, _ = zc.shape
    tl = zl.shape[1]
    nc_c, nc_l = tc // CHUNK, tl // CHUNK
    w = BRANCH_W
    xbc_blk, z_blk, sm_blk = SSD_BLOCKS
    prow = jnp.zeros((SUBLANE, LANE), F32)
    prow = prow.at[0, DT_LANE0:DT_LANE0 + 2 * N_HEADS].set(a_log.reshape(-1))
    prow = prow.at[1, DT_LANE0:DT_LANE0 + 2 * N_HEADS].set(dt_bias.reshape(-1))
    out_specs, out_shape = _mixer_out(b, tc, tl)
    return pl.pallas_call(
        functools.partial(_ssd_body, nc_c, nc_l),
        grid=(b,),
        in_specs=[_zspec(tc, XBC_W, xbc_blk), _zspec(tl, XBC_W, xbc_blk, pipeline_mode=pl.Buffered(1)),
                  _zspec(tc, w, z_blk), _zspec(tl, w, z_blk),
                  _zspec(tc, LANE, sm_blk), _zspec(tl, LANE, sm_blk),
                  _cspec((SUBLANE, XBC_W)), _cspec((1, XBC_W)), _cspec((SUBLANE, LANE)), _cspec((1, w)), _cspec((1, w))],
        out_specs=out_specs,
        out_shape=out_shape,
        scratch_shapes=[pltpu.VMEM((2, tc + tl, w), BF16)] * 4 + [
            pltpu.VMEM((tc + tl, w), F32), pltpu.VMEM((nc_c + nc_l, LANE), F32), pltpu.VMEM((tc + tl, w), F32),
            pltpu.VMEM((2 * N_PAIRS, LANE, LANE), F32)],
        compiler_params=_MIXER_PARAMS,
        name="ssd_mixer",
    )(zc, zl, zc, zl, zc, zl,
      jnp.pad(conv_w, ((0, SUBLANE - CONV_K), (0, 0))), conv_b[None, :], prow,
      jnp.repeat(d_skip, HEAD_DIM)[None, :], norm_w[None, :])


def _fnet_body(t, u_ref, ch_ref, cl_ref, sh_ref, sl_ref, gch_ref, gcl_ref, gsh_ref, gsl_ref, out_ref,
               ph_s, pl_s, qh_s, ql_s):
    d = lambda a, b: jnp.dot(a, b, preferred_element_type=F32)

    def channel_dft(n, carry):
        rows = pl.ds(pl.multiple_of(n * MXU_ROWS, MXU_ROWS), MXU_ROWS)
        uh, ul = _split(u_ref[rows, :], 2)
        p = d(uh, gch_ref[...]) + d(ul, gch_ref[...]) + d(uh, gcl_ref[...])
        q = d(uh, gsh_ref[...]) + d(ul, gsh_ref[...]) + d(uh, gsl_ref[...])
        ph_s[rows, :], pl_s[rows, :] = _split(p, 2)
        qh_s[rows, :], ql_s[rows, :] = _split(q, 2)
        return carry
    lax.fori_loop(0, t // MXU_ROWS, channel_dft, 0)

    def sequence_dft(n, carry):
        rows = pl.ds(pl.multiple_of(n * MXU_ROWS, MXU_ROWS), MXU_ROWS)
        ch, cl, sh, sl = ch_ref[rows, :], cl_ref[rows, :], sh_ref[rows, :], sl_ref[rows, :]
        re = d(ch, ph_s[...]) + d(cl, ph_s[...]) + d(ch, pl_s[...])
        im = d(sh, qh_s[...]) + d(sl, qh_s[...]) + d(sh, ql_s[...])
        out_ref[rows, :] = re - im
        return carry
    lax.fori_loop(0, t // MXU_ROWS, sequence_dft, 0)


def _dft_tables(n, scale):
    j = lax.broadcasted_iota(jnp.int32, (n, n), 0)
    k = lax.broadcasted_iota(jnp.int32, (n, n), 1)
    ang = ((j * k) % n).astype(F32) * (2.0 * math.pi / n)
    out = []
    for tab in (jnp.cos(ang) * scale, jnp.sin(ang) * scale):
        hi = tab.astype(BF16)
        out += [hi, (tab - hi.astype(F32)).astype(BF16)]
    return out


def fnet_tables(t):
    seq = _dft_tables(t, t ** -0.5)
    grp = _dft_tables(FOURIER_GROUP_W, FOURIER_GROUP_W ** -0.5)
    n_grp = BRANCH_W // FOURIER_GROUP_W
    grp = [jnp.kron(jnp.eye(n_grp, dtype=F32), g.astype(F32)).astype(BF16) for g in grp]
    return seq + grp


def fnet_mixer(z, tables):
    b, t, _ = z.shape
    w = BRANCH_W
    const = functools.partial(_cspec, pipeline_mode=pl.Buffered(1))
    return pl.pallas_call(
        functools.partial(_fnet_body, t),
        grid=(b,),
        in_specs=[_zspec(t, w, FNET_BLOCK)] + [const((t, t))] * 4 + [const((w, w))] * 4,
        out_specs=pl.BlockSpec((None, t, w), lambda i: (i, 0, 0)),
        out_shape=jax.ShapeDtypeStruct((b, t, w), F32),
        scratch_shapes=[pltpu.VMEM((t, w), BF16)] * 4,
        compiler_params=pltpu.CompilerParams(dimension_semantics=("parallel",), vmem_limit_bytes=FNET_VMEM_LIMIT),
        name="fnet_mixer",
    )(z, *tables)


def _ln_rows(x):
    mu = jnp.mean(x, axis=-1, keepdims=True)
    xc = x - mu
    return xc * lax.rsqrt(jnp.mean(xc * xc, axis=-1, keepdims=True) + NORM_EPS)


_ROW_CONST = functools.partial(_cspec, pipeline_mode=pl.Buffered(1))


def _row_spec(width):
    return pl.BlockSpec((MXU_ROWS, width), lambda i: (i, 0))


def _sample_spec(t, d):
    return pl.BlockSpec((None, 1, d), lambda i: (i // (t // MXU_ROWS), 0, 0))


def _adaln_body(c_ref, w_ref, b_ref, o_ref):
    cc = c_ref[...]
    act = (cc * _sigmoid(cc)).astype(BF16)
    o_ref[...] = jnp.dot(act, w_ref[...].astype(BF16), preferred_element_type=F32) + b_ref[...]


def adaln_modulation(cond, ada_w, ada_b):
    r, d = cond.shape
    n_layers, _, n = ada_w.shape
    tn = n // 4
    return pl.pallas_call(
        _adaln_body,
        grid=(n_layers, n // tn),
        in_specs=[pl.BlockSpec((r, d), lambda l, j: (0, 0)), pl.BlockSpec((None, d, tn), lambda l, j: (l, 0, j)),
                  pl.BlockSpec((None, 1, tn), lambda l, j: (l, 0, j))],
        out_specs=pl.BlockSpec((None, r, tn), lambda l, j: (l, 0, j)),
        out_shape=jax.ShapeDtypeStruct((n_layers, r, n), F32),
        compiler_params=pltpu.CompilerParams(dimension_semantics=("parallel", "parallel"), vmem_limit_bytes=VMEM_LIMIT),
        name="adaln",
    )(cond, ada_w, ada_b[:, None, :])


def _modproj_body(x_ref, sh_ref, sc_ref, w_ref, z_ref, h_ref):
    h = (_ln_rows(x_ref[...]) * (1.0 + sc_ref[...]) + sh_ref[...]).astype(BF16)
    h_ref[...] = h
    z_ref[...] = jnp.dot(h, w_ref[...], preferred_element_type=F32)


def modulated_project(x, shift, scale, w, t):
    m, d = x.shape
    n = w.shape[1]
    return pl.pallas_call(
        _modproj_body,
        grid=(m // MXU_ROWS,),
        in_specs=[_row_spec(d), _sample_spec(t, d), _sample_spec(t, d), _ROW_CONST((d, n))],
        out_specs=[_row_spec(n), _row_spec(d)],
        out_shape=[jax.ShapeDtypeStruct((m, n), F32), jax.ShapeDtypeStruct((m, d), BF16)],
        compiler_params=_MIXER_PARAMS,
        name="modulated_project",
    )(x, shift, scale, w)


def _merge_body(h_ref, oa_ref, ob_ref, oc_ref, od_ref, x_ref, gate1_ref, sh2_ref, sc2_ref, lng_ref, lnb_ref,
                wg_ref, bg_ref, wb_ref, wo_ref, wr_ref, xo_ref, h2_ref, aff_ref):
    h = h_ref[...]
    acc = None
    for g, o_ref in enumerate((oa_ref, ob_ref, oc_ref, od_ref)):
        gate = _sigmoid(jnp.dot(h, wg_ref[g], preferred_element_type=F32) + bg_ref[g])
        t = gate * jnp.dot(o_ref[...].astype(BF16), wb_ref[g], preferred_element_type=F32)
        acc = t if acc is None else acc + t
    y = jnp.dot(acc.astype(BF16), wo_ref[...], preferred_element_type=F32)
    x_new = _ln_rows(DEEPNORM_ALPHA * x_ref[...] + gate1_ref[...] * y) * lng_ref[...] + lnb_ref[...]
    xo_ref[...] = x_new
    h2 = (_ln_rows(x_new) * (1.0 + sc2_ref[...]) + sh2_ref[...]).astype(BF16)
    h2_ref[...] = h2
    logits = jnp.dot(h2, wr_ref[...], preferred_element_type=F32)
    valid = lax.broadcasted_iota(jnp.int32, logits.shape, 1) < N_EXPERTS
    logits = jnp.where(valid, logits, NEG_BIG)
    e = jnp.exp(logits - jnp.max(logits, axis=-1, keepdims=True))
    aff_ref[...] = e / jnp.sum(e, axis=-1, keepdims=True)


def merge_and_norm(h, outs, x, mod, ln_g, ln_b, p, t):
    m, d = x.shape
    w = outs[0].shape[1]
    gate1, shift2, scale2 = mod
    vec = _ROW_CONST((1, d))
    return pl.pallas_call(
        _merge_body,
        grid=(m // MXU_ROWS,),
        in_specs=[_row_spec(d)] + [_row_spec(w)] * N_BRANCH + [_row_spec(d)] + [_sample_spec(t, d)] * 3 + [vec, vec,
                  _ROW_CONST(p["w_gate"].shape), _ROW_CONST(p["b_gate"].shape), _ROW_CONST(p["w_branch"].shape),
                  _ROW_CONST(p["w_out"].shape), _ROW_CONST(p["w_router"].shape)],
        out_specs=[_row_spec(d), _row_spec(d), _row_spec(LANE)],
        out_shape=[jax.ShapeDtypeStruct((m, d), F32), jax.ShapeDtypeStruct((m, d), BF16),
                   jax.ShapeDtypeStruct((m, LANE), F32)],
        compiler_params=_MIXER_PARAMS,
        name="merge_and_norm",
    )(h, *outs, x, gate1, shift2, scale2, ln_g[None, :], ln_b[None, :],
      p["w_gate"], p["b_gate"], p["w_branch"], p["w_out"], p["w_router"])


def _expert_body(x_ref, wt_ref, wg_ref, wu_ref, wd_ref, yh_ref, yl_ref, wg_s, wu_s, wd_s):
    @pl.when(pl.program_id(1) == 0)
    def _():
        wg_s[...] = wg_ref[...].astype(BF16)
        wu_s[...] = wu_ref[...].astype(BF16)
        wd_s[...] = wd_ref[...].astype(BF16)

    bb, cap, d = x_ref.shape
    x = x_ref[...].reshape(bb * cap, d)
    gate = jnp.dot(x, wg_s[...], preferred_element_type=F32)
    up = jnp.dot(x, wu_s[...], preferred_element_type=F32)
    hid = (gate * _sigmoid(gate) * up).astype(BF16)
    y = jnp.dot(hid, wd_s[...], preferred_element_type=F32) * wt_ref[...].reshape(bb * cap, 1)
    yh, yl = _split(y, 2)
    yh_ref[...] = yh.reshape(yh_ref.shape)
    yl_ref[...] = yl.reshape(yl_ref.shape)


def expert_swiglu(xe, weight, w_ff_gate, w_ff_up, w_ff_down):
    b, e, cap, d = xe.shape
    f = w_ff_gate.shape[-1]
    bb = min(b, max(1, MXU_ROWS // cap))
    x_spec = pl.BlockSpec((bb, None, cap, d), lambda ei, bi: (bi, ei, 0, 0))
    wt_spec = pl.BlockSpec((bb, None, cap, 1), lambda ei, bi: (bi, ei, 0, 0))
    w_spec = lambda shape: pl.BlockSpec((None,) + shape, lambda ei, bi: (ei, 0, 0))
    return pl.pallas_call(
        _expert_body,
        grid=(e, b // bb),
        in_specs=[x_spec, wt_spec, w_spec((d, f)), w_spec((d, f)), w_spec((f, d))],
        out_specs=[x_spec, x_spec],
        out_shape=[jax.ShapeDtypeStruct(xe.shape, BF16)] * 2,
        scratch_shapes=[pltpu.VMEM((d, f), BF16), pltpu.VMEM((d, f), BF16), pltpu.VMEM((f, d), BF16)],
        compiler_params=pltpu.CompilerParams(dimension_semantics=("parallel", "arbitrary"),
                                             vmem_limit_bytes=VMEM_LIMIT),
        name="expert_swiglu",
    )(xe, weight[..., None], w_ff_gate, w_ff_up, w_ff_down)


def _combine_body(n_slots, idx_ref, yh_ref, yl_ref, x_ref, gate2_ref, lng_ref, lnb_ref, o_ref):
    tile = pl.program_id(1)
    token = tile * MXU_ROWS + lax.broadcasted_iota(jnp.int32, (MXU_ROWS, n_slots), 0)
    onehot = jnp.where(idx_ref[...] == token, 1.0, 0.0).astype(BF16)
    moe = (jnp.dot(onehot, yh_ref[...], preferred_element_type=F32)
           + jnp.dot(onehot, yl_ref[...], preferred_element_type=F32))
    o_ref[...] = _ln_rows(DEEPNORM_ALPHA * x_ref[...] + gate2_ref[...] * moe) * lng_ref[...] + lnb_ref[...]


def combine_and_norm(idx, yh, yl, x, gate2, ln_g, ln_b):
    b, t, d = x.shape
    n_slots = idx.shape[-1]
    per_sample = lambda shape: pl.BlockSpec((None,) + shape, lambda i, j: (i, 0, 0))
    tile = pl.BlockSpec((None, MXU_ROWS, d), lambda i, j: (i, j, 0))
    vec = pl.BlockSpec((1, d), lambda i, j: (0, 0))
    return pl.pallas_call(
        functools.partial(_combine_body, n_slots),
        grid=(b, t // MXU_ROWS),
        in_specs=[per_sample((1, n_slots)), per_sample((n_slots, d)), per_sample((n_slots, d)), tile,
                  per_sample((1, d)), vec, vec],
        out_specs=tile,
        out_shape=jax.ShapeDtypeStruct((b, t, d), F32),
        compiler_params=pltpu.CompilerParams(dimension_semantics=("parallel", "arbitrary"),
                                             vmem_limit_bytes=VMEM_LIMIT),
        name="combine_and_norm",
    )(idx, yh, yl, x, gate2, ln_g[None, :], ln_b[None, :])


def sincos_grid(rows, cols, dim):
    quarter = dim // 4
    omega = 1.0 / (10000.0 ** (jnp.arange(quarter, dtype=F32) / quarter))
    er = jnp.arange(rows, dtype=F32)[:, None] * omega
    ec = jnp.arange(cols, dtype=F32)[:, None] * omega
    er = jnp.concatenate([jnp.sin(er), jnp.cos(er)], axis=-1)
    ec = jnp.concatenate([jnp.sin(ec), jnp.cos(ec)], axis=-1)
    emb = jnp.concatenate([jnp.broadcast_to(er[:, None, :], (rows, cols, dim // 2)),
                           jnp.broadcast_to(ec[None, :, :], (rows, cols, dim // 2))], axis=-1)
    return emb.reshape(rows * cols, dim)


def hgrn_lower_bounds(logits):
    cum = jnp.cumsum(jax.nn.softmax(logits.astype(F32), axis=1), axis=1)
    return cum - cum[:, :1]


def _flat(a):
    return a.reshape(-1, a.shape[-1])


def token_mixer(xc, xl, mod_c, mod_l, p, fnet_tabs, b, with_ctx):
    tc, tl = xc.shape[0] // b, xl.shape[0] // b
    zc, hc = modulated_project(xc, mod_c[0], mod_c[1], p["w_in"], tc)
    zl, hl = modulated_project(xl, mod_l[0], mod_l[1], p["w_in"], tl)
    zc, zl = zc.reshape(b, tc, Z_COLS), zl.reshape(b, tl, Z_COLS)
    a_c, a_l = hgrn_mixer(zc, zl, p["lb"], p["hgrn_norm_w"])
    b_c, b_l = gdn_mixer(zc, zl, p["gdn_conv_w"], p["gdn_a_log"], p["gdn_dt_bias"], p["gdn_norm_w"])
    d_c, d_l = ssd_mixer(zc, zl, p["ssd_conv_w"], p["ssd_conv_b"], p["ssd_a_log"], p["ssd_dt_bias"],
                         p["ssd_d"], p["ssd_norm_w"])

    def merged(h, outs, x, mod, t):
        return merge_and_norm(h, tuple(_flat(o) for o in outs), x, mod[2:5], p["ln1_g"], p["ln1_b"], p, t)

    out_l = merged(hl, (a_l, b_l, fnet_mixer(zl, fnet_tabs[1]), d_l), xl, mod_l, tl)
    out_c = merged(hc, (a_c, b_c, fnet_mixer(zc, fnet_tabs[0]), d_c), xc, mod_c, tc) if with_ctx else None
    return out_c, out_l


def channel_mixer(x_mid, h, aff, b, gate2, ln_g, ln_b, w_ff_gate, w_ff_up, w_ff_down):
    t_ = h.shape[0] // b
    d = h.shape[-1]
    h = h.reshape(b, t_, d)
    cap = EC_CAPACITY_FACTOR * t_ // N_EXPERTS
    aff = aff[:, :N_EXPERTS].reshape(b, t_, N_EXPERTS)
    weight, idx = lax.top_k(jnp.swapaxes(aff, 1, 2), cap)
    xe = h[jnp.arange(b)[:, None, None], idx]
    yh, yl = expert_swiglu(xe, weight, w_ff_gate, w_ff_up, w_ff_down)
    n_slots = N_EXPERTS * cap
    x_new = combine_and_norm(idx.reshape(b, 1, n_slots), yh.reshape(b, n_slots, d), yl.reshape(b, n_slots, d),
                             x_mid.reshape(b, t_, d), gate2, ln_g, ln_b)
    return _flat(x_new)


def kernel(x, c, ctx, c_ctx, ada_w, ada_b, w_in, hgrn_lb_logits, hgrn_norm_w,
           gdn_conv_w, gdn_a_log, gdn_dt_bias, gdn_norm_w,
           ssd_conv_w, ssd_conv_b, ssd_a_log, ssd_dt_bias, ssd_d, ssd_norm_w,
           w_gate, b_gate, w_branch, w_out, ln1_g, ln1_b,
           w_router, w_ff_gate, w_ff_up, w_ff_down, ln2_g, ln2_b):
    b, n_lat, d = x.shape
    n_ctx = ctx.shape[1]
    rows = n_lat // GRID_W
    xl = _flat(x + sincos_grid(rows, GRID_W, D_MODEL).astype(x.dtype))
    xc = _flat(ctx)
    lb_all = hgrn_lower_bounds(hgrn_lb_logits)
    fnet_tabs = (fnet_tables(n_ctx), fnet_tables(n_lat))
    cond = jnp.concatenate([c, c_ctx[None, :], jnp.zeros((-(b + 1) % SUBLANE, d), c.dtype)], axis=0)
    mod_all = adaln_modulation(cond, ada_w, ada_b)
    for l in range(DEPTH):
        with_ctx = l < DEPTH - 1
        mod_l = [m[:, None, :] for m in jnp.split(mod_all[l, :b], 6, axis=-1)]
        mod_c = [jnp.broadcast_to(m[None, :, :], (b, 1, d)) for m in jnp.split(mod_all[l, b:b + 1], 6, axis=-1)]
        p = {
            "ln1_g": ln1_g[l], "ln1_b": ln1_b[l],
            "w_router": jnp.pad(w_router[l], ((0, 0), (0, LANE - N_EXPERTS))).astype(BF16),
            "w_in": permute_w_in(w_in[l]).astype(BF16), "lb": lb_all[:, l], "hgrn_norm_w": hgrn_norm_w[l],
            "gdn_conv_w": gdn_conv_w[l], "gdn_a_log": gdn_a_log[l], "gdn_dt_bias": gdn_dt_bias[l],
            "gdn_norm_w": gdn_norm_w[l], "ssd_conv_w": ssd_conv_w[l], "ssd_conv_b": ssd_conv_b[l],
            "ssd_a_log": ssd_a_log[l], "ssd_dt_bias": ssd_dt_bias[l], "ssd_d": ssd_d[l], "ssd_norm_w": ssd_norm_w[l],
            "w_gate": w_gate[l].astype(BF16), "b_gate": b_gate[l][:, None, :],
            "w_branch": w_branch[l].astype(BF16), "w_out": w_out[l].astype(BF16),
        }
        ffw = (w_ff_gate[l], w_ff_up[l], w_ff_down[l])
        out_c, out_l = token_mixer(xc, xl, mod_c, mod_l, p, fnet_tabs, b, with_ctx)
        xl = channel_mixer(*out_l, b, mod_l[5], ln2_g[l], ln2_b[l], *ffw)
        if with_ctx:
            xc = channel_mixer(*out_c, b, mod_c[5], ln2_g[l], ln2_b[l], *ffw)
    return xl.reshape(b, n_lat, d)
```

```python
import functools
import math

import jax
import jax.numpy as jnp
import numpy as np
from jax import lax
from jax.experimental import pallas as pl
from jax.experimental.pallas import tpu as pltpu

D_MODEL = 1024
DEPTH = 4
GRID_W = 64
N_BRANCH = 4
BRANCH_W = D_MODEL // N_BRANCH
HEAD_DIM = 64
N_HEADS = BRANCH_W // HEAD_DIM
N_PAIRS = N_HEADS // 2
FOURIER_GROUP_W = HEAD_DIM
CHUNK = 64
CONV_K = 5
SSD_STATE = 64
SSD_GROUPS = 2
N_EXPERTS = 16
EC_CAPACITY_FACTOR = 2
DEEPNORM_ALPHA = (2.0 * DEPTH) ** 0.25
NORM_EPS = 1e-6

LANE = 128
SUBLANE = 8
MXU_ROWS = 256
VMEM_LIMIT = 48 * 1024 * 1024
NEG_BIG = -1e30

F32 = jnp.float32
BF16 = jnp.bfloat16

XBC_W = BRANCH_W + 2 * SSD_GROUPS * SSD_STATE
Z_ORDER = ("b_qkv", "a_q", "a_f_fwd", "a_f_bwd", "a_v", "a_g", "b_g", "c_u", "d_xbc", "d_z", "b_a", "b_beta", "d_dt")
REF_SPLITS = (
    ("a_q", BRANCH_W), ("a_f_fwd", BRANCH_W), ("a_f_bwd", BRANCH_W), ("a_v", BRANCH_W), ("a_g", BRANCH_W),
    ("b_qkv", 3 * BRANCH_W), ("b_g", BRANCH_W), ("b_a", 2 * N_HEADS), ("b_beta", 2 * N_HEADS),
    ("c_u", BRANCH_W), ("d_xbc", XBC_W), ("d_z", BRANCH_W), ("d_dt", 2 * N_HEADS),
)
Z_COLS = 3 * BRANCH_W + 5 * BRANCH_W + 2 * BRANCH_W + XBC_W + BRANCH_W + LANE
GDN_BLOCKS = (0, 8, 26)
HGRN_BLOCKS = (3, 4, 5, 6, 7)
FNET_BLOCK = 9
SSD_BLOCKS = (5, 12, 26)
DT_LANE0 = 4 * N_HEADS


def permute_w_in(w_in):
    start, spans = 0, {}
    for name, size in REF_SPLITS:
        spans[name] = (start, start + size)
        start += size
    parts = [w_in[..., spans[n][0]:spans[n][1]] for n in Z_ORDER]
    used = sum(p.shape[-1] for p in parts)
    return jnp.concatenate(parts + [jnp.zeros(w_in.shape[:-1] + (Z_COLS - used,), w_in.dtype)], axis=-1)


def _bdot(a, b):
    return jnp.dot(a.astype(BF16), b.astype(BF16), preferred_element_type=F32)


def _bdot_nt(a, b):
    return lax.dot_general(a.astype(BF16), b.astype(BF16), (((1,), (1,)), ((), ())), preferred_element_type=F32)


def _bdot_tn(a, b):
    return lax.dot_general(a.astype(BF16), b.astype(BF16), (((0,), (0,)), ((), ())), preferred_element_type=F32)


def _split(x, terms):
    out = []
    for _ in range(terms):
        p = x.astype(BF16)
        out.append(p)
        x = x - p.astype(F32)
    return out


def _dot_exact_lhs(m, x, terms=3):
    mb = m.astype(BF16)
    return sum(jnp.dot(mb, p, preferred_element_type=F32) for p in _split(x, terms))


def _dot_exact_rhs(x, m, terms=2):
    mb = m.astype(BF16)
    return sum(jnp.dot(p, mb, preferred_element_type=F32) for p in _split(x, terms))


def _iota2(n, m):
    return lax.broadcasted_iota(jnp.int32, (n, m), 0), lax.broadcasted_iota(jnp.int32, (n, m), 1)


def _sigmoid(x):
    return 1.0 / (1.0 + jnp.exp(-x))


def _softplus(x):
    return jnp.maximum(x, 0.0) + jnp.log(1.0 + jnp.exp(-jnp.abs(x)))


def _conv(x_ref, w_ref, n, n_chunks):
    t = n_chunks * CHUNK
    start = pl.multiple_of(n * CHUNK, CHUNK)
    cur = x_ref[pl.ds(start, CHUNK), :]
    prev_start = pl.multiple_of(jnp.maximum(start - SUBLANE, 0), SUBLANE)
    next_start = pl.multiple_of(jnp.minimum(start + CHUNK, t - SUBLANE), SUBLANE)
    prev = x_ref[pl.ds(prev_start, SUBLANE), :] * jnp.where(n > 0, 1.0, 0.0)
    nxt = x_ref[pl.ds(next_start, SUBLANE), :] * jnp.where(n < n_chunks - 1, 1.0, 0.0)
    ext = jnp.concatenate([prev, cur, nxt], axis=0)
    pad = CONV_K // 2
    acc = None
    n_ext = CHUNK + 2 * SUBLANE
    for k in range(CONV_K):
        shifted = ext if k == pad else pltpu.roll(ext, (pad - k) % n_ext, 0)
        term = shifted[SUBLANE:SUBLANE + CHUNK, :] * w_ref[k:k + 1, :]
        acc = term if acc is None else acc + term
    return acc


PREP_UNROLL = 4


def _interleave(progs):
    live = list(progs)
    while live:
        nxt = []
        for p in live:
            try:
                next(p)
                nxt.append(p)
            except StopIteration:
                pass
        live = nxt


def _chunk_loop(prog, n_chunks):
    def body(i, carry):
        _interleave([prog(i * PREP_UNROLL + j) for j in range(PREP_UNROLL)])
        return carry
    lax.fori_loop(0, n_chunks // PREP_UNROLL, body, 0)


def _backward_chunk(step, nc_c, n_tot):
    return jnp.where(step < nc_c, nc_c - 1 - step, n_tot - 1 - (step - nc_c))


def _chunk_rows(n):
    return pl.ds(pl.multiple_of(n * CHUNK, CHUNK), CHUNK)


def _head_norm_gate(o_s, gate_ref, out_ref, normw, bdm, n_chunks, base):
    def body(n, carry):
        rows = pl.ds(pl.multiple_of(n * MXU_ROWS, MXU_ROWS), MXU_ROWS)
        o = o_s[pl.ds(pl.multiple_of(base + n * MXU_ROWS, MXU_ROWS), MXU_ROWS), :]
        ms = _dot_exact_rhs(o * o, bdm) * (1.0 / HEAD_DIM)
        gt = gate_ref[rows, :]
        out_ref[rows, :] = o * lax.rsqrt(ms + NORM_EPS) * normw[...] * (gt * _sigmoid(gt))
        return carry
    lax.fori_loop(0, n_chunks * CHUNK // MXU_ROWS, body, 0)


def head_block_ones():
    i = np.arange(BRANCH_W)
    return jnp.asarray((i[:, None] // HEAD_DIM) == (i[None, :] // HEAD_DIM), BF16)


def _zspec(t, width, blk, **kw):
    return pl.BlockSpec((None, t, width), lambda i: (i, 0, blk), **kw)


def _cspec(shape, **kw):
    return pl.BlockSpec(shape, lambda *_: (0,) * len(shape), **kw)


def _mixer_out(b, tc, tl):
    assert tc % MXU_ROWS == 0 and tl % MXU_ROWS == 0 and MXU_ROWS % (PREP_UNROLL * CHUNK) == 0
    specs = [pl.BlockSpec((None, tc, BRANCH_W), lambda i: (i, 0, 0)), pl.BlockSpec((None, tl, BRANCH_W), lambda i: (i, 0, 0))]
    shapes = [jax.ShapeDtypeStruct((b, tc, BRANCH_W), F32), jax.ShapeDtypeStruct((b, tl, BRANCH_W), F32)]
    return specs, shapes


_MIXER_PARAMS = pltpu.CompilerParams(dimension_semantics=("parallel",), vmem_limit_bytes=VMEM_LIMIT)


def _pair_blocks(x):
    first = (lax.broadcasted_iota(jnp.int32, x.shape, 1) % LANE) < HEAD_DIM
    zero = jnp.zeros_like(x)
    return jnp.concatenate([jnp.where(first, x, zero), jnp.where(first, zero, x)], axis=0)


def _pair_cols(x, l0):
    lane = lax.broadcasted_iota(jnp.int32, (x.shape[0], LANE), 1)
    return jnp.where(lane < HEAD_DIM, x[:, l0:l0 + 1], x[:, l0 + 1:l0 + 2])


def _pair_iota():
    i, j = _iota2(CHUNK, LANE)
    return i, j % HEAD_DIM


def _pair_order_masks(d):
    i, j = _pair_iota()
    return ((j <= i), (j < i)) if d == 0 else ((j >= i), (j > i))


def _pair_dot(x, y):
    return jnp.dot(x.astype(BF16), _pair_blocks(y.astype(BF16)), preferred_element_type=F32)


def _pair_dot_x3(x, y):
    xh, xl = _split(x, 2)
    yh, yl = (_pair_blocks(t) for t in _split(y, 2))
    d = lambda p, q: jnp.dot(p, q, preferred_element_type=F32)
    return d(xh, yh) + d(xl, yh) + d(xh, yl)


def _head_block_mask():
    r, c = _iota2(LANE, LANE)
    return (r < HEAD_DIM) == (c < HEAD_DIM)


def _pairs(x):
    return [x[:, p * LANE:(p + 1) * LANE] for p in range(N_PAIRS)]


def _unit_lower_inverses(mats):
    i, j = _pair_iota()
    eye = (i == j).astype(F32)
    same4 = (i // 4) == (j // 4)
    d4 = [jnp.where(same4, a, 0.0) for a in mats]
    sq = [_pair_dot(d, d) for d in d4]
    xs = [eye - d for d in d4]
    yield
    xs = [x + _pair_dot(x, q) for x, q in zip(xs, sq)]
    yield
    s = 4
    while s < CHUNK:
        sel = ((i // (2 * s)) == (j // (2 * s))) & ((i // s) != (j // s))
        ox = [_pair_dot(jnp.where(sel, a, 0.0), x) for a, x in zip(mats, xs)]
        yield
        xs = [x - _pair_dot(x, y) for x, y in zip(xs, ox)]
        yield
        s *= 2
    return xs


def _gdn_body(nc_c, nc_l, qkv_c, qkv_l, gate_c, gate_l, sm_c, sm_l, convw, prow, normw, bd,
              out_c, out_l, u_s, w_s, qk_s, qd_s, kd_s, gl_s, o_s, st_s):
    w = BRANCH_W
    bdm = bd[...]
    lane = lax.broadcasted_iota(jnp.int32, (CHUNK, LANE), 1)
    ii, jj = _iota2(CHUNK, CHUNK)
    incl_lower = (jj <= ii).astype(F32)
    masks = [_pair_order_masks(d) for d in range(2)]
    dp = [(d, p) for d in range(2) for p in range(N_PAIRS)]
    n_tot = nc_c + nc_l

    def prep(x_ref, s_ref, n_chunks, base):
        def prog(n):
            y = _conv(x_ref, convw, n, n_chunks)
            y = y * _sigmoid(y)
            q, k, v = y[:, :w], y[:, w:2 * w], y[:, 2 * w:]
            qss, kss = _dot_exact_rhs(q * q, bdm), _dot_exact_rhs(k * k, bdm)
            yield
            q = q * lax.rsqrt(qss + NORM_EPS) * HEAD_DIM ** -0.5
            k = k * lax.rsqrt(kss + NORM_EPS)
            sm = s_ref[_chunk_rows(n), :]
            la = jnp.where(lane < 2 * N_HEADS, -jnp.exp(prow[0:1, :]) * _softplus(sm + prow[1:2, :]), 0.0)
            beta_all = _sigmoid(sm)
            prefix = _dot_exact_lhs(incl_lower, la)
            total = jnp.sum(la, axis=0, keepdims=True)
            g_all = jnp.where(lane < N_HEADS, prefix, total - prefix + la)
            g_t = g_all.T
            qp, kp, vp = _pairs(q), _pairs(k), _pairs(v)
            qkk = [lax.dot_general(jnp.concatenate([qp[p], kp[p]], axis=0).astype(BF16), _pair_blocks(kp[p].astype(BF16)),
                                   (((1,), (1,)), ((), ())), preferred_element_type=F32) for p in range(N_PAIRS)]
            yield
            l0 = [d * N_HEADS + 2 * p for d, p in dp]
            g_col = [_pair_cols(g_all, l) for l in l0]
            g_row = [jnp.concatenate([g_t[l:l + 1, :], g_t[l + 1:l + 2, :]], axis=1) for l in l0]
            beta = [_pair_cols(beta_all, 2 * N_HEADS + l) for l in l0]
            gl = [_pair_cols(total, l) for l in l0]
            decay = [jnp.exp(jnp.where(masks[d][0], g_col[x] - g_row[x], NEG_BIG)) for x, (d, p) in enumerate(dp)]
            qk = [qkk[p][:CHUNK] * decay[x] for x, (d, p) in enumerate(dp)]
            a = [jnp.where(masks[d][1], beta[x] * qkk[p][CHUNK:] * decay[x], 0.0) for x, (d, p) in enumerate(dp)]
            tinv = yield from _unit_lower_inverses(a)
            eg = [jnp.exp(g) for g in g_col]
            rhs = [jnp.concatenate([beta[x] * vp[p], beta[x] * kp[p] * eg[x]], axis=1) for x, (d, p) in enumerate(dp)]
            uw = [_pair_dot(t, r) for t, r in zip(tinv, rhs)]
            yield
            resid = [r - y0 - _pair_dot_x3(m, y0) for r, y0, m in zip(rhs, uw, a)]
            yield
            uw = [y0 + _pair_dot(t, r) for y0, t, r in zip(uw, tinv, resid)]
            yield
            rows = pl.ds(pl.multiple_of(base + n * CHUNK, CHUNK), CHUNK)
            for x, (d, p) in enumerate(dp):
                cols = slice(p * LANE, (p + 1) * LANE)
                u_s[d, rows, cols] = uw[x][:, :LANE]
                w_s[d, rows, cols] = uw[x][:, LANE:].astype(BF16)
                qk_s[d, rows, cols] = qk[x].astype(BF16)
                qd_s[d, rows, cols] = (qp[p] * eg[x]).astype(BF16)
                kd_s[d, rows, cols] = (kp[p] * jnp.exp(gl[x] - g_col[x])).astype(BF16)
            gl_s[pl.ds(base // CHUNK + n, 1), :] = jnp.exp(total)
            yield
        _chunk_loop(prog, n_chunks)

    prep(qkv_c, sm_c, nc_c, 0)
    prep(qkv_l, sm_l, nc_l, nc_c * CHUNK)

    st_s[...] = jnp.zeros_like(st_s)
    o_s[...] = jnp.zeros_like(o_s)
    block = _head_block_mask()
    first_rows = lax.broadcasted_iota(jnp.int32, (LANE, 1), 0) < HEAD_DIM

    def scan_body(step, carry):
        n_dir = (step, _backward_chunk(step, nc_c, n_tot))
        rows = [_chunk_rows(n) for n in n_dir]
        egl_rows = [gl_s[pl.ds(n, 1), :] for n in n_dir]
        tiles = lambda ref, x: ref[dp[x][0], rows[dp[x][0]], dp[x][1] * LANE:(dp[x][1] + 1) * LANE]
        s_prev = [st_s[x] for x in range(len(dp))]
        v_new = [tiles(u_s, x) - jnp.dot(tiles(w_s, x), s_prev[x].astype(BF16), preferred_element_type=F32)
                 for x in range(len(dp))]
        o = [jnp.dot(tiles(qd_s, x), s_prev[x].astype(BF16), preferred_element_type=F32)
             + jnp.dot(tiles(qk_s, x), _pair_blocks(v_new[x].astype(BF16)), preferred_element_type=F32)
             for x in range(len(dp))]
        for x, (d, p) in enumerate(dp):
            l = d * N_HEADS + 2 * p
            egl = jnp.where(first_rows, egl_rows[d][:, l:l + 1], egl_rows[d][:, l + 1:l + 2])
            st_s[x] = egl * s_prev[x] + jnp.where(block, _bdot_tn(tiles(kd_s, x), v_new[x]), 0.0)
            cols = slice(p * LANE, (p + 1) * LANE)
            o_s[rows[d], cols] = o_s[rows[d], cols] + o[x]
        return carry
    lax.fori_loop(0, n_tot, scan_body, 0)

    _head_norm_gate(o_s, gate_c, out_c, normw, bdm, nc_c, 0)
    _head_norm_gate(o_s, gate_l, out_l, normw, bdm, nc_l, nc_c * CHUNK)


def gdn_mixer(zc, zl, conv_w, a_log, dt_bias, norm_w):
    b, tc, _ = zc.shape
    tl = zl.shape[1]
    nc_c, nc_l = tc // CHUNK, tl // CHUNK
    w = BRANCH_W
    qkv_blk, gate_blk, sm_blk = GDN_BLOCKS
    prow = jnp.zeros((SUBLANE, LANE), F32)
    prow = prow.at[0, :2 * N_HEADS].set(a_log.reshape(-1)).at[1, :2 * N_HEADS].set(dt_bias.reshape(-1))
    out_specs, out_shape = _mixer_out(b, tc, tl)
    return pl.pallas_call(
        functools.partial(_gdn_body, nc_c, nc_l),
        grid=(b,),
        in_specs=[_zspec(tc, 3 * w, qkv_blk), _zspec(tl, 3 * w, qkv_blk, pipeline_mode=pl.Buffered(1)),
                  _zspec(tc, w, gate_blk), _zspec(tl, w, gate_blk),
                  _zspec(tc, LANE, sm_blk), _zspec(tl, LANE, sm_blk),
                  _cspec((SUBLANE, 3 * w)), _cspec((SUBLANE, LANE)), _cspec((1, w)), _cspec((w, w))],
        out_specs=out_specs,
        out_shape=out_shape,
        scratch_shapes=[pltpu.VMEM((2, tc + tl, w), F32)] + [pltpu.VMEM((2, tc + tl, w), BF16)] * 4 + [
            pltpu.VMEM((nc_c + nc_l, LANE), F32), pltpu.VMEM((tc + tl, w), F32),
            pltpu.VMEM((2 * N_PAIRS, LANE, LANE), F32)],
        compiler_params=_MIXER_PARAMS,
        name="gdn_mixer",
    )(zc, zl, zc, zl, zc, zl,
      jnp.pad(conv_w, ((0, SUBLANE - CONV_K), (0, 0))), prow, norm_w[None, :], head_block_ones())


GLA_LEVELS = (32, 16, 8, 4, 2, 1)


def _gla_level_tables(d):
    n_lv = len(GLA_LEVELS)
    i, t = _iota2(n_lv * CHUNK, CHUNK)
    sel = jnp.zeros((n_lv * CHUNK, CHUNK), F32)
    r, c = _pair_iota()
    masks = []
    for x, s in enumerate(GLA_LEVELS):
        row = i - x * CHUNK
        bound = 2 * s * (row // (2 * s)) + s - 1 + d
        sel = jnp.where((i // CHUNK == x) & (t == bound), 1.0, sel)
        same = (r // (2 * s)) == (c // (2 * s))
        r_hi, c_hi = (r % (2 * s)) >= s, (c % (2 * s)) >= s
        masks.append(same & (r_hi & ~c_hi if d == 0 else ~r_hi & c_hi))
    return sel, masks


def _hgrn_body(nc_c, nc_l, q_c, q_l, ff_c, ff_l, fb_c, fb_l, v_c, v_l, gate_c, gate_l, lbrow, normw, bd,
               out_c, out_l, att_s, qd_s, kd_s, v_s, gl_s, o_s, st_s):
    bdm = bd[...]
    ii, jj = _iota2(CHUNK, CHUNK)
    incl_lower = (jj <= ii).astype(F32)
    pi, pj = _pair_iota()
    eye = pi == pj
    tables = [_gla_level_tables(d) for d in range(2)]
    dp = [(d, p) for d in range(2) for p in range(N_PAIRS)]
    n_tot = nc_c + nc_l

    def prep(q_ref, f_refs, v_ref, n_chunks, base):
        def prog(n):
            rin = _chunk_rows(n)
            rows = pl.ds(pl.multiple_of(base + n * CHUNK, CHUNK), CHUNK)
            zq = q_ref[rin, :]
            q = zq * _sigmoid(zq)
            v_s[rows, :] = v_ref[rin, :].astype(BF16)
            for d in range(2):
                zf = f_refs[d][rin, :]
                log1m_lb, one_m_lb = lbrow[2 * d:2 * d + 1, :], lbrow[2 * d + 1:2 * d + 2, :]
                e = jnp.exp(-jnp.abs(zf))
                r = 1.0 / (1.0 + e)
                sig, sig_neg = jnp.where(zf >= 0, r, e * r), jnp.where(zf >= 0, e * r, r)
                log_sig = jnp.minimum(zf, 0.0) - jnp.log(1.0 + e)
                lf = jnp.maximum(jnp.log((1.0 - one_m_lb) + one_m_lb * sig), log1m_lb + log_sig)
                k = one_m_lb * sig_neg
                prefix = _dot_exact_lhs(incl_lower, lf)
                diag = _dot_exact_rhs(q * k, bdm)
                yield
                total = jnp.sum(lf, axis=0, keepdims=True)
                g = prefix if d == 0 else total - prefix + lf
                sel, masks = tables[d]
                c_all = _dot_exact_lhs(sel, g)
                yield
                acc = [jnp.where(eye, t, 0.0) for t in _pairs(diag)]
                for x in range(len(GLA_LEVELS)):
                    c = c_all[x * CHUNK:(x + 1) * CHUNK, :]
                    qt = _pairs((q * jnp.exp(jnp.minimum(g - c, 0.0))).astype(BF16))
                    kt = _pairs((k * jnp.exp(jnp.minimum(c - g, 0.0))).astype(BF16))
                    lvl = [lax.dot_general(qt[p], _pair_blocks(kt[p]), (((1,), (1,)), ((), ())),
                                           preferred_element_type=F32) for p in range(N_PAIRS)]
                    acc = [a + jnp.where(masks[x], t, 0.0) for a, t in zip(acc, lvl)]
                    yield
                att_s[d, rows, :] = jnp.concatenate(acc, axis=1).astype(BF16)
                qd_s[d, rows, :] = (q * jnp.exp(g)).astype(BF16)
                kd_s[d, rows, :] = (k * jnp.exp(total - g)).astype(BF16)
                gl_s[d, pl.ds(base // CHUNK + n, 1), :] = jnp.exp(total)
        _chunk_loop(prog, n_chunks)

    prep(q_c, (ff_c, fb_c), v_c, nc_c, 0)
    prep(q_l, (ff_l, fb_l), v_l, nc_l, nc_c * CHUNK)

    st_s[...] = jnp.zeros_like(st_s)
    o_s[...] = jnp.zeros_like(o_s)
    block = _head_block_mask()

    def scan_body(step, carry):
        n_dir = (step, _backward_chunk(step, nc_c, n_tot))
        rows = [_chunk_rows(n) for n in n_dir]
        for x, (d, p) in enumerate(dp):
            cols = slice(p * LANE, (p + 1) * LANE)
            att, qd, kd = att_s[d, rows[d], cols], qd_s[d, rows[d], cols], kd_s[d, rows[d], cols]
            v = v_s[rows[d], cols]
            s_prev = st_s[x]
            o = (jnp.dot(att, _pair_blocks(v), preferred_element_type=F32)
                 + lax.dot_general(qd, s_prev.astype(BF16), (((1,), (1,)), ((), ())), preferred_element_type=F32))
            egl = gl_s[d, pl.ds(n_dir[d], 1), :][:, cols]
            st_s[x] = egl * s_prev + jnp.where(block, _bdot_tn(v, kd), 0.0)
            o_s[rows[d], cols] = o_s[rows[d], cols] + o
        return carry
    lax.fori_loop(0, n_tot, scan_body, 0)

    _head_norm_gate(o_s, gate_c, out_c, normw, bdm, nc_c, 0)
    _head_norm_gate(o_s, gate_l, out_l, normw, bdm, nc_l, nc_c * CHUNK)


def hgrn_mixer(zc, zl, lb, norm_w):
    b, tc, _ = zc.shape
    tl = zl.shape[1]
    nc_c, nc_l = tc // CHUNK, tl // CHUNK
    w = BRANCH_W
    lbrow = jnp.zeros((SUBLANE, w), F32)
    for d in range(2):
        lbrow = lbrow.at[2 * d].set(jnp.log1p(-lb[d])).at[2 * d + 1].set(1.0 - lb[d])
    in_specs, args = [], []
    for blk in HGRN_BLOCKS:
        in_specs += [_zspec(tc, w, blk), _zspec(tl, w, blk, pipeline_mode=pl.Buffered(1))]
        args += [zc, zl]
    out_specs, out_shape = _mixer_out(b, tc, tl)
    return pl.pallas_call(
        functools.partial(_hgrn_body, nc_c, nc_l),
        grid=(b,),
        in_specs=in_specs + [_cspec((SUBLANE, w)), _cspec((1, w)), _cspec((w, w))],
        out_specs=out_specs,
        out_shape=out_shape,
        scratch_shapes=[pltpu.VMEM((2, tc + tl, w), BF16)] * 3 + [
            pltpu.VMEM((tc + tl, w), BF16), pltpu.VMEM((2, nc_c + nc_l, w), F32), pltpu.VMEM((tc + tl, w), F32),
            pltpu.VMEM((2 * N_PAIRS, LANE, LANE), F32)],
        compiler_params=_MIXER_PARAMS,
        name="hgrn_mixer",
    )(*args, lbrow, norm_w[None, :], head_block_ones())


def _ssd_body(nc_c, nc_l, xbc_c, xbc_l, z_c, z_l, sm_c, sm_l, convw, convb, prow, dskip, normw,
              out_c, out_l, att_s, v_s, qd_s, kd_s, x_s, gl_s, o_s, st_s):
    w = BRANCH_W
    gw = SSD_GROUPS * SSD_STATE
    lane = lax.broadcasted_iota(jnp.int32, (CHUNK, LANE), 1)
    ii, jj = _iota2(CHUNK, CHUNK)
    incl_lower = (jj <= ii).astype(F32)
    masks = [_pair_order_masks(d) for d in range(2)]
    dp = [(d, p) for d in range(2) for p in range(N_PAIRS)]
    n_tot = nc_c + nc_l
    assert N_PAIRS == SSD_GROUPS

    def prep(x_ref, s_ref, n_chunks, base):
        def prog(n):
            rows = pl.ds(pl.multiple_of(base + n * CHUNK, CHUNK), CHUNK)
            y = _conv(x_ref, convw, n, n_chunks) + convb[...]
            y = y * _sigmoid(y)
            xs, bs, cs = y[:, :w], y[:, w:w + gw], y[:, w + gw:]
            x_s[rows, :] = xs
            sm = s_ref[_chunk_rows(n), :]
            dt_all = _softplus(sm + prow[1:2, :])
            in_dt = (lane >= DT_LANE0) & (lane < DT_LANE0 + 2 * N_HEADS)
            la = jnp.where(in_dt, -jnp.exp(prow[0:1, :]) * dt_all, 0.0)
            prefix = _dot_exact_lhs(incl_lower, la)
            total = jnp.sum(la, axis=0, keepdims=True)
            g_all = jnp.where(lane < DT_LANE0 + N_HEADS, prefix, total - prefix + la)
            g_t = g_all.T
            bg = [bs[:, g * SSD_STATE:(g + 1) * SSD_STATE] for g in range(SSD_GROUPS)]
            cg = [cs[:, g * SSD_STATE:(g + 1) * SSD_STATE] for g in range(SSD_GROUPS)]
            cb = [_bdot_nt(cg[g], jnp.concatenate([bg[g], bg[g]], axis=0)) for g in range(SSD_GROUPS)]
            yield
            for d, p in dp:
                l = DT_LANE0 + d * N_HEADS + 2 * p
                cols = slice(p * LANE, (p + 1) * LANE)
                g_col = _pair_cols(g_all, l)
                g_row = jnp.concatenate([g_t[l:l + 1, :], g_t[l + 1:l + 2, :]], axis=1)
                decay = jnp.exp(jnp.where(masks[d][0], g_col - g_row, NEG_BIG))
                att_s[d, rows, cols] = (cb[p] * decay).astype(BF16)
                v_s[d, rows, cols] = (xs[:, cols] * _pair_cols(dt_all, l)).astype(BF16)
                qd_s[d, rows, cols] = (jnp.concatenate([cg[p], cg[p]], axis=1) * jnp.exp(g_col)).astype(BF16)
                kd_s[d, rows, cols] = (jnp.concatenate([bg[p], bg[p]], axis=1)
                                       * jnp.exp(_pair_cols(total, l) - g_col)).astype(BF16)
            gl_s[pl.ds(base // CHUNK + n, 1), :] = jnp.exp(total)
            yield
        _chunk_loop(prog, n_chunks)

    prep(xbc_c, sm_c, nc_c, 0)
    prep(xbc_l, sm_l, nc_l, nc_c * CHUNK)

    st_s[...] = jnp.zeros_like(st_s)
    o_s[...] = jnp.zeros_like(o_s)
    block = _head_block_mask()
    first_rows = lax.broadcasted_iota(jnp.int32, (LANE, 1), 0) < SSD_STATE

    def scan_body(step, carry):
        n_dir = (step, _backward_chunk(step, nc_c, n_tot))
        rows = [_chunk_rows(n) for n in n_dir]
        egl_rows = [gl_s[pl.ds(n, 1), :] for n in n_dir]
        for x, (d, p) in enumerate(dp):
            cols = slice(p * LANE, (p + 1) * LANE)
            l = DT_LANE0 + d * N_HEADS + 2 * p
            att, v = att_s[d, rows[d], cols], v_s[d, rows[d], cols]
            qd, kd = qd_s[d, rows[d], cols], kd_s[d, rows[d], cols]
            s_prev = st_s[x]
            o = (jnp.dot(att, _pair_blocks(v), preferred_element_type=F32)
                 + jnp.dot(qd, s_prev.astype(BF16), preferred_element_type=F32))
            egl = jnp.where(first_rows, egl_rows[d][:, l:l + 1], egl_rows[d][:, l + 1:l + 2])
            st_s[x] = egl * s_prev + jnp.where(block, _bdot_tn(kd, v), 0.0)
            o_s[rows[d], cols] = o_s[rows[d], cols] + o
        return carry
    lax.fori_loop(0, n_tot, scan_body, 0)

    def finish(z_ref, out_ref, n_chunks, base):
        def body(n, carry):
            rows_in = pl.ds(pl.multiple_of(base + n * MXU_ROWS, MXU_ROWS), MXU_ROWS)
            rows = pl.ds(pl.multiple_of(n * MXU_ROWS, MXU_ROWS), MXU_ROWS)
            zt = z_ref[rows, :]
            y = (o_s[rows_in, :] + dskip[...] * x_s[rows_in, :]) * (zt * _sigmoid(zt))
            ms = jnp.sum(y * y, axis=1, keepdims=True) * (1.0 / w)
            out_ref[rows, :] = y * lax.rsqrt(ms + NORM_EPS) * normw[...]
            return carry
        lax.fori_loop(0, n_chunks * CHUNK // MXU_ROWS, body, 0)

    finish(z_c, out_c, nc_c, 0)
    finish(z_l, out_l, nc_l, nc_c * CHUNK)


def ssd_mixer(zc, zl, conv_w, conv_b, a_log, dt_bias, d_skip, norm_w):
    b, tc, _ = zc.shape
    tl = zl.shape[1]
    nc_c, nc_l = tc // CHUNK, tl // CHUNK
    w = BRANCH_W
    xbc_blk, z_blk, sm_blk = SSD_BLOCKS
    prow = jnp.zeros((SUBLANE, LANE), F32)
    prow = prow.at[0, DT_LANE0:DT_LANE0 + 2 * N_HEADS].set(a_log.reshape(-1))
    prow = prow.at[1, DT_LANE0:DT_LANE0 + 2 * N_HEADS].set(dt_bias.reshape(-1))
    out_specs, out_shape = _mixer_out(b, tc, tl)
    return pl.pallas_call(
        functools.partial(_ssd_body, nc_c, nc_l),
        grid=(b,),
        in_specs=[_zspec(tc, XBC_W, xbc_blk), _zspec(tl, XBC_W, xbc_blk, pipeline_mode=pl.Buffered(1)),
                  _zspec(tc, w, z_blk), _zspec(tl, w, z_blk),
                  _zspec(tc, LANE, sm_blk), _zspec(tl, LANE, sm_blk),
                  _cspec((SUBLANE, XBC_W)), _cspec((1, XBC_W)), _cspec((SUBLANE, LANE)), _cspec((1, w)), _cspec((1, w))],
        out_specs=out_specs,
        out_shape=out_shape,
        scratch_shapes=[pltpu.VMEM((2, tc + tl, w), BF16)] * 4 + [
            pltpu.VMEM((tc + tl, w), F32), pltpu.VMEM((nc_c + nc_l, LANE), F32), pltpu.VMEM((tc + tl, w), F32),
            pltpu.VMEM((2 * N_PAIRS, LANE, LANE), F32)],
        compiler_params=_MIXER_PARAMS,
        name="ssd_mixer",
    )(zc, zl, zc, zl, zc, zl,
      jnp.pad(conv_w, ((0, SUBLANE - CONV_K), (0, 0))), conv_b[None, :], prow,
      jnp.repeat(d_skip, HEAD_DIM)[None, :], norm_w[None, :])


def _fnet_body(t, ue0, ue1, uo0, uo1, ceh, cel, coh, col, seh, sel, soh, sol, gch_ref, gcl_ref, gsh_ref, gsl_ref,
               out_ref, p_s, q_s):
    d = lambda a, b: jnp.dot(a, b, preferred_element_type=F32)
    half = t // 2
    tile = min(MXU_ROWS, half)

    def channel_dft(n, carry):
        rows = pl.ds(pl.multiple_of(n * tile, tile), tile)
        for parity in range(2):
            lo_ref, hi_ref = ((ue0, ue1), (uo0, uo1))[parity]
            uh, ul = _split(jnp.concatenate([lo_ref[rows, :], hi_ref[rows, :]], axis=1), 2)
            p = d(uh, gch_ref[...]) + d(ul, gch_ref[...]) + d(uh, gcl_ref[...])
            q = d(uh, gsh_ref[...]) + d(ul, gsh_ref[...]) + d(uh, gsl_ref[...])
            p_s[parity, 0, rows, :], p_s[parity, 1, rows, :] = _split(p, 2)
            q_s[parity, 0, rows, :], q_s[parity, 1, rows, :] = _split(q, 2)
        return carry
    lax.fori_loop(0, half // tile, channel_dft, 0)

    def sequence_dft(n, carry):
        rows = pl.ds(pl.multiple_of(n * tile, tile), tile)
        parts = []
        for parity, (ch, cl, sh, sl) in enumerate(((ceh, cel, seh, sel), (coh, col, soh, sol))):
            ph, plo, qh, qlo = p_s[parity, 0], p_s[parity, 1], q_s[parity, 0], q_s[parity, 1]
            re = d(ch[rows, :], ph) + d(cl[rows, :], ph) + d(ch[rows, :], plo)
            im = d(sh[rows, :], qh) + d(sl[rows, :], qh) + d(sh[rows, :], qlo)
            parts.append(re - im)
        out_ref[rows, :] = parts[0] + parts[1]
        out_ref[pl.ds(pl.multiple_of(half + n * tile, tile), tile), :] = parts[0] - parts[1]
        return carry
    lax.fori_loop(0, half // tile, sequence_dft, 0)


def _hi_lo(tab):
    hi = tab.astype(BF16)
    return [hi, (tab - hi.astype(F32)).astype(BF16)]


def _dft_tables(n, scale):
    j = lax.broadcasted_iota(jnp.int32, (n, n), 0)
    k = lax.broadcasted_iota(jnp.int32, (n, n), 1)
    ang = ((j * k) % n).astype(F32) * (2.0 * math.pi / n)
    return _hi_lo(jnp.cos(ang) * scale) + _hi_lo(jnp.sin(ang) * scale)


def _half_dft_tables(t):
    j = lax.broadcasted_iota(jnp.int32, (t // 2, t // 2), 0)
    m = lax.broadcasted_iota(jnp.int32, (t // 2, t // 2), 1)
    out = {}
    for parity in range(2):
        ang = ((j * (2 * m + parity)) % t).astype(F32) * (2.0 * math.pi / t)
        out["c", parity], out["s", parity] = _hi_lo(jnp.cos(ang) * t ** -0.5), _hi_lo(jnp.sin(ang) * t ** -0.5)
    return out["c", 0] + out["c", 1] + out["s", 0] + out["s", 1]


def fnet_tables(t):
    grp = _dft_tables(FOURIER_GROUP_W, FOURIER_GROUP_W ** -0.5)
    n_grp = BRANCH_W // FOURIER_GROUP_W
    grp = [jnp.kron(jnp.eye(n_grp, dtype=F32), g.astype(F32)).astype(BF16) for g in grp]
    return _half_dft_tables(t) + grp


def fnet_mixer(z, tables):
    b, t, _ = z.shape
    w = BRANCH_W
    const = functools.partial(_cspec, pipeline_mode=pl.Buffered(1))
    z2 = z.reshape(b, t // 2, 2 * Z_COLS)
    first = FNET_BLOCK * w // LANE
    u_blocks = (first, first + 1, first + Z_COLS // LANE, first + 1 + Z_COLS // LANE)
    return pl.pallas_call(
        functools.partial(_fnet_body, t),
        grid=(b,),
        in_specs=[_zspec(t // 2, LANE, blk) for blk in u_blocks] + [const((t // 2, t // 2))] * 8 + [const((w, w))] * 4,
        out_specs=pl.BlockSpec((None, t, w), lambda i: (i, 0, 0)),
        out_shape=jax.ShapeDtypeStruct((b, t, w), F32),
        scratch_shapes=[pltpu.VMEM((2, 2, t // 2, w), BF16)] * 2,
        compiler_params=_MIXER_PARAMS,
        name="fnet_mixer",
    )(z2, z2, z2, z2, *tables)


def _ln_rows(x):
    mu = jnp.mean(x, axis=-1, keepdims=True)
    xc = x - mu
    return xc * lax.rsqrt(jnp.mean(xc * xc, axis=-1, keepdims=True) + NORM_EPS)


def _row_spec(width):
    return pl.BlockSpec((MXU_ROWS, width), lambda i: (i, 0))


def _layer_spec(shape, l):
    return pl.BlockSpec((None,) + shape, lambda *_: (l,) + (0,) * len(shape), pipeline_mode=pl.Buffered(1))


def _mod_spec(l, j, row):
    return pl.BlockSpec((None, None, None, 1, D_MODEL), lambda *idx: (l, row(*idx), j, 0, 0))


def _stream_row(t, fixed_row):
    return (lambda i: fixed_row) if fixed_row is not None else (lambda i: i // (t // MXU_ROWS))


def _adaln_body(c_ref, w_ref, b_ref, o_ref):
    cc = c_ref[...]
    act = (cc * _sigmoid(cc)).astype(BF16)
    o_ref[...] = jnp.dot(act, w_ref[...].astype(BF16), preferred_element_type=F32) + b_ref[...]


def adaln_modulation(cond, ada_w, ada_b):
    r, d = cond.shape
    n_layers, _, n = ada_w.shape
    tn = n // 4
    return pl.pallas_call(
        _adaln_body,
        grid=(n_layers, n // tn),
        in_specs=[pl.BlockSpec((r, d), lambda l, j: (0, 0)), pl.BlockSpec((None, d, tn), lambda l, j: (l, 0, j)),
                  pl.BlockSpec((None, 1, tn), lambda l, j: (l, 0, j))],
        out_specs=pl.BlockSpec((None, r, tn), lambda l, j: (l, 0, j)),
        out_shape=jax.ShapeDtypeStruct((n_layers, r, n), F32),
        compiler_params=pltpu.CompilerParams(dimension_semantics=("parallel", "parallel"), vmem_limit_bytes=VMEM_LIMIT),
        name="adaln",
    )(cond, ada_w, ada_b[:, None, :])


def _modproj_body(x_ref, sh_ref, sc_ref, w_ref, z_ref, h_ref):
    h = (_ln_rows(x_ref[...]) * (1.0 + sc_ref[...]) + sh_ref[...]).astype(BF16)
    h_ref[...] = h
    z_ref[...] = jnp.dot(h, w_ref[...], preferred_element_type=F32)


def modulated_project(x, mod, w, l, t, fixed_row):
    m, d = x.shape
    n = w.shape[-1]
    row = _stream_row(t, fixed_row)
    return pl.pallas_call(
        _modproj_body,
        grid=(m // MXU_ROWS,),
        in_specs=[_row_spec(d), _mod_spec(l, 0, row), _mod_spec(l, 1, row), _layer_spec((d, n), l)],
        out_specs=[_row_spec(n), _row_spec(d)],
        out_shape=[jax.ShapeDtypeStruct((m, n), F32), jax.ShapeDtypeStruct((m, d), BF16)],
        compiler_params=_MIXER_PARAMS,
        name="modulated_project",
    )(x, mod, mod, w)


def _merge_body(h_ref, oa_ref, ob_ref, oc_ref, od_ref, x_ref, gate1_ref, sh2_ref, sc2_ref, lng_ref, lnb_ref,
                wg_ref, bg_ref, wb_ref, wo_ref, wr_ref, xo_ref, h2_ref, aff_ref):
    h = h_ref[...]
    acc = None
    for g, o_ref in enumerate((oa_ref, ob_ref, oc_ref, od_ref)):
        gate = _sigmoid(jnp.dot(h, wg_ref[g], preferred_element_type=F32) + bg_ref[g])
        t = gate * jnp.dot(o_ref[...].astype(BF16), wb_ref[g], preferred_element_type=F32)
        acc = t if acc is None else acc + t
    y = jnp.dot(acc.astype(BF16), wo_ref[...], preferred_element_type=F32)
    x_new = _ln_rows(DEEPNORM_ALPHA * x_ref[...] + gate1_ref[...] * y) * lng_ref[...] + lnb_ref[...]
    xo_ref[...] = x_new
    h2 = (_ln_rows(x_new) * (1.0 + sc2_ref[...]) + sh2_ref[...]).astype(BF16)
    h2_ref[...] = h2
    logits = jnp.dot(h2, wr_ref[...], preferred_element_type=F32)
    valid = lax.broadcasted_iota(jnp.int32, logits.shape, 1) < N_EXPERTS
    logits = jnp.where(valid, logits, NEG_BIG)
    e = jnp.exp(logits - jnp.max(logits, axis=-1, keepdims=True))
    aff_ref[...] = e / jnp.sum(e, axis=-1, keepdims=True)


def merge_and_norm(h, outs, x, mod, p, l, t, fixed_row):
    m, d = x.shape
    w = outs[0].shape[1]
    row = _stream_row(t, fixed_row)
    stacked = lambda name: _layer_spec(p[name].shape[1:], l)
    return pl.pallas_call(
        _merge_body,
        grid=(m // MXU_ROWS,),
        in_specs=[_row_spec(d)] + [_row_spec(w)] * N_BRANCH + [_row_spec(d)] + [_mod_spec(l, j, row) for j in (2, 3, 4)]
        + [stacked(name) for name in ("ln1_g", "ln1_b", "w_gate", "b_gate", "w_branch", "w_out", "w_router")],
        out_specs=[_row_spec(d), _row_spec(d), _row_spec(LANE)],
        out_shape=[jax.ShapeDtypeStruct((m, d), F32), jax.ShapeDtypeStruct((m, d), BF16),
                   jax.ShapeDtypeStruct((m, LANE), F32)],
        compiler_params=_MIXER_PARAMS,
        name="merge_and_norm",
    )(h, *outs, x, mod, mod, mod, p["ln1_g"], p["ln1_b"], p["w_gate"], p["b_gate"], p["w_branch"], p["w_out"], p["w_router"])


def _expert_body(x_ref, wt_ref, wg_ref, wu_ref, wd_ref, yh_ref, yl_ref, wg_s, wu_s, wd_s):
    @pl.when(pl.program_id(1) == 0)
    def _():
        wg_s[...] = wg_ref[...].astype(BF16)
        wu_s[...] = wu_ref[...].astype(BF16)
        wd_s[...] = wd_ref[...].astype(BF16)

    bb, cap, d = x_ref.shape
    x = x_ref[...].reshape(bb * cap, d)
    gate = jnp.dot(x, wg_s[...], preferred_element_type=F32)
    up = jnp.dot(x, wu_s[...], preferred_element_type=F32)
    hid = (gate * _sigmoid(gate) * up).astype(BF16)
    y = jnp.dot(hid, wd_s[...], preferred_element_type=F32) * wt_ref[...].reshape(bb * cap, 1)
    yh, yl = _split(y, 2)
    yh_ref[...] = yh.reshape(yh_ref.shape)
    yl_ref[...] = yl.reshape(yl_ref.shape)


def expert_swiglu(xe, weight, w_ff_gate, w_ff_up, w_ff_down, l):
    b, e, cap, d = xe.shape
    f = w_ff_gate.shape[-1]
    bb = min(b, max(1, MXU_ROWS // cap))
    x_spec = pl.BlockSpec((bb, None, cap, d), lambda ei, bi: (bi, ei, 0, 0))
    wt_spec = pl.BlockSpec((bb, None, cap, 1), lambda ei, bi: (bi, ei, 0, 0))
    w_spec = lambda shape: pl.BlockSpec((None, None) + shape, lambda ei, bi: (l, ei, 0, 0))
    return pl.pallas_call(
        _expert_body,
        grid=(e, b // bb),
        in_specs=[x_spec, wt_spec, w_spec((d, f)), w_spec((d, f)), w_spec((f, d))],
        out_specs=[x_spec, x_spec],
        out_shape=[jax.ShapeDtypeStruct(xe.shape, BF16)] * 2,
        scratch_shapes=[pltpu.VMEM((d, f), BF16), pltpu.VMEM((d, f), BF16), pltpu.VMEM((f, d), BF16)],
        compiler_params=pltpu.CompilerParams(dimension_semantics=("parallel", "arbitrary"),
                                             vmem_limit_bytes=VMEM_LIMIT),
        name="expert_swiglu",
    )(xe, weight[..., None], w_ff_gate, w_ff_up, w_ff_down)


def _combine_body(n_slots, idx_ref, yh_ref, yl_ref, x_ref, gate2_ref, lng_ref, lnb_ref, o_ref):
    tile = pl.program_id(1)
    token = tile * MXU_ROWS + lax.broadcasted_iota(jnp.int32, (MXU_ROWS, n_slots), 0)
    onehot = jnp.where(idx_ref[...] == token, 1.0, 0.0).astype(BF16)
    moe = (jnp.dot(onehot, yh_ref[...], preferred_element_type=F32)
           + jnp.dot(onehot, yl_ref[...], preferred_element_type=F32))
    o_ref[...] = _ln_rows(DEEPNORM_ALPHA * x_ref[...] + gate2_ref[...] * moe) * lng_ref[...] + lnb_ref[...]


def combine_and_norm(idx, yh, yl, x, mod, ln_g, ln_b, l, fixed_row):
    b, t, d = x.shape
    n_slots = idx.shape[-1]
    per_sample = lambda shape: pl.BlockSpec((None,) + shape, lambda i, j: (i, 0, 0))
    tile = pl.BlockSpec((None, MXU_ROWS, d), lambda i, j: (i, j, 0))
    vec = _layer_spec((1, d), l)
    row = (lambda i, j: fixed_row) if fixed_row is not None else (lambda i, j: i)
    return pl.pallas_call(
        functools.partial(_combine_body, n_slots),
        grid=(b, t // MXU_ROWS),
        in_specs=[per_sample((1, n_slots)), per_sample((n_slots, d)), per_sample((n_slots, d)), tile,
                  _mod_spec(l, 5, row), vec, vec],
        out_specs=tile,
        out_shape=jax.ShapeDtypeStruct((b, t, d), F32),
        compiler_params=pltpu.CompilerParams(dimension_semantics=("parallel", "arbitrary"),
                                             vmem_limit_bytes=VMEM_LIMIT),
        name="combine_and_norm",
    )(idx, yh, yl, x, mod, ln_g, ln_b)


def sincos_grid(rows, cols, dim):
    quarter = dim // 4
    omega = 1.0 / (10000.0 ** (jnp.arange(quarter, dtype=F32) / quarter))
    er = jnp.arange(rows, dtype=F32)[:, None] * omega
    ec = jnp.arange(cols, dtype=F32)[:, None] * omega
    er = jnp.concatenate([jnp.sin(er), jnp.cos(er)], axis=-1)
    ec = jnp.concatenate([jnp.sin(ec), jnp.cos(ec)], axis=-1)
    emb = jnp.concatenate([jnp.broadcast_to(er[:, None, :], (rows, cols, dim // 2)),
                           jnp.broadcast_to(ec[None, :, :], (rows, cols, dim // 2))], axis=-1)
    return emb.reshape(rows * cols, dim)


def hgrn_lower_bounds(logits):
    cum = jnp.cumsum(jax.nn.softmax(logits.astype(F32), axis=1), axis=1)
    return cum - cum[:, :1]


def _flat(a):
    return a.reshape(-1, a.shape[-1])


def token_mixer(xc, xl, mod, p, fnet_tabs, b, l, with_ctx):
    tc, tl = xc.shape[0] // b, xl.shape[0] // b
    zc, hc = modulated_project(xc, mod, p["w_in"], l, tc, b)
    zl, hl = modulated_project(xl, mod, p["w_in"], l, tl, None)
    zc, zl = zc.reshape(b, tc, Z_COLS), zl.reshape(b, tl, Z_COLS)
    a_c, a_l = hgrn_mixer(zc, zl, p["lb"][:, l], p["hgrn_norm_w"][l])
    b_c, b_l = gdn_mixer(zc, zl, p["gdn_conv_w"][l], p["gdn_a_log"][l], p["gdn_dt_bias"][l], p["gdn_norm_w"][l])
    d_c, d_l = ssd_mixer(zc, zl, p["ssd_conv_w"][l], p["ssd_conv_b"][l], p["ssd_a_log"][l], p["ssd_dt_bias"][l],
                         p["ssd_d"][l], p["ssd_norm_w"][l])

    def merged(h, outs, x, t, fixed_row):
        return merge_and_norm(h, tuple(_flat(o) for o in outs), x, mod, p, l, t, fixed_row)

    out_l = merged(hl, (a_l, b_l, fnet_mixer(zl, fnet_tabs[1]), d_l), xl, tl, None)
    out_c = merged(hc, (a_c, b_c, fnet_mixer(zc, fnet_tabs[0]), d_c), xc, tc, b) if with_ctx else None
    return out_c, out_l


def channel_mixer(x_mid, h, aff, b, mod, p, l, fixed_row):
    t_ = h.shape[0] // b
    d = h.shape[-1]
    h = h.reshape(b, t_, d)
    cap = EC_CAPACITY_FACTOR * t_ // N_EXPERTS
    aff = aff[:, :N_EXPERTS].reshape(b, t_, N_EXPERTS)
    weight, idx = lax.top_k(jnp.swapaxes(aff, 1, 2), cap)
    xe = h[jnp.arange(b)[:, None, None], idx]
    yh, yl = expert_swiglu(xe, weight, p["w_ff_gate"], p["w_ff_up"], p["w_ff_down"], l)
    n_slots = N_EXPERTS * cap
    x_new = combine_and_norm(idx.reshape(b, 1, n_slots), yh.reshape(b, n_slots, d), yl.reshape(b, n_slots, d),
                             x_mid.reshape(b, t_, d), mod, p["ln2_g"], p["ln2_b"], l, fixed_row)
    return _flat(x_new)


def kernel(x, c, ctx, c_ctx, ada_w, ada_b, w_in, hgrn_lb_logits, hgrn_norm_w,
           gdn_conv_w, gdn_a_log, gdn_dt_bias, gdn_norm_w,
           ssd_conv_w, ssd_conv_b, ssd_a_log, ssd_dt_bias, ssd_d, ssd_norm_w,
           w_gate, b_gate, w_branch, w_out, ln1_g, ln1_b,
           w_router, w_ff_gate, w_ff_up, w_ff_down, ln2_g, ln2_b):
    b, n_lat, d = x.shape
    n_ctx = ctx.shape[1]
    rows = n_lat // GRID_W
    xl = _flat(x + sincos_grid(rows, GRID_W, D_MODEL).astype(x.dtype))
    xc = _flat(ctx)
    fnet_tabs = (fnet_tables(n_ctx), fnet_tables(n_lat))
    cond = jnp.concatenate([c, c_ctx[None, :], jnp.zeros((-(b + 1) % SUBLANE, d), c.dtype)], axis=0)
    mod = adaln_modulation(cond, ada_w, ada_b).reshape(DEPTH, cond.shape[0], 6, 1, d)
    p = {
        "w_in": permute_w_in(w_in).astype(BF16), "lb": hgrn_lower_bounds(hgrn_lb_logits), "hgrn_norm_w": hgrn_norm_w,
        "gdn_conv_w": gdn_conv_w, "gdn_a_log": gdn_a_log, "gdn_dt_bias": gdn_dt_bias, "gdn_norm_w": gdn_norm_w,
        "ssd_conv_w": ssd_conv_w, "ssd_conv_b": ssd_conv_b, "ssd_a_log": ssd_a_log, "ssd_dt_bias": ssd_dt_bias,
        "ssd_d": ssd_d, "ssd_norm_w": ssd_norm_w,
        "w_gate": w_gate.astype(BF16), "b_gate": b_gate[:, :, None, :], "w_branch": w_branch.astype(BF16),
        "w_out": w_out.astype(BF16), "ln1_g": ln1_g[:, None, :], "ln1_b": ln1_b[:, None, :],
        "w_router": jnp.pad(w_router, ((0, 0), (0, 0), (0, LANE - N_EXPERTS))).astype(BF16),
        "w_ff_gate": w_ff_gate, "w_ff_up": w_ff_up, "w_ff_down": w_ff_down,
        "ln2_g": ln2_g[:, None, :], "ln2_b": ln2_b[:, None, :],
    }
    for l in range(DEPTH):
        with_ctx = l < DEPTH - 1
        out_c, out_l = token_mixer(xc, xl, mod, p, fnet_tabs, b, l, with_ctx)
        xl = channel_mixer(*out_l, b, mod, p, l, None)
        if with_ctx:
            xc = channel_mixer(*out_c, b, mod, p, l, b)
    return xl.reshape(b, n_lat, d)
```

```python
import functools
import math

import jax
import jax.numpy as jnp
import numpy as np
from jax import lax
from jax.experimental import pallas as pl
from jax.experimental.pallas import tpu as pltpu

D_MODEL = 1024
DEPTH = 4
GRID_W = 64
N_BRANCH = 4
BRANCH_W = D_MODEL // N_BRANCH
HEAD_DIM = 64
N_HEADS = BRANCH_W // HEAD_DIM
N_PAIRS = N_HEADS // 2
FOURIER_GROUP_W = HEAD_DIM
CHUNK = 64
CONV_K = 5
SSD_STATE = 64
SSD_GROUPS = 2
N_EXPERTS = 16
EC_CAPACITY_FACTOR = 2
DEEPNORM_ALPHA = (2.0 * DEPTH) ** 0.25
NORM_EPS = 1e-6

LANE = 128
SUBLANE = 8
MXU_ROWS = 256
VMEM_LIMIT = 48 * 1024 * 1024
NEG_BIG = -1e30

F32 = jnp.float32
BF16 = jnp.bfloat16

XBC_W = BRANCH_W + 2 * SSD_GROUPS * SSD_STATE
Z_ORDER = ("b_qkv", "a_q", "a_f_fwd", "a_f_bwd", "a_v", "a_g", "b_g", "c_u", "d_xbc", "d_z", "b_a", "b_beta", "d_dt")
REF_SPLITS = (
    ("a_q", BRANCH_W), ("a_f_fwd", BRANCH_W), ("a_f_bwd", BRANCH_W), ("a_v", BRANCH_W), ("a_g", BRANCH_W),
    ("b_qkv", 3 * BRANCH_W), ("b_g", BRANCH_W), ("b_a", 2 * N_HEADS), ("b_beta", 2 * N_HEADS),
    ("c_u", BRANCH_W), ("d_xbc", XBC_W), ("d_z", BRANCH_W), ("d_dt", 2 * N_HEADS),
)
Z_COLS = 3 * BRANCH_W + 5 * BRANCH_W + 2 * BRANCH_W + XBC_W + BRANCH_W + LANE
GDN_BLOCKS = (0, 8, 26)
HGRN_BLOCKS = (3, 4, 5, 6, 7)
FNET_BLOCK = 9
SSD_BLOCKS = (5, 12, 26)
DT_LANE0 = 4 * N_HEADS


def permute_w_in(w_in):
    start, spans = 0, {}
    for name, size in REF_SPLITS:
        spans[name] = (start, start + size)
        start += size
    parts = [w_in[..., spans[n][0]:spans[n][1]] for n in Z_ORDER]
    used = sum(p.shape[-1] for p in parts)
    return jnp.concatenate(parts + [jnp.zeros(w_in.shape[:-1] + (Z_COLS - used,), w_in.dtype)], axis=-1)


def _bdot(a, b):
    return jnp.dot(a.astype(BF16), b.astype(BF16), preferred_element_type=F32)


def _bdot_nt(a, b):
    return lax.dot_general(a.astype(BF16), b.astype(BF16), (((1,), (1,)), ((), ())), preferred_element_type=F32)


def _bdot_tn(a, b):
    return lax.dot_general(a.astype(BF16), b.astype(BF16), (((0,), (0,)), ((), ())), preferred_element_type=F32)


def _split(x, terms):
    out = []
    for _ in range(terms):
        p = x.astype(BF16)
        out.append(p)
        x = x - p.astype(F32)
    return out


def _dot_exact_lhs(m, x, terms=3):
    mb = m.astype(BF16)
    return sum(jnp.dot(mb, p, preferred_element_type=F32) for p in _split(x, terms))


def _dot_exact_rhs(x, m, terms=2):
    mb = m.astype(BF16)
    return sum(jnp.dot(p, mb, preferred_element_type=F32) for p in _split(x, terms))


def _iota2(n, m):
    return lax.broadcasted_iota(jnp.int32, (n, m), 0), lax.broadcasted_iota(jnp.int32, (n, m), 1)


def _sigmoid(x):
    return 1.0 / (1.0 + jnp.exp(-x))


def _softplus(x):
    return jnp.maximum(x, 0.0) + jnp.log(1.0 + jnp.exp(-jnp.abs(x)))


def _conv(x_ref, w_ref, n, n_chunks):
    t = n_chunks * CHUNK
    start = pl.multiple_of(n * CHUNK, CHUNK)
    cur = x_ref[pl.ds(start, CHUNK), :]
    prev_start = pl.multiple_of(jnp.maximum(start - SUBLANE, 0), SUBLANE)
    next_start = pl.multiple_of(jnp.minimum(start + CHUNK, t - SUBLANE), SUBLANE)
    prev = x_ref[pl.ds(prev_start, SUBLANE), :] * jnp.where(n > 0, 1.0, 0.0)
    nxt = x_ref[pl.ds(next_start, SUBLANE), :] * jnp.where(n < n_chunks - 1, 1.0, 0.0)
    ext = jnp.concatenate([prev, cur, nxt], axis=0)
    pad = CONV_K // 2
    acc = None
    n_ext = CHUNK + 2 * SUBLANE
    for k in range(CONV_K):
        shifted = ext if k == pad else pltpu.roll(ext, (pad - k) % n_ext, 0)
        term = shifted[SUBLANE:SUBLANE + CHUNK, :] * w_ref[k:k + 1, :]
        acc = term if acc is None else acc + term
    return acc


PREP_UNROLL = 4


def _interleave(progs):
    live = list(progs)
    while live:
        nxt = []
        for p in live:
            try:
                next(p)
                nxt.append(p)
            except StopIteration:
                pass
        live = nxt


def _chunk_loop(prog, n_chunks):
    def body(i, carry):
        _interleave([prog(i * PREP_UNROLL + j) for j in range(PREP_UNROLL)])
        return carry
    lax.fori_loop(0, n_chunks // PREP_UNROLL, body, 0)


def _backward_chunk(step, nc_c, n_tot):
    return jnp.where(step < nc_c, nc_c - 1 - step, n_tot - 1 - (step - nc_c))


def _chunk_rows(n):
    return pl.ds(pl.multiple_of(n * CHUNK, CHUNK), CHUNK)


def _head_norm_gate(o_s, gate_ref, out_ref, normw, bdm, n_chunks, base):
    def body(n, carry):
        rows = pl.ds(pl.multiple_of(n * MXU_ROWS, MXU_ROWS), MXU_ROWS)
        o = o_s[pl.ds(pl.multiple_of(base + n * MXU_ROWS, MXU_ROWS), MXU_ROWS), :]
        ms = _dot_exact_rhs(o * o, bdm) * (1.0 / HEAD_DIM)
        gt = gate_ref[rows, :]
        out_ref[rows, :] = o * lax.rsqrt(ms + NORM_EPS) * normw[...] * (gt * _sigmoid(gt))
        return carry
    lax.fori_loop(0, n_chunks * CHUNK // MXU_ROWS, body, 0)


def head_block_ones():
    i = np.arange(BRANCH_W)
    return jnp.asarray((i[:, None] // HEAD_DIM) == (i[None, :] // HEAD_DIM), BF16)


def _zspec(t, width, blk, **kw):
    return pl.BlockSpec((None, t, width), lambda i: (i, 0, blk), **kw)


def _cspec(shape, **kw):
    return pl.BlockSpec(shape, lambda *_: (0,) * len(shape), **kw)


def _mixer_out(b, tc, tl):
    assert tc % MXU_ROWS == 0 and tl % MXU_ROWS == 0 and MXU_ROWS % (PREP_UNROLL * CHUNK) == 0
    specs = [pl.BlockSpec((None, tc, BRANCH_W), lambda i: (i, 0, 0)), pl.BlockSpec((None, tl, BRANCH_W), lambda i: (i, 0, 0))]
    shapes = [jax.ShapeDtypeStruct((b, tc, BRANCH_W), F32), jax.ShapeDtypeStruct((b, tl, BRANCH_W), F32)]
    return specs, shapes


_MIXER_PARAMS = pltpu.CompilerParams(dimension_semantics=("parallel",), vmem_limit_bytes=VMEM_LIMIT)


def _pair_blocks(x):
    first = (lax.broadcasted_iota(jnp.int32, x.shape, 1) % LANE) < HEAD_DIM
    zero = jnp.zeros_like(x)
    return jnp.concatenate([jnp.where(first, x, zero), jnp.where(first, zero, x)], axis=0)


def _pair_cols(x, l0):
    lane = lax.broadcasted_iota(jnp.int32, (x.shape[0], LANE), 1)
    return jnp.where(lane < HEAD_DIM, x[:, l0:l0 + 1], x[:, l0 + 1:l0 + 2])


def _pair_iota():
    i, j = _iota2(CHUNK, LANE)
    return i, j % HEAD_DIM


def _pair_order_masks(d):
    i, j = _pair_iota()
    return ((j <= i), (j < i)) if d == 0 else ((j >= i), (j > i))


def _pair_dot(x, y):
    return jnp.dot(x.astype(BF16), _pair_blocks(y.astype(BF16)), preferred_element_type=F32)


def _pair_dot_x3(x, y):
    xh, xl = _split(x, 2)
    yh, yl = (_pair_blocks(t) for t in _split(y, 2))
    d = lambda p, q: jnp.dot(p, q, preferred_element_type=F32)
    return d(xh, yh) + d(xl, yh) + d(xh, yl)


def _head_block_mask():
    r, c = _iota2(LANE, LANE)
    return (r < HEAD_DIM) == (c < HEAD_DIM)


def _pairs(x):
    return [x[:, p * LANE:(p + 1) * LANE] for p in range(N_PAIRS)]


def _unit_lower_inverses(mats):
    i, j = _pair_iota()
    eye = (i == j).astype(F32)
    same4 = (i // 4) == (j // 4)
    d4 = [jnp.where(same4, a, 0.0) for a in mats]
    sq = [_pair_dot(d, d) for d in d4]
    xs = [eye - d for d in d4]
    yield
    xs = [x + _pair_dot(x, q) for x, q in zip(xs, sq)]
    yield
    s = 4
    while s < CHUNK:
        sel = ((i // (2 * s)) == (j // (2 * s))) & ((i // s) != (j // s))
        ox = [_pair_dot(jnp.where(sel, a, 0.0), x) for a, x in zip(mats, xs)]
        yield
        xs = [x - _pair_dot(x, y) for x, y in zip(xs, ox)]
        yield
        s *= 2
    return xs


def _gdn_body(nc_c, nc_l, qkv_c, qkv_l, gate_c, gate_l, sm_c, sm_l, convw, prow, normw, bd,
              out_c, out_l, u_s, w_s, qk_s, qd_s, kd_s, gl_s, o_s, st_s):
    w = BRANCH_W
    bdm = bd[...]
    lane = lax.broadcasted_iota(jnp.int32, (CHUNK, LANE), 1)
    ii, jj = _iota2(CHUNK, CHUNK)
    incl_lower = (jj <= ii).astype(F32)
    masks = [_pair_order_masks(d) for d in range(2)]
    dp = [(d, p) for d in range(2) for p in range(N_PAIRS)]
    n_tot = nc_c + nc_l

    def prep(x_ref, s_ref, n_chunks, base):
        def prog(n):
            y = _conv(x_ref, convw, n, n_chunks)
            y = y * _sigmoid(y)
            q, k, v = y[:, :w], y[:, w:2 * w], y[:, 2 * w:]
            qss, kss = _dot_exact_rhs(q * q, bdm), _dot_exact_rhs(k * k, bdm)
            yield
            q = q * lax.rsqrt(qss + NORM_EPS) * HEAD_DIM ** -0.5
            k = k * lax.rsqrt(kss + NORM_EPS)
            sm = s_ref[_chunk_rows(n), :]
            la = jnp.where(lane < 2 * N_HEADS, -jnp.exp(prow[0:1, :]) * _softplus(sm + prow[1:2, :]), 0.0)
            beta_all = _sigmoid(sm)
            prefix = _dot_exact_lhs(incl_lower, la)
            total = jnp.sum(la, axis=0, keepdims=True)
            g_all = jnp.where(lane < N_HEADS, prefix, total - prefix + la)
            g_t = g_all.T
            qp, kp, vp = _pairs(q), _pairs(k), _pairs(v)
            qkk = [lax.dot_general(jnp.concatenate([qp[p], kp[p]], axis=0).astype(BF16), _pair_blocks(kp[p].astype(BF16)),
                                   (((1,), (1,)), ((), ())), preferred_element_type=F32) for p in range(N_PAIRS)]
            yield
            l0 = [d * N_HEADS + 2 * p for d, p in dp]
            g_col = [_pair_cols(g_all, l) for l in l0]
            g_row = [jnp.concatenate([g_t[l:l + 1, :], g_t[l + 1:l + 2, :]], axis=1) for l in l0]
            beta = [_pair_cols(beta_all, 2 * N_HEADS + l) for l in l0]
            gl = [_pair_cols(total, l) for l in l0]
            decay = [jnp.exp(jnp.where(masks[d][0], g_col[x] - g_row[x], NEG_BIG)) for x, (d, p) in enumerate(dp)]
            qk = [qkk[p][:CHUNK] * decay[x] for x, (d, p) in enumerate(dp)]
            a = [jnp.where(masks[d][1], beta[x] * qkk[p][CHUNK:] * decay[x], 0.0) for x, (d, p) in enumerate(dp)]
            tinv = yield from _unit_lower_inverses(a)
            eg = [jnp.exp(g) for g in g_col]
            rhs = [jnp.concatenate([beta[x] * vp[p], beta[x] * kp[p] * eg[x]], axis=1) for x, (d, p) in enumerate(dp)]
            uw = [_pair_dot(t, r) for t, r in zip(tinv, rhs)]
            yield
            resid = [r - y0 - _pair_dot_x3(m, y0) for r, y0, m in zip(rhs, uw, a)]
            yield
            uw = [y0 + _pair_dot(t, r) for y0, t, r in zip(uw, tinv, resid)]
            yield
            rows = pl.ds(pl.multiple_of(base + n * CHUNK, CHUNK), CHUNK)
            for x, (d, p) in enumerate(dp):
                cols = slice(p * LANE, (p + 1) * LANE)
                u_s[d, rows, cols] = uw[x][:, :LANE]
                w_s[d, rows, cols] = uw[x][:, LANE:].astype(BF16)
                qk_s[d, rows, cols] = qk[x].astype(BF16)
                qd_s[d, rows, cols] = (qp[p] * eg[x]).astype(BF16)
                kd_s[d, rows, cols] = (kp[p] * jnp.exp(gl[x] - g_col[x])).astype(BF16)
            gl_s[pl.ds(base // CHUNK + n, 1), :] = jnp.exp(total)
            yield
        _chunk_loop(prog, n_chunks)

    prep(qkv_c, sm_c, nc_c, 0)
    prep(qkv_l, sm_l, nc_l, nc_c * CHUNK)

    st_s[...] = jnp.zeros_like(st_s)
    o_s[...] = jnp.zeros_like(o_s)
    block = _head_block_mask()
    first_rows = lax.broadcasted_iota(jnp.int32, (LANE, 1), 0) < HEAD_DIM

    def scan_body(step, carry):
        n_dir = (step, _backward_chunk(step, nc_c, n_tot))
        rows = [_chunk_rows(n) for n in n_dir]
        egl_rows = [gl_s[pl.ds(n, 1), :] for n in n_dir]
        tiles = lambda ref, x: ref[dp[x][0], rows[dp[x][0]], dp[x][1] * LANE:(dp[x][1] + 1) * LANE]
        s_prev = [st_s[x] for x in range(len(dp))]
        v_new = [tiles(u_s, x) - jnp.dot(tiles(w_s, x), s_prev[x].astype(BF16), preferred_element_type=F32)
                 for x in range(len(dp))]
        o = [jnp.dot(tiles(qd_s, x), s_prev[x].astype(BF16), preferred_element_type=F32)
             + jnp.dot(tiles(qk_s, x), _pair_blocks(v_new[x].astype(BF16)), preferred_element_type=F32)
             for x in range(len(dp))]
        for x, (d, p) in enumerate(dp):
            l = d * N_HEADS + 2 * p
            egl = jnp.where(first_rows, egl_rows[d][:, l:l + 1], egl_rows[d][:, l + 1:l + 2])
            st_s[x] = egl * s_prev[x] + jnp.where(block, _bdot_tn(tiles(kd_s, x), v_new[x]), 0.0)
            cols = slice(p * LANE, (p + 1) * LANE)
            o_s[rows[d], cols] = o_s[rows[d], cols] + o[x]
        return carry
    lax.fori_loop(0, n_tot, scan_body, 0)

    _head_norm_gate(o_s, gate_c, out_c, normw, bdm, nc_c, 0)
    _head_norm_gate(o_s, gate_l, out_l, normw, bdm, nc_l, nc_c * CHUNK)


def gdn_mixer(zc, zl, conv_w, a_log, dt_bias, norm_w):
    b, tc, _ = zc.shape
    tl = zl.shape[1]
    nc_c, nc_l = tc // CHUNK, tl // CHUNK
    w = BRANCH_W
    qkv_blk, gate_blk, sm_blk = GDN_BLOCKS
    prow = jnp.zeros((SUBLANE, LANE), F32)
    prow = prow.at[0, :2 * N_HEADS].set(a_log.reshape(-1)).at[1, :2 * N_HEADS].set(dt_bias.reshape(-1))
    out_specs, out_shape = _mixer_out(b, tc, tl)
    return pl.pallas_call(
        functools.partial(_gdn_body, nc_c, nc_l),
        grid=(b,),
        in_specs=[_zspec(tc, 3 * w, qkv_blk), _zspec(tl, 3 * w, qkv_blk, pipeline_mode=pl.Buffered(1)),
                  _zspec(tc, w, gate_blk), _zspec(tl, w, gate_blk),
                  _zspec(tc, LANE, sm_blk), _zspec(tl, LANE, sm_blk),
                  _cspec((SUBLANE, 3 * w)), _cspec((SUBLANE, LANE)), _cspec((1, w)), _cspec((w, w))],
        out_specs=out_specs,
        out_shape=out_shape,
        scratch_shapes=[pltpu.VMEM((2, tc + tl, w), F32)] + [pltpu.VMEM((2, tc + tl, w), BF16)] * 4 + [
            pltpu.VMEM((nc_c + nc_l, LANE), F32), pltpu.VMEM((tc + tl, w), F32),
            pltpu.VMEM((2 * N_PAIRS, LANE, LANE), F32)],
        compiler_params=_MIXER_PARAMS,
        name="gdn_mixer",
    )(zc, zl, zc, zl, zc, zl,
      jnp.pad(conv_w, ((0, SUBLANE - CONV_K), (0, 0))), prow, norm_w[None, :], head_block_ones())


GLA_LEVELS = (32, 16, 8, 4, 2, 1)


def _gla_level_tables(d):
    n_lv = len(GLA_LEVELS)
    i, t = _iota2(n_lv * CHUNK, CHUNK)
    sel = jnp.zeros((n_lv * CHUNK, CHUNK), F32)
    r, c = _pair_iota()
    masks = []
    for x, s in enumerate(GLA_LEVELS):
        row = i - x * CHUNK
        bound = 2 * s * (row // (2 * s)) + s - 1 + d
        sel = jnp.where((i // CHUNK == x) & (t == bound), 1.0, sel)
        same = (r // (2 * s)) == (c // (2 * s))
        r_hi, c_hi = (r % (2 * s)) >= s, (c % (2 * s)) >= s
        masks.append(same & (r_hi & ~c_hi if d == 0 else ~r_hi & c_hi))
    return sel, masks


def _hgrn_body(nc_c, nc_l, q_c, q_l, ff_c, ff_l, fb_c, fb_l, v_c, v_l, gate_c, gate_l, lbrow, normw, bd,
               out_c, out_l, att_s, qd_s, kd_s, v_s, gl_s, o_s, st_s):
    bdm = bd[...]
    ii, jj = _iota2(CHUNK, CHUNK)
    incl_lower = (jj <= ii).astype(F32)
    pi, pj = _pair_iota()
    eye = pi == pj
    tables = [_gla_level_tables(d) for d in range(2)]
    dp = [(d, p) for d in range(2) for p in range(N_PAIRS)]
    n_tot = nc_c + nc_l

    def prep(q_ref, f_refs, v_ref, n_chunks, base):
        def prog(n):
            rin = _chunk_rows(n)
            rows = pl.ds(pl.multiple_of(base + n * CHUNK, CHUNK), CHUNK)
            zq = q_ref[rin, :]
            q = zq * _sigmoid(zq)
            v_s[rows, :] = v_ref[rin, :].astype(BF16)
            for d in range(2):
                zf = f_refs[d][rin, :]
                log1m_lb, one_m_lb = lbrow[2 * d:2 * d + 1, :], lbrow[2 * d + 1:2 * d + 2, :]
                e = jnp.exp(-jnp.abs(zf))
                r = 1.0 / (1.0 + e)
                sig, sig_neg = jnp.where(zf >= 0, r, e * r), jnp.where(zf >= 0, e * r, r)
                log_sig = jnp.minimum(zf, 0.0) - jnp.log(1.0 + e)
                lf = jnp.maximum(jnp.log((1.0 - one_m_lb) + one_m_lb * sig), log1m_lb + log_sig)
                k = one_m_lb * sig_neg
                prefix = _dot_exact_lhs(incl_lower, lf)
                diag = _dot_exact_rhs(q * k, bdm)
                yield
                total = jnp.sum(lf, axis=0, keepdims=True)
                g = prefix if d == 0 else total - prefix + lf
                sel, masks = tables[d]
                c_all = _dot_exact_lhs(sel, g)
                yield
                acc = [jnp.where(eye, t, 0.0) for t in _pairs(diag)]
                for x in range(len(GLA_LEVELS)):
                    c = c_all[x * CHUNK:(x + 1) * CHUNK, :]
                    qt = _pairs((q * jnp.exp(jnp.minimum(g - c, 0.0))).astype(BF16))
                    kt = _pairs((k * jnp.exp(jnp.minimum(c - g, 0.0))).astype(BF16))
                    lvl = [lax.dot_general(qt[p], _pair_blocks(kt[p]), (((1,), (1,)), ((), ())),
                                           preferred_element_type=F32) for p in range(N_PAIRS)]
                    acc = [a + jnp.where(masks[x], t, 0.0) for a, t in zip(acc, lvl)]
                    yield
                att_s[d, rows, :] = jnp.concatenate(acc, axis=1).astype(BF16)
                qd_s[d, rows, :] = (q * jnp.exp(g)).astype(BF16)
                kd_s[d, rows, :] = (k * jnp.exp(total - g)).astype(BF16)
                gl_s[d, pl.ds(base // CHUNK + n, 1), :] = jnp.exp(total)
        _chunk_loop(prog, n_chunks)

    prep(q_c, (ff_c, fb_c), v_c, nc_c, 0)
    prep(q_l, (ff_l, fb_l), v_l, nc_l, nc_c * CHUNK)

    st_s[...] = jnp.zeros_like(st_s)
    o_s[...] = jnp.zeros_like(o_s)
    block = _head_block_mask()

    def scan_body(step, carry):
        n_dir = (step, _backward_chunk(step, nc_c, n_tot))
        rows = [_chunk_rows(n) for n in n_dir]
        for x, (d, p) in enumerate(dp):
            cols = slice(p * LANE, (p + 1) * LANE)
            att, qd, kd = att_s[d, rows[d], cols], qd_s[d, rows[d], cols], kd_s[d, rows[d], cols]
            v = v_s[rows[d], cols]
            s_prev = st_s[x]
            o = (jnp.dot(att, _pair_blocks(v), preferred_element_type=F32)
                 + lax.dot_general(qd, s_prev.astype(BF16), (((1,), (1,)), ((), ())), preferred_element_type=F32))
            egl = gl_s[d, pl.ds(n_dir[d], 1), :][:, cols]
            st_s[x] = egl * s_prev + jnp.where(block, _bdot_tn(v, kd), 0.0)
            o_s[rows[d], cols] = o_s[rows[d], cols] + o
        return carry
    lax.fori_loop(0, n_tot, scan_body, 0)

    _head_norm_gate(o_s, gate_c, out_c, normw, bdm, nc_c, 0)
    _head_norm_gate(o_s, gate_l, out_l, normw, bdm, nc_l, nc_c * CHUNK)


def hgrn_mixer(zc, zl, lb, norm_w):
    b, tc, _ = zc.shape
    tl = zl.shape[1]
    nc_c, nc_l = tc // CHUNK, tl // CHUNK
    w = BRANCH_W
    lbrow = jnp.zeros((SUBLANE, w), F32)
    for d in range(2):
        lbrow = lbrow.at[2 * d].set(jnp.log1p(-lb[d])).at[2 * d + 1].set(1.0 - lb[d])
    in_specs, args = [], []
    for blk in HGRN_BLOCKS:
        in_specs += [_zspec(tc, w, blk), _zspec(tl, w, blk, pipeline_mode=pl.Buffered(1))]
        args += [zc, zl]
    out_specs, out_shape = _mixer_out(b, tc, tl)
    return pl.pallas_call(
        functools.partial(_hgrn_body, nc_c, nc_l),
        grid=(b,),
        in_specs=in_specs + [_cspec((SUBLANE, w)), _cspec((1, w)), _cspec((w, w))],
        out_specs=out_specs,
        out_shape=out_shape,
        scratch_shapes=[pltpu.VMEM((2, tc + tl, w), BF16)] * 3 + [
            pltpu.VMEM((tc + tl, w), BF16), pltpu.VMEM((2, nc_c + nc_l, w), F32), pltpu.VMEM((tc + tl, w), F32),
            pltpu.VMEM((2 * N_PAIRS, LANE, LANE), F32)],
        compiler_params=_MIXER_PARAMS,
        name="hgrn_mixer",
    )(*args, lbrow, norm_w[None, :], head_block_ones())


def _ssd_body(nc_c, nc_l, xbc_c, xbc_l, z_c, z_l, sm_c, sm_l, convw, convb, prow, dskip, normw,
              out_c, out_l, att_s, v_s, qd_s, kd_s, x_s, gl_s, o_s, st_s):
    w = BRANCH_W
    gw = SSD_GROUPS * SSD_STATE
    lane = lax.broadcasted_iota(jnp.int32, (CHUNK, LANE), 1)
    ii, jj = _iota2(CHUNK, CHUNK)
    incl_lower = (jj <= ii).astype(F32)
    masks = [_pair_order_masks(d) for d in range(2)]
    dp = [(d, p) for d in range(2) for p in range(N_PAIRS)]
    n_tot = nc_c + nc_l
    assert N_PAIRS == SSD_GROUPS

    def prep(x_ref, s_ref, n_chunks, base):
        def prog(n):
            rows = pl.ds(pl.multiple_of(base + n * CHUNK, CHUNK), CHUNK)
            y = _conv(x_ref, convw, n, n_chunks) + convb[...]
            y = y * _sigmoid(y)
            xs, bs, cs = y[:, :w], y[:, w:w + gw], y[:, w + gw:]
            x_s[rows, :] = xs
            sm = s_ref[_chunk_rows(n), :]
            dt_all = _softplus(sm + prow[1:2, :])
            in_dt = (lane >= DT_LANE0) & (lane < DT_LANE0 + 2 * N_HEADS)
            la = jnp.where(in_dt, -jnp.exp(prow[0:1, :]) * dt_all, 0.0)
            prefix = _dot_exact_lhs(incl_lower, la)
            total = jnp.sum(la, axis=0, keepdims=True)
            g_all = jnp.where(lane < DT_LANE0 + N_HEADS, prefix, total - prefix + la)
            g_t = g_all.T
            bg = [bs[:, g * SSD_STATE:(g + 1) * SSD_STATE] for g in range(SSD_GROUPS)]
            cg = [cs[:, g * SSD_STATE:(g + 1) * SSD_STATE] for g in range(SSD_GROUPS)]
            cb = [_bdot_nt(cg[g], jnp.concatenate([bg[g], bg[g]], axis=0)) for g in range(SSD_GROUPS)]
            yield
            for d, p in dp:
                l = DT_LANE0 + d * N_HEADS + 2 * p
                cols = slice(p * LANE, (p + 1) * LANE)
                g_col = _pair_cols(g_all, l)
                g_row = jnp.concatenate([g_t[l:l + 1, :], g_t[l + 1:l + 2, :]], axis=1)
                decay = jnp.exp(jnp.where(masks[d][0], g_col - g_row, NEG_BIG))
                att_s[d, rows, cols] = (cb[p] * decay).astype(BF16)
                v_s[d, rows, cols] = (xs[:, cols] * _pair_cols(dt_all, l)).astype(BF16)
                qd_s[d, rows, cols] = (jnp.concatenate([cg[p], cg[p]], axis=1) * jnp.exp(g_col)).astype(BF16)
                kd_s[d, rows, cols] = (jnp.concatenate([bg[p], bg[p]], axis=1)
                                       * jnp.exp(_pair_cols(total, l) - g_col)).astype(BF16)
            gl_s[pl.ds(base // CHUNK + n, 1), :] = jnp.exp(total)
            yield
        _chunk_loop(prog, n_chunks)

    prep(xbc_c, sm_c, nc_c, 0)
    prep(xbc_l, sm_l, nc_l, nc_c * CHUNK)

    st_s[...] = jnp.zeros_like(st_s)
    o_s[...] = jnp.zeros_like(o_s)
    block = _head_block_mask()
    first_rows = lax.broadcasted_iota(jnp.int32, (LANE, 1), 0) < SSD_STATE

    def scan_body(step, carry):
        n_dir = (step, _backward_chunk(step, nc_c, n_tot))
        rows = [_chunk_rows(n) for n in n_dir]
        egl_rows = [gl_s[pl.ds(n, 1), :] for n in n_dir]
        for x, (d, p) in enumerate(dp):
            cols = slice(p * LANE, (p + 1) * LANE)
            l = DT_LANE0 + d * N_HEADS + 2 * p
            att, v = att_s[d, rows[d], cols], v_s[d, rows[d], cols]
            qd, kd = qd_s[d, rows[d], cols], kd_s[d, rows[d], cols]
            s_prev = st_s[x]
            o = (jnp.dot(att, _pair_blocks(v), preferred_element_type=F32)
                 + jnp.dot(qd, s_prev.astype(BF16), preferred_element_type=F32))
            egl = jnp.where(first_rows, egl_rows[d][:, l:l + 1], egl_rows[d][:, l + 1:l + 2])
            st_s[x] = egl * s_prev + jnp.where(block, _bdot_tn(kd, v), 0.0)
            o_s[rows[d], cols] = o_s[rows[d], cols] + o
        return carry
    lax.fori_loop(0, n_tot, scan_body, 0)

    def finish(z_ref, out_ref, n_chunks, base):
        def body(n, carry):
            rows_in = pl.ds(pl.multiple_of(base + n * MXU_ROWS, MXU_ROWS), MXU_ROWS)
            rows = pl.ds(pl.multiple_of(n * MXU_ROWS, MXU_ROWS), MXU_ROWS)
            zt = z_ref[rows, :]
            y = (o_s[rows_in, :] + dskip[...] * x_s[rows_in, :]) * (zt * _sigmoid(zt))
            ms = jnp.sum(y * y, axis=1, keepdims=True) * (1.0 / w)
            out_ref[rows, :] = y * lax.rsqrt(ms + NORM_EPS) * normw[...]
            return carry
        lax.fori_loop(0, n_chunks * CHUNK // MXU_ROWS, body, 0)

    finish(z_c, out_c, nc_c, 0)
    finish(z_l, out_l, nc_l, nc_c * CHUNK)


def ssd_mixer(zc, zl, conv_w, conv_b, a_log, dt_bias, d_skip, norm_w):
    b, tc, _ = zc.shape
    tl = zl.shape[1]
    nc_c, nc_l = tc // CHUNK, tl // CHUNK
    w = BRANCH_W
    xbc_blk, z_blk, sm_blk = SSD_BLOCKS
    prow = jnp.zeros((SUBLANE, LANE), F32)
    prow = prow.at[0, DT_LANE0:DT_LANE0 + 2 * N_HEADS].set(a_log.reshape(-1))
    prow = prow.at[1, DT_LANE0:DT_LANE0 + 2 * N_HEADS].set(dt_bias.reshape(-1))
    out_specs, out_shape = _mixer_out(b, tc, tl)
    return pl.pallas_call(
        functools.partial(_ssd_body, nc_c, nc_l),
        grid=(b,),
        in_specs=[_zspec(tc, XBC_W, xbc_blk), _zspec(tl, XBC_W, xbc_blk, pipeline_mode=pl.Buffered(1)),
                  _zspec(tc, w, z_blk), _zspec(tl, w, z_blk),
                  _zspec(tc, LANE, sm_blk), _zspec(tl, LANE, sm_blk),
                  _cspec((SUBLANE, XBC_W)), _cspec((1, XBC_W)), _cspec((SUBLANE, LANE)), _cspec((1, w)), _cspec((1, w))],
        out_specs=out_specs,
        out_shape=out_shape,
        scratch_shapes=[pltpu.VMEM((2, tc + tl, w), BF16)] * 4 + [
            pltpu.VMEM((tc + tl, w), F32), pltpu.VMEM((nc_c + nc_l, LANE), F32), pltpu.VMEM((tc + tl, w), F32),
            pltpu.VMEM((2 * N_PAIRS, LANE, LANE), F32)],
        compiler_params=_MIXER_PARAMS,
        name="ssd_mixer",
    )(zc, zl, zc, zl, zc, zl,
      jnp.pad(conv_w, ((0, SUBLANE - CONV_K), (0, 0))), conv_b[None, :], prow,
      jnp.repeat(d_skip, HEAD_DIM)[None, :], norm_w[None, :])


def _fnet_body(t, u0_ref, u1_ref, ceh, cel, coh, col, seh, sel, soh, sol, gch_ref, gcl_ref, gsh_ref, gsl_ref,
               out_ref, p_s, q_s):
    d = lambda a, b: jnp.dot(a, b, preferred_element_type=F32)
    half = t // 2
    tile = min(MXU_ROWS, half)

    def channel_dft(n, carry):
        rows = pl.ds(pl.multiple_of(n * tile, tile), tile)
        for parity in range(2):
            src = pl.ds(2 * n * tile + parity, tile, stride=2)
            uh, ul = _split(jnp.concatenate([u0_ref[src, :], u1_ref[src, :]], axis=1), 2)
            p = d(uh, gch_ref[...]) + d(ul, gch_ref[...]) + d(uh, gcl_ref[...])
            q = d(uh, gsh_ref[...]) + d(ul, gsh_ref[...]) + d(uh, gsl_ref[...])
            p_s[parity, 0, rows, :], p_s[parity, 1, rows, :] = _split(p, 2)
            q_s[parity, 0, rows, :], q_s[parity, 1, rows, :] = _split(q, 2)
        return carry
    lax.fori_loop(0, half // tile, channel_dft, 0)

    def sequence_dft(n, carry):
        rows = pl.ds(pl.multiple_of(n * tile, tile), tile)
        parts = []
        for parity, (ch, cl, sh, sl) in enumerate(((ceh, cel, seh, sel), (coh, col, soh, sol))):
            ph, plo, qh, qlo = p_s[parity, 0], p_s[parity, 1], q_s[parity, 0], q_s[parity, 1]
            re = d(ch[rows, :], ph) + d(cl[rows, :], ph) + d(ch[rows, :], plo)
            im = d(sh[rows, :], qh) + d(sl[rows, :], qh) + d(sh[rows, :], qlo)
            parts.append(re - im)
        out_ref[rows, :] = parts[0] + parts[1]
        out_ref[pl.ds(pl.multiple_of(half + n * tile, tile), tile), :] = parts[0] - parts[1]
        return carry
    lax.fori_loop(0, half // tile, sequence_dft, 0)


def _hi_lo(tab):
    hi = tab.astype(BF16)
    return [hi, (tab - hi.astype(F32)).astype(BF16)]


def _dft_tables(n, scale):
    j = lax.broadcasted_iota(jnp.int32, (n, n), 0)
    k = lax.broadcasted_iota(jnp.int32, (n, n), 1)
    ang = ((j * k) % n).astype(F32) * (2.0 * math.pi / n)
    return _hi_lo(jnp.cos(ang) * scale) + _hi_lo(jnp.sin(ang) * scale)


def _half_dft_tables(t):
    j = lax.broadcasted_iota(jnp.int32, (t // 2, t // 2), 0)
    m = lax.broadcasted_iota(jnp.int32, (t // 2, t // 2), 1)
    out = {}
    for parity in range(2):
        ang = ((j * (2 * m + parity)) % t).astype(F32) * (2.0 * math.pi / t)
        out["c", parity], out["s", parity] = _hi_lo(jnp.cos(ang) * t ** -0.5), _hi_lo(jnp.sin(ang) * t ** -0.5)
    return out["c", 0] + out["c", 1] + out["s", 0] + out["s", 1]


def fnet_tables(t):
    grp = _dft_tables(FOURIER_GROUP_W, FOURIER_GROUP_W ** -0.5)
    n_grp = BRANCH_W // FOURIER_GROUP_W
    grp = [jnp.kron(jnp.eye(n_grp, dtype=F32), g.astype(F32)).astype(BF16) for g in grp]
    return _half_dft_tables(t) + grp


def fnet_mixer(z, tables):
    b, t, _ = z.shape
    w = BRANCH_W
    const = functools.partial(_cspec, pipeline_mode=pl.Buffered(1))
    first = FNET_BLOCK * w // LANE
    return pl.pallas_call(
        functools.partial(_fnet_body, t),
        grid=(b,),
        in_specs=[_zspec(t, LANE, first), _zspec(t, LANE, first + 1)] + [const((t // 2, t // 2))] * 8 + [const((w, w))] * 4,
        out_specs=pl.BlockSpec((None, t, w), lambda i: (i, 0, 0)),
        out_shape=jax.ShapeDtypeStruct((b, t, w), F32),
        scratch_shapes=[pltpu.VMEM((2, 2, t // 2, w), BF16)] * 2,
        compiler_params=_MIXER_PARAMS,
        name="fnet_mixer",
    )(z, z, *tables)


def _ln_rows(x):
    mu = jnp.mean(x, axis=-1, keepdims=True)
    xc = x - mu
    return xc * lax.rsqrt(jnp.mean(xc * xc, axis=-1, keepdims=True) + NORM_EPS)


def _row_spec(width):
    return pl.BlockSpec((MXU_ROWS, width), lambda i: (i, 0))


def _layer_spec(shape, l):
    return pl.BlockSpec((None,) + shape, lambda *_: (l,) + (0,) * len(shape), pipeline_mode=pl.Buffered(1))


def _mod_spec(l, j, row):
    return pl.BlockSpec((None, None, None, 1, D_MODEL), lambda *idx: (l, row(*idx), j, 0, 0))


def _stream_row(t, fixed_row):
    return (lambda i: fixed_row) if fixed_row is not None else (lambda i: i // (t // MXU_ROWS))


def _adaln_body(c_ref, w_ref, b_ref, o_ref):
    cc = c_ref[...]
    act = (cc * _sigmoid(cc)).astype(BF16)
    o_ref[...] = jnp.dot(act, w_ref[...].astype(BF16), preferred_element_type=F32) + b_ref[...]


def adaln_modulation(cond, ada_w, ada_b):
    r, d = cond.shape
    n_layers, _, n = ada_w.shape
    tn = n // 4
    return pl.pallas_call(
        _adaln_body,
        grid=(n_layers, n // tn),
        in_specs=[pl.BlockSpec((r, d), lambda l, j: (0, 0)), pl.BlockSpec((None, d, tn), lambda l, j: (l, 0, j)),
                  pl.BlockSpec((None, 1, tn), lambda l, j: (l, 0, j))],
        out_specs=pl.BlockSpec((None, r, tn), lambda l, j: (l, 0, j)),
        out_shape=jax.ShapeDtypeStruct((n_layers, r, n), F32),
        compiler_params=pltpu.CompilerParams(dimension_semantics=("parallel", "parallel"), vmem_limit_bytes=VMEM_LIMIT),
        name="adaln",
    )(cond, ada_w, ada_b[:, None, :])


def _modproj_body(x_ref, sh_ref, sc_ref, w_ref, z_ref, h_ref):
    h = (_ln_rows(x_ref[...]) * (1.0 + sc_ref[...]) + sh_ref[...]).astype(BF16)
    h_ref[...] = h
    z_ref[...] = jnp.dot(h, w_ref[...], preferred_element_type=F32)


def modulated_project(x, mod, w, l, t, fixed_row):
    m, d = x.shape
    n = w.shape[-1]
    row = _stream_row(t, fixed_row)
    return pl.pallas_call(
        _modproj_body,
        grid=(m // MXU_ROWS,),
        in_specs=[_row_spec(d), _mod_spec(l, 0, row), _mod_spec(l, 1, row), _layer_spec((d, n), l)],
        out_specs=[_row_spec(n), _row_spec(d)],
        out_shape=[jax.ShapeDtypeStruct((m, n), F32), jax.ShapeDtypeStruct((m, d), BF16)],
        compiler_params=_MIXER_PARAMS,
        name="modulated_project",
    )(x, mod, mod, w)


def _merge_body(h_ref, oa_ref, ob_ref, oc_ref, od_ref, x_ref, gate1_ref, sh2_ref, sc2_ref, lng_ref, lnb_ref,
                wg_ref, bg_ref, wb_ref, wo_ref, wr_ref, xo_ref, h2_ref, aff_ref):
    h = h_ref[...]
    acc = None
    for g, o_ref in enumerate((oa_ref, ob_ref, oc_ref, od_ref)):
        gate = _sigmoid(jnp.dot(h, wg_ref[g], preferred_element_type=F32) + bg_ref[g])
        t = gate * jnp.dot(o_ref[...].astype(BF16), wb_ref[g], preferred_element_type=F32)
        acc = t if acc is None else acc + t
    y = jnp.dot(acc.astype(BF16), wo_ref[...], preferred_element_type=F32)
    x_new = _ln_rows(DEEPNORM_ALPHA * x_ref[...] + gate1_ref[...] * y) * lng_ref[...] + lnb_ref[...]
    xo_ref[...] = x_new
    h2 = (_ln_rows(x_new) * (1.0 + sc2_ref[...]) + sh2_ref[...]).astype(BF16)
    h2_ref[...] = h2
    logits = jnp.dot(h2, wr_ref[...], preferred_element_type=F32)
    valid = lax.broadcasted_iota(jnp.int32, logits.shape, 1) < N_EXPERTS
    logits = jnp.where(valid, logits, NEG_BIG)
    e = jnp.exp(logits - jnp.max(logits, axis=-1, keepdims=True))
    aff_ref[...] = e / jnp.sum(e, axis=-1, keepdims=True)


def merge_and_norm(h, outs, x, mod, p, l, t, fixed_row):
    m, d = x.shape
    w = outs[0].shape[1]
    row = _stream_row(t, fixed_row)
    stacked = lambda name: _layer_spec(p[name].shape[1:], l)
    return pl.pallas_call(
        _merge_body,
        grid=(m // MXU_ROWS,),
        in_specs=[_row_spec(d)] + [_row_spec(w)] * N_BRANCH + [_row_spec(d)] + [_mod_spec(l, j, row) for j in (2, 3, 4)]
        + [stacked(name) for name in ("ln1_g", "ln1_b", "w_gate", "b_gate", "w_branch", "w_out", "w_router")],
        out_specs=[_row_spec(d), _row_spec(d), _row_spec(LANE)],
        out_shape=[jax.ShapeDtypeStruct((m, d), F32), jax.ShapeDtypeStruct((m, d), BF16),
                   jax.ShapeDtypeStruct((m, LANE), F32)],
        compiler_params=_MIXER_PARAMS,
        name="merge_and_norm",
    )(h, *outs, x, mod, mod, mod, p["ln1_g"], p["ln1_b"], p["w_gate"], p["b_gate"], p["w_branch"], p["w_out"], p["w_router"])


def _expert_body(x_ref, wt_ref, wg_ref, wu_ref, wd_ref, yh_ref, yl_ref, wg_s, wu_s, wd_s):
    @pl.when(pl.program_id(1) == 0)
    def _():
        wg_s[...] = wg_ref[...].astype(BF16)
        wu_s[...] = wu_ref[...].astype(BF16)
        wd_s[...] = wd_ref[...].astype(BF16)

    bb, cap, d = x_ref.shape
    x = x_ref[...].reshape(bb * cap, d)
    gate = jnp.dot(x, wg_s[...], preferred_element_type=F32)
    up = jnp.dot(x, wu_s[...], preferred_element_type=F32)
    hid = (gate * _sigmoid(gate) * up).astype(BF16)
    y = jnp.dot(hid, wd_s[...], preferred_element_type=F32) * wt_ref[...].reshape(bb * cap, 1)
    yh, yl = _split(y, 2)
    yh_ref[...] = yh.reshape(yh_ref.shape)
    yl_ref[...] = yl.reshape(yl_ref.shape)


def expert_swiglu(xe, weight, w_ff_gate, w_ff_up, w_ff_down, l):
    b, e, cap, d = xe.shape
    f = w_ff_gate.shape[-1]
    bb = min(b, max(1, MXU_ROWS // cap))
    x_spec = pl.BlockSpec((bb, None, cap, d), lambda ei, bi: (bi, ei, 0, 0))
    wt_spec = pl.BlockSpec((bb, None, cap, 1), lambda ei, bi: (bi, ei, 0, 0))
    w_spec = lambda shape: pl.BlockSpec((None, None) + shape, lambda ei, bi: (l, ei, 0, 0))
    return pl.pallas_call(
        _expert_body,
        grid=(e, b // bb),
        in_specs=[x_spec, wt_spec, w_spec((d, f)), w_spec((d, f)), w_spec((f, d))],
        out_specs=[x_spec, x_spec],
        out_shape=[jax.ShapeDtypeStruct(xe.shape, BF16)] * 2,
        scratch_shapes=[pltpu.VMEM((d, f), BF16), pltpu.VMEM((d, f), BF16), pltpu.VMEM((f, d), BF16)],
        compiler_params=pltpu.CompilerParams(dimension_semantics=("parallel", "arbitrary"),
                                             vmem_limit_bytes=VMEM_LIMIT),
        name="expert_swiglu",
    )(xe, weight[..., None], w_ff_gate, w_ff_up, w_ff_down)


def _combine_body(n_slots, idx_ref, yh_ref, yl_ref, x_ref, gate2_ref, lng_ref, lnb_ref, o_ref):
    tile = pl.program_id(1)
    token = tile * MXU_ROWS + lax.broadcasted_iota(jnp.int32, (MXU_ROWS, n_slots), 0)
    onehot = jnp.where(idx_ref[...] == token, 1.0, 0.0).astype(BF16)
    moe = (jnp.dot(onehot, yh_ref[...], preferred_element_type=F32)
           + jnp.dot(onehot, yl_ref[...], preferred_element_type=F32))
    o_ref[...] = _ln_rows(DEEPNORM_ALPHA * x_ref[...] + gate2_ref[...] * moe) * lng_ref[...] + lnb_ref[...]


def combine_and_norm(idx, yh, yl, x, mod, ln_g, ln_b, l, fixed_row):
    b, t, d = x.shape
    n_slots = idx.shape[-1]
    per_sample = lambda shape: pl.BlockSpec((None,) + shape, lambda i, j: (i, 0, 0))
    tile = pl.BlockSpec((None, MXU_ROWS, d), lambda i, j: (i, j, 0))
    vec = _layer_spec((1, d), l)
    row = (lambda i, j: fixed_row) if fixed_row is not None else (lambda i, j: i)
    return pl.pallas_call(
        functools.partial(_combine_body, n_slots),
        grid=(b, t // MXU_ROWS),
        in_specs=[per_sample((1, n_slots)), per_sample((n_slots, d)), per_sample((n_slots, d)), tile,
                  _mod_spec(l, 5, row), vec, vec],
        out_specs=tile,
        out_shape=jax.ShapeDtypeStruct((b, t, d), F32),
        compiler_params=pltpu.CompilerParams(dimension_semantics=("parallel", "arbitrary"),
                                             vmem_limit_bytes=VMEM_LIMIT),
        name="combine_and_norm",
    )(idx, yh, yl, x, mod, ln_g, ln_b)


def sincos_grid(rows, cols, dim):
    quarter = dim // 4
    omega = 1.0 / (10000.0 ** (jnp.arange(quarter, dtype=F32) / quarter))
    er = jnp.arange(rows, dtype=F32)[:, None] * omega
    ec = jnp.arange(cols, dtype=F32)[:, None] * omega
    er = jnp.concatenate([jnp.sin(er), jnp.cos(er)], axis=-1)
    ec = jnp.concatenate([jnp.sin(ec), jnp.cos(ec)], axis=-1)
    emb = jnp.concatenate([jnp.broadcast_to(er[:, None, :], (rows, cols, dim // 2)),
                           jnp.broadcast_to(ec[None, :, :], (rows, cols, dim // 2))], axis=-1)
    return emb.reshape(rows * cols, dim)


def hgrn_lower_bounds(logits):
    cum = jnp.cumsum(jax.nn.softmax(logits.astype(F32), axis=1), axis=1)
    return cum - cum[:, :1]


def _flat(a):
    return a.reshape(-1, a.shape[-1])


def token_mixer(xc, xl, mod, p, fnet_tabs, b, l, with_ctx):
    tc, tl = xc.shape[0] // b, xl.shape[0] // b
    zc, hc = modulated_project(xc, mod, p["w_in"], l, tc, b)
    zl, hl = modulated_project(xl, mod, p["w_in"], l, tl, None)
    zc, zl = zc.reshape(b, tc, Z_COLS), zl.reshape(b, tl, Z_COLS)
    a_c, a_l = hgrn_mixer(zc, zl, p["lb"][:, l], p["hgrn_norm_w"][l])
    b_c, b_l = gdn_mixer(zc, zl, p["gdn_conv_w"][l], p["gdn_a_log"][l], p["gdn_dt_bias"][l], p["gdn_norm_w"][l])
    d_c, d_l = ssd_mixer(zc, zl, p["ssd_conv_w"][l], p["ssd_conv_b"][l], p["ssd_a_log"][l], p["ssd_dt_bias"][l],
                         p["ssd_d"][l], p["ssd_norm_w"][l])

    def merged(h, outs, x, t, fixed_row):
        return merge_and_norm(h, tuple(_flat(o) for o in outs), x, mod, p, l, t, fixed_row)

    out_l = merged(hl, (a_l, b_l, fnet_mixer(zl, fnet_tabs[1]), d_l), xl, tl, None)
    out_c = merged(hc, (a_c, b_c, fnet_mixer(zc, fnet_tabs[0]), d_c), xc, tc, b) if with_ctx else None
    return out_c, out_l


def channel_mixer(x_mid, h, aff, b, mod, p, l, fixed_row):
    t_ = h.shape[0] // b
    d = h.shape[-1]
    h = h.reshape(b, t_, d)
    cap = EC_CAPACITY_FACTOR * t_ // N_EXPERTS
    aff = aff[:, :N_EXPERTS].reshape(b, t_, N_EXPERTS)
    weight, idx = lax.top_k(jnp.swapaxes(aff, 1, 2), cap)
    xe = h[jnp.arange(b)[:, None, None], idx]
    yh, yl = expert_swiglu(xe, weight, p["w_ff_gate"], p["w_ff_up"], p["w_ff_down"], l)
    n_slots = N_EXPERTS * cap
    x_new = combine_and_norm(idx.reshape(b, 1, n_slots), yh.reshape(b, n_slots, d), yl.reshape(b, n_slots, d),
                             x_mid.reshape(b, t_, d), mod, p["ln2_g"], p["ln2_b"], l, fixed_row)
    return _flat(x_new)


def kernel(x, c, ctx, c_ctx, ada_w, ada_b, w_in, hgrn_lb_logits, hgrn_norm_w,
           gdn_conv_w, gdn_a_log, gdn_dt_bias, gdn_norm_w,
           ssd_conv_w, ssd_conv_b, ssd_a_log, ssd_dt_bias, ssd_d, ssd_norm_w,
           w_gate, b_gate, w_branch, w_out, ln1_g, ln1_b,
           w_router, w_ff_gate, w_ff_up, w_ff_down, ln2_g, ln2_b):
    b, n_lat, d = x.shape
    n_ctx = ctx.shape[1]
    rows = n_lat // GRID_W
    xl = _flat(x + sincos_grid(rows, GRID_W, D_MODEL).astype(x.dtype))
    xc = _flat(ctx)
    fnet_tabs = (fnet_tables(n_ctx), fnet_tables(n_lat))
    cond = jnp.concatenate([c, c_ctx[None, :], jnp.zeros((-(b + 1) % SUBLANE, d), c.dtype)], axis=0)
    mod = adaln_modulation(cond, ada_w, ada_b).reshape(DEPTH, cond.shape[0], 6, 1, d)
    p = {
        "w_in": permute_w_in(w_in).astype(BF16), "lb": hgrn_lower_bounds(hgrn_lb_logits), "hgrn_norm_w": hgrn_norm_w,
        "gdn_conv_w": gdn_conv_w, "gdn_a_log": gdn_a_log, "gdn_dt_bias": gdn_dt_bias, "gdn_norm_w": gdn_norm_w,
        "ssd_conv_w": ssd_conv_w, "ssd_conv_b": ssd_conv_b, "ssd_a_log": ssd_a_log, "ssd_dt_bias": ssd_dt_bias,
        "ssd_d": ssd_d, "ssd_norm_w": ssd_norm_w,
        "w_gate": w_gate.astype(BF16), "b_gate": b_gate[:, :, None, :], "w_branch": w_branch.astype(BF16),
        "w_out": w_out.astype(BF16), "ln1_g": ln1_g[:, None, :], "ln1_b": ln1_b[:, None, :],
        "w_router": jnp.pad(w_router, ((0, 0), (0, 0), (0, LANE - N_EXPERTS))).astype(BF16),
        "w_ff_gate": w_ff_gate, "w_ff_up": w_ff_up, "w_ff_down": w_ff_down,
        "ln2_g": ln2_g[:, None, :], "ln2_b": ln2_b[:, None, :],
    }
    for l in range(DEPTH):
        with_ctx = l < DEPTH - 1
        out_c, out_l = token_mixer(xc, xl, mod, p, fnet_tabs, b, l, with_ctx)
        xl = channel_mixer(*out_l, b, mod, p, l, None)
        if with_ctx:
            xc = channel_mixer(*out_c, b, mod, p, l, b)
    return xl.reshape(b, n_lat, d)
```

```python
import functools
import math

import jax
import jax.numpy as jnp
import numpy as np
from jax import lax
from jax.experimental import pallas as pl
from jax.experimental.pallas import tpu as pltpu

D_MODEL = 1024
DEPTH = 4
GRID_W = 64
N_BRANCH = 4
BRANCH_W = D_MODEL // N_BRANCH
HEAD_DIM = 64
N_HEADS = BRANCH_W // HEAD_DIM
N_PAIRS = N_HEADS // 2
FOURIER_GROUP_W = HEAD_DIM
CHUNK = 64
CONV_K = 5
SSD_STATE = 64
SSD_GROUPS = 2
N_EXPERTS = 16
EC_CAPACITY_FACTOR = 2
DEEPNORM_ALPHA = (2.0 * DEPTH) ** 0.25
NORM_EPS = 1e-6

LANE = 128
SUBLANE = 8
MXU_ROWS = 256
VMEM_LIMIT = 48 * 1024 * 1024
NEG_BIG = -1e30

F32 = jnp.float32
BF16 = jnp.bfloat16

XBC_W = BRANCH_W + 2 * SSD_GROUPS * SSD_STATE
Z_ORDER = ("b_qkv", "a_q", "a_f_fwd", "a_f_bwd", "a_v", "a_g", "b_g", "c_u", "d_xbc", "d_z", "b_a", "b_beta", "d_dt")
REF_SPLITS = (
    ("a_q", BRANCH_W), ("a_f_fwd", BRANCH_W), ("a_f_bwd", BRANCH_W), ("a_v", BRANCH_W), ("a_g", BRANCH_W),
    ("b_qkv", 3 * BRANCH_W), ("b_g", BRANCH_W), ("b_a", 2 * N_HEADS), ("b_beta", 2 * N_HEADS),
    ("c_u", BRANCH_W), ("d_xbc", XBC_W), ("d_z", BRANCH_W), ("d_dt", 2 * N_HEADS),
)
Z_COLS = 3 * BRANCH_W + 5 * BRANCH_W + 2 * BRANCH_W + XBC_W + BRANCH_W + LANE
GDN_BLOCKS = (0, 8, 26)
HGRN_BLOCKS = (3, 4, 5, 6, 7)
FNET_BLOCK = 9
SSD_BLOCKS = (5, 12, 26)
DT_LANE0 = 4 * N_HEADS


def permute_w_in(w_in):
    start, spans = 0, {}
    for name, size in REF_SPLITS:
        spans[name] = (start, start + size)
        start += size
    parts = [w_in[..., spans[n][0]:spans[n][1]] for n in Z_ORDER]
    used = sum(p.shape[-1] for p in parts)
    return jnp.concatenate(parts + [jnp.zeros(w_in.shape[:-1] + (Z_COLS - used,), w_in.dtype)], axis=-1)


def _bdot(a, b):
    return jnp.dot(a.astype(BF16), b.astype(BF16), preferred_element_type=F32)


def _bdot_nt(a, b):
    return lax.dot_general(a.astype(BF16), b.astype(BF16), (((1,), (1,)), ((), ())), preferred_element_type=F32)


def _bdot_tn(a, b):
    return lax.dot_general(a.astype(BF16), b.astype(BF16), (((0,), (0,)), ((), ())), preferred_element_type=F32)


def _split(x, terms):
    out = []
    for _ in range(terms):
        p = x.astype(BF16)
        out.append(p)
        x = x - p.astype(F32)
    return out


def _dot_exact_lhs(m, x, terms=3):
    mb = m.astype(BF16)
    return sum(jnp.dot(mb, p, preferred_element_type=F32) for p in _split(x, terms))


def _dot_exact_rhs(x, m, terms=2):
    mb = m.astype(BF16)
    return sum(jnp.dot(p, mb, preferred_element_type=F32) for p in _split(x, terms))


def _iota2(n, m):
    return lax.broadcasted_iota(jnp.int32, (n, m), 0), lax.broadcasted_iota(jnp.int32, (n, m), 1)


def _sigmoid(x):
    return 1.0 / (1.0 + jnp.exp(-x))


def _softplus(x):
    return jnp.maximum(x, 0.0) + jnp.log(1.0 + jnp.exp(-jnp.abs(x)))


def _conv(x_ref, w_ref, n, n_chunks):
    t = n_chunks * CHUNK
    start = pl.multiple_of(n * CHUNK, CHUNK)
    cur = x_ref[pl.ds(start, CHUNK), :]
    prev_start = pl.multiple_of(jnp.maximum(start - SUBLANE, 0), SUBLANE)
    next_start = pl.multiple_of(jnp.minimum(start + CHUNK, t - SUBLANE), SUBLANE)
    prev = x_ref[pl.ds(prev_start, SUBLANE), :] * jnp.where(n > 0, 1.0, 0.0)
    nxt = x_ref[pl.ds(next_start, SUBLANE), :] * jnp.where(n < n_chunks - 1, 1.0, 0.0)
    ext = jnp.concatenate([prev, cur, nxt], axis=0)
    pad = CONV_K // 2
    acc = None
    n_ext = CHUNK + 2 * SUBLANE
    for k in range(CONV_K):
        shifted = ext if k == pad else pltpu.roll(ext, (pad - k) % n_ext, 0)
        term = shifted[SUBLANE:SUBLANE + CHUNK, :] * w_ref[k:k + 1, :]
        acc = term if acc is None else acc + term
    return acc


PREP_UNROLL = 4
SCAN_UNROLL = 6


def _interleave(progs):
    live = list(progs)
    while live:
        nxt = []
        for p in live:
            try:
                next(p)
                nxt.append(p)
            except StopIteration:
                pass
        live = nxt


def _chunk_loop(prog, n_chunks):
    def body(i, carry):
        _interleave([prog(i * PREP_UNROLL + j) for j in range(PREP_UNROLL)])
        return carry
    lax.fori_loop(0, n_chunks // PREP_UNROLL, body, 0)


def _backward_chunk(step, nc_c, n_tot):
    return jnp.where(step < nc_c, nc_c - 1 - step, n_tot - 1 - (step - nc_c))


def _chunk_rows(n):
    return pl.ds(pl.multiple_of(n * CHUNK, CHUNK), CHUNK)


def _head_norm_gate(o_s, gate_ref, out_ref, normw, bdm, n_chunks, base):
    def body(n, carry):
        rows = pl.ds(pl.multiple_of(n * MXU_ROWS, MXU_ROWS), MXU_ROWS)
        o = o_s[pl.ds(pl.multiple_of(base + n * MXU_ROWS, MXU_ROWS), MXU_ROWS), :]
        ms = _dot_exact_rhs(o * o, bdm) * (1.0 / HEAD_DIM)
        gt = gate_ref[rows, :]
        out_ref[rows, :] = o * lax.rsqrt(ms + NORM_EPS) * normw[...] * (gt * _sigmoid(gt))
        return carry
    lax.fori_loop(0, n_chunks * CHUNK // MXU_ROWS, body, 0)


def head_block_ones():
    i = np.arange(BRANCH_W)
    return jnp.asarray((i[:, None] // HEAD_DIM) == (i[None, :] // HEAD_DIM), BF16)


def _zspec(t, width, blk, **kw):
    return pl.BlockSpec((None, t, width), lambda i: (i, 0, blk), **kw)


def _cspec(shape, **kw):
    return pl.BlockSpec(shape, lambda *_: (0,) * len(shape), **kw)


def _mixer_out(b, tc, tl):
    assert tc % MXU_ROWS == 0 and tl % MXU_ROWS == 0 and MXU_ROWS % (PREP_UNROLL * CHUNK) == 0
    specs = [pl.BlockSpec((None, tc, BRANCH_W), lambda i: (i, 0, 0)), pl.BlockSpec((None, tl, BRANCH_W), lambda i: (i, 0, 0))]
    shapes = [jax.ShapeDtypeStruct((b, tc, BRANCH_W), F32), jax.ShapeDtypeStruct((b, tl, BRANCH_W), F32)]
    return specs, shapes


_MIXER_PARAMS = pltpu.CompilerParams(dimension_semantics=("parallel",), vmem_limit_bytes=VMEM_LIMIT)


def _pair_blocks(x):
    first = (lax.broadcasted_iota(jnp.int32, x.shape, 1) % LANE) < HEAD_DIM
    zero = jnp.zeros_like(x)
    return jnp.concatenate([jnp.where(first, x, zero), jnp.where(first, zero, x)], axis=0)


def _pair_cols(x, l0):
    lane = lax.broadcasted_iota(jnp.int32, (x.shape[0], LANE), 1)
    return jnp.where(lane < HEAD_DIM, x[:, l0:l0 + 1], x[:, l0 + 1:l0 + 2])


def _pair_iota():
    i, j = _iota2(CHUNK, LANE)
    return i, j % HEAD_DIM


def _pair_order_masks(d):
    i, j = _pair_iota()
    return ((j <= i), (j < i)) if d == 0 else ((j >= i), (j > i))


def _pair_dot(x, y):
    return jnp.dot(x.astype(BF16), _pair_blocks(y.astype(BF16)), preferred_element_type=F32)


def _pair_dot_x3(x, y):
    xh, xl = _split(x, 2)
    yh, yl = (_pair_blocks(t) for t in _split(y, 2))
    d = lambda p, q: jnp.dot(p, q, preferred_element_type=F32)
    return d(xh, yh) + d(xl, yh) + d(xh, yl)


def _head_block_mask():
    r, c = _iota2(LANE, LANE)
    return (r < HEAD_DIM) == (c < HEAD_DIM)


def _pairs(x):
    return [x[:, p * LANE:(p + 1) * LANE] for p in range(N_PAIRS)]


def _unit_lower_inverses(mats):
    i, j = _pair_iota()
    eye = (i == j).astype(F32)
    same4 = (i // 4) == (j // 4)
    d4 = [jnp.where(same4, a, 0.0) for a in mats]
    sq = [_pair_dot(d, d) for d in d4]
    xs = [eye - d for d in d4]
    yield
    xs = [x + _pair_dot(x, q) for x, q in zip(xs, sq)]
    yield
    s = 4
    while s < CHUNK:
        sel = ((i // (2 * s)) == (j // (2 * s))) & ((i // s) != (j // s))
        ox = [_pair_dot(jnp.where(sel, a, 0.0), x) for a, x in zip(mats, xs)]
        yield
        xs = [x - _pair_dot(x, y) for x, y in zip(xs, ox)]
        yield
        s *= 2
    return xs


def _gdn_body(nc_c, nc_l, qkv_c, qkv_l, gate_c, gate_l, sm_c, sm_l, convw, prow, normw, bd,
              out_c, out_l, u_s, w_s, qk_s, qd_s, kd_s, gl_s, o_s, st_s):
    w = BRANCH_W
    bdm = bd[...]
    lane = lax.broadcasted_iota(jnp.int32, (CHUNK, LANE), 1)
    ii, jj = _iota2(CHUNK, CHUNK)
    incl_lower = (jj <= ii).astype(F32)
    masks = [_pair_order_masks(d) for d in range(2)]
    dp = [(d, p) for d in range(2) for p in range(N_PAIRS)]
    n_tot = nc_c + nc_l

    def prep(x_ref, s_ref, n_chunks, base):
        def prog(n):
            y = _conv(x_ref, convw, n, n_chunks)
            y = y * _sigmoid(y)
            q, k, v = y[:, :w], y[:, w:2 * w], y[:, 2 * w:]
            qss, kss = _dot_exact_rhs(q * q, bdm), _dot_exact_rhs(k * k, bdm)
            yield
            q = q * lax.rsqrt(qss + NORM_EPS) * HEAD_DIM ** -0.5
            k = k * lax.rsqrt(kss + NORM_EPS)
            sm = s_ref[_chunk_rows(n), :]
            la = jnp.where(lane < 2 * N_HEADS, -jnp.exp(prow[0:1, :]) * _softplus(sm + prow[1:2, :]), 0.0)
            beta_all = _sigmoid(sm)
            prefix = _dot_exact_lhs(incl_lower, la)
            total = jnp.sum(la, axis=0, keepdims=True)
            g_all = jnp.where(lane < N_HEADS, prefix, total - prefix + la)
            g_t = g_all.T
            qp, kp, vp = _pairs(q), _pairs(k), _pairs(v)
            qkk = [lax.dot_general(jnp.concatenate([qp[p], kp[p]], axis=0).astype(BF16), _pair_blocks(kp[p].astype(BF16)),
                                   (((1,), (1,)), ((), ())), preferred_element_type=F32) for p in range(N_PAIRS)]
            yield
            l0 = [d * N_HEADS + 2 * p for d, p in dp]
            g_col = [_pair_cols(g_all, l) for l in l0]
            g_row = [jnp.concatenate([g_t[l:l + 1, :], g_t[l + 1:l + 2, :]], axis=1) for l in l0]
            beta = [_pair_cols(beta_all, 2 * N_HEADS + l) for l in l0]
            gl = [_pair_cols(total, l) for l in l0]
            decay = [jnp.exp(jnp.where(masks[d][0], g_col[x] - g_row[x], NEG_BIG)) for x, (d, p) in enumerate(dp)]
            qk = [qkk[p][:CHUNK] * decay[x] for x, (d, p) in enumerate(dp)]
            a = [jnp.where(masks[d][1], beta[x] * qkk[p][CHUNK:] * decay[x], 0.0) for x, (d, p) in enumerate(dp)]
            tinv = yield from _unit_lower_inverses(a)
            eg = [jnp.exp(g) for g in g_col]
            rhs = [jnp.concatenate([beta[x] * vp[p], beta[x] * kp[p] * eg[x]], axis=1) for x, (d, p) in enumerate(dp)]
            uw = [_pair_dot(t, r) for t, r in zip(tinv, rhs)]
            yield
            resid = [r - y0 - _pair_dot_x3(m, y0) for r, y0, m in zip(rhs, uw, a)]
            yield
            uw = [y0 + _pair_dot(t, r) for y0, t, r in zip(uw, tinv, resid)]
            yield
            rows = pl.ds(pl.multiple_of(base + n * CHUNK, CHUNK), CHUNK)
            for x, (d, p) in enumerate(dp):
                cols = slice(p * LANE, (p + 1) * LANE)
                u_s[d, rows, cols] = uw[x][:, :LANE]
                w_s[d, rows, cols] = uw[x][:, LANE:].astype(BF16)
                qk_s[d, rows, cols] = qk[x].astype(BF16)
                qd_s[d, rows, cols] = (qp[p] * eg[x]).astype(BF16)
                kd_s[d, rows, cols] = (kp[p] * jnp.exp(gl[x] - g_col[x])).astype(BF16)
            gl_s[pl.ds(base // CHUNK + n, 1), :] = jnp.exp(total)
            yield
        _chunk_loop(prog, n_chunks)

    prep(qkv_c, sm_c, nc_c, 0)
    prep(qkv_l, sm_l, nc_l, nc_c * CHUNK)

    st_s[...] = jnp.zeros_like(st_s)
    o_s[...] = jnp.zeros_like(o_s)
    block = _head_block_mask()
    first_rows = lax.broadcasted_iota(jnp.int32, (LANE, 1), 0) < HEAD_DIM

    def scan_body(step, carry):
        n_dir = (step, _backward_chunk(step, nc_c, n_tot))
        rows = [_chunk_rows(n) for n in n_dir]
        egl_rows = [gl_s[pl.ds(n, 1), :] for n in n_dir]
        tiles = lambda ref, x: ref[dp[x][0], rows[dp[x][0]], dp[x][1] * LANE:(dp[x][1] + 1) * LANE]
        s_prev = [st_s[x] for x in range(len(dp))]
        v_new = [tiles(u_s, x) - jnp.dot(tiles(w_s, x), s_prev[x].astype(BF16), preferred_element_type=F32)
                 for x in range(len(dp))]
        o = [jnp.dot(tiles(qd_s, x), s_prev[x].astype(BF16), preferred_element_type=F32)
             + jnp.dot(tiles(qk_s, x), _pair_blocks(v_new[x].astype(BF16)), preferred_element_type=F32)
             for x in range(len(dp))]
        for x, (d, p) in enumerate(dp):
            l = d * N_HEADS + 2 * p
            egl = jnp.where(first_rows, egl_rows[d][:, l:l + 1], egl_rows[d][:, l + 1:l + 2])
            st_s[x] = egl * s_prev[x] + jnp.where(block, _bdot_tn(tiles(kd_s, x), v_new[x]), 0.0)
            cols = slice(p * LANE, (p + 1) * LANE)
            o_s[rows[d], cols] = o_s[rows[d], cols] + o[x]
        return carry
    lax.fori_loop(0, n_tot, scan_body, 0)

    _head_norm_gate(o_s, gate_c, out_c, normw, bdm, nc_c, 0)
    _head_norm_gate(o_s, gate_l, out_l, normw, bdm, nc_l, nc_c * CHUNK)


def gdn_mixer(zc, zl, conv_w, a_log, dt_bias, norm_w):
    b, tc, _ = zc.shape
    tl = zl.shape[1]
    nc_c, nc_l = tc // CHUNK, tl // CHUNK
    w = BRANCH_W
    qkv_blk, gate_blk, sm_blk = GDN_BLOCKS
    prow = jnp.zeros((SUBLANE, LANE), F32)
    prow = prow.at[0, :2 * N_HEADS].set(a_log.reshape(-1)).at[1, :2 * N_HEADS].set(dt_bias.reshape(-1))
    out_specs, out_shape = _mixer_out(b, tc, tl)
    return pl.pallas_call(
        functools.partial(_gdn_body, nc_c, nc_l),
        grid=(b,),
        in_specs=[_zspec(tc, 3 * w, qkv_blk), _zspec(tl, 3 * w, qkv_blk, pipeline_mode=pl.Buffered(1)),
                  _zspec(tc, w, gate_blk), _zspec(tl, w, gate_blk),
                  _zspec(tc, LANE, sm_blk), _zspec(tl, LANE, sm_blk),
                  _cspec((SUBLANE, 3 * w)), _cspec((SUBLANE, LANE)), _cspec((1, w)), _cspec((w, w))],
        out_specs=out_specs,
        out_shape=out_shape,
        scratch_shapes=[pltpu.VMEM((2, tc + tl, w), F32)] + [pltpu.VMEM((2, tc + tl, w), BF16)] * 4 + [
            pltpu.VMEM((nc_c + nc_l, LANE), F32), pltpu.VMEM((tc + tl, w), F32),
            pltpu.VMEM((2 * N_PAIRS, LANE, LANE), F32)],
        compiler_params=_MIXER_PARAMS,
        name="gdn_mixer",
    )(zc, zl, zc, zl, zc, zl,
      jnp.pad(conv_w, ((0, SUBLANE - CONV_K), (0, 0))), prow, norm_w[None, :], head_block_ones())


GLA_LEVELS = (32, 16, 8, 4, 2, 1)


def _gla_level_tables(d):
    n_lv = len(GLA_LEVELS)
    i, t = _iota2(n_lv * CHUNK, CHUNK)
    sel = jnp.zeros((n_lv * CHUNK, CHUNK), F32)
    r, c = _pair_iota()
    masks = []
    for x, s in enumerate(GLA_LEVELS):
        row = i - x * CHUNK
        bound = 2 * s * (row // (2 * s)) + s - 1 + d
        sel = jnp.where((i // CHUNK == x) & (t == bound), 1.0, sel)
        same = (r // (2 * s)) == (c // (2 * s))
        r_hi, c_hi = (r % (2 * s)) >= s, (c % (2 * s)) >= s
        masks.append(same & (r_hi & ~c_hi if d == 0 else ~r_hi & c_hi))
    return sel, masks


def _hgrn_body(nc_c, nc_l, q_c, q_l, ff_c, ff_l, fb_c, fb_l, v_c, v_l, gate_c, gate_l, lbrow, normw, bd,
               out_c, out_l, att_s, qd_s, kd_s, v_s, gl_s, o_s, st_s):
    bdm = bd[...]
    ii, jj = _iota2(CHUNK, CHUNK)
    incl_lower = (jj <= ii).astype(F32)
    pi, pj = _pair_iota()
    eye = pi == pj
    tables = [_gla_level_tables(d) for d in range(2)]
    dp = [(d, p) for d in range(2) for p in range(N_PAIRS)]
    n_tot = nc_c + nc_l

    def prep(q_ref, f_refs, v_ref, n_chunks, base):
        def prog(n):
            rin = _chunk_rows(n)
            rows = pl.ds(pl.multiple_of(base + n * CHUNK, CHUNK), CHUNK)
            zq = q_ref[rin, :]
            q = zq * _sigmoid(zq)
            v_s[rows, :] = v_ref[rin, :].astype(BF16)
            for d in range(2):
                zf = f_refs[d][rin, :]
                log1m_lb, one_m_lb = lbrow[2 * d:2 * d + 1, :], lbrow[2 * d + 1:2 * d + 2, :]
                e = jnp.exp(-jnp.abs(zf))
                r = 1.0 / (1.0 + e)
                sig, sig_neg = jnp.where(zf >= 0, r, e * r), jnp.where(zf >= 0, e * r, r)
                log_sig = jnp.minimum(zf, 0.0) - jnp.log(1.0 + e)
                lf = jnp.maximum(jnp.log((1.0 - one_m_lb) + one_m_lb * sig), log1m_lb + log_sig)
                k = one_m_lb * sig_neg
                prefix = _dot_exact_lhs(incl_lower, lf)
                diag = _dot_exact_rhs(q * k, bdm)
                yield
                total = jnp.sum(lf, axis=0, keepdims=True)
                g = prefix if d == 0 else total - prefix + lf
                sel, masks = tables[d]
                c_all = _dot_exact_lhs(sel, g)
                yield
                acc = [jnp.where(eye, t, 0.0) for t in _pairs(diag)]
                for x in range(len(GLA_LEVELS)):
                    c = c_all[x * CHUNK:(x + 1) * CHUNK, :]
                    qt = _pairs((q * jnp.exp(jnp.minimum(g - c, 0.0))).astype(BF16))
                    kt = _pairs((k * jnp.exp(jnp.minimum(c - g, 0.0))).astype(BF16))
                    lvl = [lax.dot_general(qt[p], _pair_blocks(kt[p]), (((1,), (1,)), ((), ())),
                                           preferred_element_type=F32) for p in range(N_PAIRS)]
                    acc = [a + jnp.where(masks[x], t, 0.0) for a, t in zip(acc, lvl)]
                    yield
                att_s[d, rows, :] = jnp.concatenate(acc, axis=1).astype(BF16)
                qd_s[d, rows, :] = (q * jnp.exp(g)).astype(BF16)
                kd_s[d, rows, :] = (k * jnp.exp(total - g)).astype(BF16)
                gl_s[d, pl.ds(base // CHUNK + n, 1), :] = jnp.exp(total)
        _chunk_loop(prog, n_chunks)

    prep(q_c, (ff_c, fb_c), v_c, nc_c, 0)
    prep(q_l, (ff_l, fb_l), v_l, nc_l, nc_c * CHUNK)

    st_s[...] = jnp.zeros_like(st_s)
    o_s[...] = jnp.zeros_like(o_s)
    block = _head_block_mask()

    def scan_body(trip, carry):
        state = [st_s[x] for x in range(len(dp))]
        steps = [trip * SCAN_UNROLL + u for u in range(SCAN_UNROLL)]
        n_dirs = [(step, _backward_chunk(step, nc_c, n_tot)) for step in steps]
        pre = []
        for n_dir in n_dirs:
            rows = [_chunk_rows(n) for n in n_dir]
            per_chain = []
            for x, (d, p) in enumerate(dp):
                cols = slice(p * LANE, (p + 1) * LANE)
                att, qd, kd = att_s[d, rows[d], cols], qd_s[d, rows[d], cols], kd_s[d, rows[d], cols]
                v = v_s[rows[d], cols]
                intra = jnp.dot(att, _pair_blocks(v), preferred_element_type=F32)
                delta = jnp.where(block, _bdot_tn(v, kd), 0.0)
                egl = gl_s[d, pl.ds(n_dir[d], 1), :][:, cols]
                per_chain.append((rows[d], cols, qd, intra, delta, egl))
            pre.append(per_chain)
        for per_chain in pre:
            for x, (rows_d, cols, qd, intra, delta, egl) in enumerate(per_chain):
                o = intra + lax.dot_general(qd, state[x].astype(BF16), (((1,), (1,)), ((), ())),
                                            preferred_element_type=F32)
                state[x] = egl * state[x] + delta
                o_s[rows_d, cols] = o_s[rows_d, cols] + o
        for x in range(len(dp)):
            st_s[x] = state[x]
        return carry
    assert n_tot % SCAN_UNROLL == 0
    lax.fori_loop(0, n_tot // SCAN_UNROLL, scan_body, 0)

    _head_norm_gate(o_s, gate_c, out_c, normw, bdm, nc_c, 0)
    _head_norm_gate(o_s, gate_l, out_l, normw, bdm, nc_l, nc_c * CHUNK)


def hgrn_mixer(zc, zl, lb, norm_w):
    b, tc, _ = zc.shape
    tl = zl.shape[1]
    nc_c, nc_l = tc // CHUNK, tl // CHUNK
    w = BRANCH_W
    lbrow = jnp.zeros((SUBLANE, w), F32)
    for d in range(2):
        lbrow = lbrow.at[2 * d].set(jnp.log1p(-lb[d])).at[2 * d + 1].set(1.0 - lb[d])
    in_specs, args = [], []
    for blk in HGRN_BLOCKS:
        in_specs += [_zspec(tc, w, blk), _zspec(tl, w, blk, pipeline_mode=pl.Buffered(1))]
        args += [zc, zl]
    out_specs, out_shape = _mixer_out(b, tc, tl)
    return pl.pallas_call(
        functools.partial(_hgrn_body, nc_c, nc_l),
        grid=(b,),
        in_specs=in_specs + [_cspec((SUBLANE, w)), _cspec((1, w)), _cspec((w, w))],
        out_specs=out_specs,
        out_shape=out_shape,
        scratch_shapes=[pltpu.VMEM((2, tc + tl, w), BF16)] * 3 + [
            pltpu.VMEM((tc + tl, w), BF16), pltpu.VMEM((2, nc_c + nc_l, w), F32), pltpu.VMEM((tc + tl, w), F32),
            pltpu.VMEM((2 * N_PAIRS, LANE, LANE), F32)],
        compiler_params=_MIXER_PARAMS,
        name="hgrn_mixer",
    )(*args, lbrow, norm_w[None, :], head_block_ones())


def _ssd_body(nc_c, nc_l, xbc_c, xbc_l, z_c, z_l, sm_c, sm_l, convw, convb, prow, dskip, normw,
              out_c, out_l, att_s, v_s, qd_s, kd_s, x_s, gl_s, o_s, st_s):
    w = BRANCH_W
    gw = SSD_GROUPS * SSD_STATE
    lane = lax.broadcasted_iota(jnp.int32, (CHUNK, LANE), 1)
    ii, jj = _iota2(CHUNK, CHUNK)
    incl_lower = (jj <= ii).astype(F32)
    masks = [_pair_order_masks(d) for d in range(2)]
    dp = [(d, p) for d in range(2) for p in range(N_PAIRS)]
    n_tot = nc_c + nc_l
    assert N_PAIRS == SSD_GROUPS

    def prep(x_ref, s_ref, n_chunks, base):
        def prog(n):
            rows = pl.ds(pl.multiple_of(base + n * CHUNK, CHUNK), CHUNK)
            y = _conv(x_ref, convw, n, n_chunks) + convb[...]
            y = y * _sigmoid(y)
            xs, bs, cs = y[:, :w], y[:, w:w + gw], y[:, w + gw:]
            x_s[rows, :] = xs
            sm = s_ref[_chunk_rows(n), :]
            dt_all = _softplus(sm + prow[1:2, :])
            in_dt = (lane >= DT_LANE0) & (lane < DT_LANE0 + 2 * N_HEADS)
            la = jnp.where(in_dt, -jnp.exp(prow[0:1, :]) * dt_all, 0.0)
            prefix = _dot_exact_lhs(incl_lower, la)
            total = jnp.sum(la, axis=0, keepdims=True)
            g_all = jnp.where(lane < DT_LANE0 + N_HEADS, prefix, total - prefix + la)
            g_t = g_all.T
            bg = [bs[:, g * SSD_STATE:(g + 1) * SSD_STATE] for g in range(SSD_GROUPS)]
            cg = [cs[:, g * SSD_STATE:(g + 1) * SSD_STATE] for g in range(SSD_GROUPS)]
            cb = [_bdot_nt(cg[g], jnp.concatenate([bg[g], bg[g]], axis=0)) for g in range(SSD_GROUPS)]
            yield
            for d, p in dp:
                l = DT_LANE0 + d * N_HEADS + 2 * p
                cols = slice(p * LANE, (p + 1) * LANE)
                g_col = _pair_cols(g_all, l)
                g_row = jnp.concatenate([g_t[l:l + 1, :], g_t[l + 1:l + 2, :]], axis=1)
                decay = jnp.exp(jnp.where(masks[d][0], g_col - g_row, NEG_BIG))
                att_s[d, rows, cols] = (cb[p] * decay).astype(BF16)
                v_s[d, rows, cols] = (xs[:, cols] * _pair_cols(dt_all, l)).astype(BF16)
                qd_s[d, rows, cols] = (jnp.concatenate([cg[p], cg[p]], axis=1) * jnp.exp(g_col)).astype(BF16)
                kd_s[d, rows, cols] = (jnp.concatenate([bg[p], bg[p]], axis=1)
                                       * jnp.exp(_pair_cols(total, l) - g_col)).astype(BF16)
            gl_s[pl.ds(base // CHUNK + n, 1), :] = jnp.exp(total)
            yield
        _chunk_loop(prog, n_chunks)

    prep(xbc_c, sm_c, nc_c, 0)
    prep(xbc_l, sm_l, nc_l, nc_c * CHUNK)

    st_s[...] = jnp.zeros_like(st_s)
    o_s[...] = jnp.zeros_like(o_s)
    block = _head_block_mask()
    first_rows = lax.broadcasted_iota(jnp.int32, (LANE, 1), 0) < SSD_STATE

    def scan_body(trip, carry):
        state = [st_s[x] for x in range(len(dp))]
        steps = [trip * SCAN_UNROLL + u for u in range(SCAN_UNROLL)]
        n_dirs = [(step, _backward_chunk(step, nc_c, n_tot)) for step in steps]
        pre = []
        for n_dir in n_dirs:
            rows = [_chunk_rows(n) for n in n_dir]
            egl_rows = [gl_s[pl.ds(n, 1), :] for n in n_dir]
            per_chain = []
            for x, (d, p) in enumerate(dp):
                cols = slice(p * LANE, (p + 1) * LANE)
                l = DT_LANE0 + d * N_HEADS + 2 * p
                att, v = att_s[d, rows[d], cols], v_s[d, rows[d], cols]
                qd, kd = qd_s[d, rows[d], cols], kd_s[d, rows[d], cols]
                intra = jnp.dot(att, _pair_blocks(v), preferred_element_type=F32)
                delta = jnp.where(block, _bdot_tn(kd, v), 0.0)
                egl = jnp.where(first_rows, egl_rows[d][:, l:l + 1], egl_rows[d][:, l + 1:l + 2])
                per_chain.append((rows[d], cols, qd, intra, delta, egl))
            pre.append(per_chain)
        for per_chain in pre:
            for x, (rows_d, cols, qd, intra, delta, egl) in enumerate(per_chain):
                o = intra + jnp.dot(qd, state[x].astype(BF16), preferred_element_type=F32)
                state[x] = egl * state[x] + delta
                o_s[rows_d, cols] = o_s[rows_d, cols] + o
        for x in range(len(dp)):
            st_s[x] = state[x]
        return carry
    assert n_tot % SCAN_UNROLL == 0
    lax.fori_loop(0, n_tot // SCAN_UNROLL, scan_body, 0)

    def finish(z_ref, out_ref, n_chunks, base):
        def body(n, carry):
            rows_in = pl.ds(pl.multiple_of(base + n * MXU_ROWS, MXU_ROWS), MXU_ROWS)
            rows = pl.ds(pl.multiple_of(n * MXU_ROWS, MXU_ROWS), MXU_ROWS)
            zt = z_ref[rows, :]
            y = (o_s[rows_in, :] + dskip[...] * x_s[rows_in, :]) * (zt * _sigmoid(zt))
            ms = jnp.sum(y * y, axis=1, keepdims=True) * (1.0 / w)
            out_ref[rows, :] = y * lax.rsqrt(ms + NORM_EPS) * normw[...]
            return carry
        lax.fori_loop(0, n_chunks * CHUNK // MXU_ROWS, body, 0)

    finish(z_c, out_c, nc_c, 0)
    finish(z_l, out_l, nc_l, nc_c * CHUNK)


def ssd_mixer(zc, zl, conv_w, conv_b, a_log, dt_bias, d_skip, norm_w):
    b, tc, _ = zc.shape
    tl = zl.shape[1]
    nc_c, nc_l = tc // CHUNK, tl // CHUNK
    w = BRANCH_W
    xbc_blk, z_blk, sm_blk = SSD_BLOCKS
    prow = jnp.zeros((SUBLANE, LANE), F32)
    prow = prow.at[0, DT_LANE0:DT_LANE0 + 2 * N_HEADS].set(a_log.reshape(-1))
    prow = prow.at[1, DT_LANE0:DT_LANE0 + 2 * N_HEADS].set(dt_bias.reshape(-1))
    out_specs, out_shape = _mixer_out(b, tc, tl)
    return pl.pallas_call(
        functools.partial(_ssd_body, nc_c, nc_l),
        grid=(b,),
        in_specs=[_zspec(tc, XBC_W, xbc_blk), _zspec(tl, XBC_W, xbc_blk, pipeline_mode=pl.Buffered(1)),
                  _zspec(tc, w, z_blk), _zspec(tl, w, z_blk),
                  _zspec(tc, LANE, sm_blk), _zspec(tl, LANE, sm_blk),
                  _cspec((SUBLANE, XBC_W)), _cspec((1, XBC_W)), _cspec((SUBLANE, LANE)), _cspec((1, w)), _cspec((1, w))],
        out_specs=out_specs,
        out_shape=out_shape,
        scratch_shapes=[pltpu.VMEM((2, tc + tl, w), BF16)] * 4 + [
            pltpu.VMEM((tc + tl, w), F32), pltpu.VMEM((nc_c + nc_l, LANE), F32), pltpu.VMEM((tc + tl, w), F32),
            pltpu.VMEM((2 * N_PAIRS, LANE, LANE), F32)],
        compiler_params=_MIXER_PARAMS,
        name="ssd_mixer",
    )(zc, zl, zc, zl, zc, zl,
      jnp.pad(conv_w, ((0, SUBLANE - CONV_K), (0, 0))), conv_b[None, :], prow,
      jnp.repeat(d_skip, HEAD_DIM)[None, :], norm_w[None, :])


def _fnet_body(t, u0_ref, u1_ref, ceh, cel, coh, col, seh, sel, soh, sol, gch_ref, gcl_ref, gsh_ref, gsl_ref,
               out_ref, p_s, q_s):
    d = lambda a, b: jnp.dot(a, b, preferred_element_type=F32)
    half = t // 2
    tile = min(MXU_ROWS, half)

    def channel_dft(n, carry):
        rows = pl.ds(pl.multiple_of(n * tile, tile), tile)
        for parity in range(2):
            src = pl.ds(2 * n * tile + parity, tile, stride=2)
            uh, ul = _split(jnp.concatenate([u0_ref[src, :], u1_ref[src, :]], axis=1), 2)
            p = d(uh, gch_ref[...]) + d(ul, gch_ref[...]) + d(uh, gcl_ref[...])
            q = d(uh, gsh_ref[...]) + d(ul, gsh_ref[...]) + d(uh, gsl_ref[...])
            p_s[parity, 0, rows, :], p_s[parity, 1, rows, :] = _split(p, 2)
            q_s[parity, 0, rows, :], q_s[parity, 1, rows, :] = _split(q, 2)
        return carry
    lax.fori_loop(0, half // tile, channel_dft, 0)

    def sequence_dft(n, carry):
        rows = pl.ds(pl.multiple_of(n * tile, tile), tile)
        parts = []
        for parity, (ch, cl, sh, sl) in enumerate(((ceh, cel, seh, sel), (coh, col, soh, sol))):
            ph, plo, qh, qlo = p_s[parity, 0], p_s[parity, 1], q_s[parity, 0], q_s[parity, 1]
            re = d(ch[rows, :], ph) + d(cl[rows, :], ph) + d(ch[rows, :], plo)
            im = d(sh[rows, :], qh) + d(sl[rows, :], qh) + d(sh[rows, :], qlo)
            parts.append(re - im)
        out_ref[rows, :] = parts[0] + parts[1]
        out_ref[pl.ds(pl.multiple_of(half + n * tile, tile), tile), :] = parts[0] - parts[1]
        return carry
    lax.fori_loop(0, half // tile, sequence_dft, 0)


def _hi_lo(tab):
    hi = tab.astype(BF16)
    return [hi, (tab - hi.astype(F32)).astype(BF16)]


def _dft_tables(n, scale):
    j = lax.broadcasted_iota(jnp.int32, (n, n), 0)
    k = lax.broadcasted_iota(jnp.int32, (n, n), 1)
    ang = ((j * k) % n).astype(F32) * (2.0 * math.pi / n)
    return _hi_lo(jnp.cos(ang) * scale) + _hi_lo(jnp.sin(ang) * scale)


def _half_dft_tables(t):
    j = lax.broadcasted_iota(jnp.int32, (t // 2, t // 2), 0)
    m = lax.broadcasted_iota(jnp.int32, (t // 2, t // 2), 1)
    out = {}
    for parity in range(2):
        ang = ((j * (2 * m + parity)) % t).astype(F32) * (2.0 * math.pi / t)
        out["c", parity], out["s", parity] = _hi_lo(jnp.cos(ang) * t ** -0.5), _hi_lo(jnp.sin(ang) * t ** -0.5)
    return out["c", 0] + out["c", 1] + out["s", 0] + out["s", 1]


def fnet_tables(t):
    grp = _dft_tables(FOURIER_GROUP_W, FOURIER_GROUP_W ** -0.5)
    n_grp = BRANCH_W // FOURIER_GROUP_W
    grp = [jnp.kron(jnp.eye(n_grp, dtype=F32), g.astype(F32)).astype(BF16) for g in grp]
    return _half_dft_tables(t) + grp


def fnet_mixer(z, tables):
    b, t, _ = z.shape
    w = BRANCH_W
    const = functools.partial(_cspec, pipeline_mode=pl.Buffered(1))
    first = FNET_BLOCK * w // LANE
    return pl.pallas_call(
        functools.partial(_fnet_body, t),
        grid=(b,),
        in_specs=[_zspec(t, LANE, first), _zspec(t, LANE, first + 1)] + [const((t // 2, t // 2))] * 8 + [const((w, w))] * 4,
        out_specs=pl.BlockSpec((None, t, w), lambda i: (i, 0, 0)),
        out_shape=jax.ShapeDtypeStruct((b, t, w), F32),
        scratch_shapes=[pltpu.VMEM((2, 2, t // 2, w), BF16)] * 2,
        compiler_params=_MIXER_PARAMS,
        name="fnet_mixer",
    )(z, z, *tables)


def _ln_rows(x):
    mu = jnp.mean(x, axis=-1, keepdims=True)
    xc = x - mu
    return xc * lax.rsqrt(jnp.mean(xc * xc, axis=-1, keepdims=True) + NORM_EPS)


def _row_spec(width):
    return pl.BlockSpec((MXU_ROWS, width), lambda i: (i, 0))


def _layer_spec(shape, l):
    return pl.BlockSpec((None,) + shape, lambda *_: (l,) + (0,) * len(shape), pipeline_mode=pl.Buffered(1))


def _mod_spec(l, j, row):
    return pl.BlockSpec((None, None, None, 1, D_MODEL), lambda *idx: (l, row(*idx), j, 0, 0))


def _stream_row(t, fixed_row):
    return (lambda i: fixed_row) if fixed_row is not None else (lambda i: i // (t // MXU_ROWS))


def _adaln_body(c_ref, w_ref, b_ref, o_ref):
    cc = c_ref[...]
    act = (cc * _sigmoid(cc)).astype(BF16)
    o_ref[...] = jnp.dot(act, w_ref[...].astype(BF16), preferred_element_type=F32) + b_ref[...]


def adaln_modulation(cond, ada_w, ada_b):
    r, d = cond.shape
    n_layers, _, n = ada_w.shape
    tn = n // 4
    return pl.pallas_call(
        _adaln_body,
        grid=(n_layers, n // tn),
        in_specs=[pl.BlockSpec((r, d), lambda l, j: (0, 0)), pl.BlockSpec((None, d, tn), lambda l, j: (l, 0, j)),
                  pl.BlockSpec((None, 1, tn), lambda l, j: (l, 0, j))],
        out_specs=pl.BlockSpec((None, r, tn), lambda l, j: (l, 0, j)),
        out_shape=jax.ShapeDtypeStruct((n_layers, r, n), F32),
        compiler_params=pltpu.CompilerParams(dimension_semantics=("parallel", "parallel"), vmem_limit_bytes=VMEM_LIMIT),
        name="adaln",
    )(cond, ada_w, ada_b[:, None, :])


def _modproj_body(x_ref, sh_ref, sc_ref, w_ref, z_ref, h_ref):
    h = (_ln_rows(x_ref[...]) * (1.0 + sc_ref[...]) + sh_ref[...]).astype(BF16)
    h_ref[...] = h
    z_ref[...] = jnp.dot(h, w_ref[...], preferred_element_type=F32)


def modulated_project(x, mod, w, l, t, fixed_row):
    m, d = x.shape
    n = w.shape[-1]
    row = _stream_row(t, fixed_row)
    return pl.pallas_call(
        _modproj_body,
        grid=(m // MXU_ROWS,),
        in_specs=[_row_spec(d), _mod_spec(l, 0, row), _mod_spec(l, 1, row), _layer_spec((d, n), l)],
        out_specs=[_row_spec(n), _row_spec(d)],
        out_shape=[jax.ShapeDtypeStruct((m, n), F32), jax.ShapeDtypeStruct((m, d), BF16)],
        compiler_params=_MIXER_PARAMS,
        name="modulated_project",
    )(x, mod, mod, w)


def _merge_body(h_ref, oa_ref, ob_ref, oc_ref, od_ref, x_ref, gate1_ref, sh2_ref, sc2_ref, lng_ref, lnb_ref,
                wg_ref, bg_ref, wb_ref, wo_ref, wr_ref, xo_ref, h2_ref, aff_ref):
    h = h_ref[...]
    acc = None
    for g, o_ref in enumerate((oa_ref, ob_ref, oc_ref, od_ref)):
        gate = _sigmoid(jnp.dot(h, wg_ref[g], preferred_element_type=F32) + bg_ref[g])
        t = gate * jnp.dot(o_ref[...].astype(BF16), wb_ref[g], preferred_element_type=F32)
        acc = t if acc is None else acc + t
    y = jnp.dot(acc.astype(BF16), wo_ref[...], preferred_element_type=F32)
    x_new = _ln_rows(DEEPNORM_ALPHA * x_ref[...] + gate1_ref[...] * y) * lng_ref[...] + lnb_ref[...]
    xo_ref[...] = x_new
    h2 = (_ln_rows(x_new) * (1.0 + sc2_ref[...]) + sh2_ref[...]).astype(BF16)
    h2_ref[...] = h2
    logits = jnp.dot(h2, wr_ref[...], preferred_element_type=F32)
    valid = lax.broadcasted_iota(jnp.int32, logits.shape, 1) < N_EXPERTS
    logits = jnp.where(valid, logits, NEG_BIG)
    e = jnp.exp(logits - jnp.max(logits, axis=-1, keepdims=True))
    aff_ref[...] = e / jnp.sum(e, axis=-1, keepdims=True)


def merge_and_norm(h, outs, x, mod, p, l, t, fixed_row):
    m, d = x.shape
    w = outs[0].shape[1]
    row = _stream_row(t, fixed_row)
    stacked = lambda name: _layer_spec(p[name].shape[1:], l)
    return pl.pallas_call(
        _merge_body,
        grid=(m // MXU_ROWS,),
        in_specs=[_row_spec(d)] + [_row_spec(w)] * N_BRANCH + [_row_spec(d)] + [_mod_spec(l, j, row) for j in (2, 3, 4)]
        + [stacked(name) for name in ("ln1_g", "ln1_b", "w_gate", "b_gate", "w_branch", "w_out", "w_router")],
        out_specs=[_row_spec(d), _row_spec(d), _row_spec(LANE)],
        out_shape=[jax.ShapeDtypeStruct((m, d), F32), jax.ShapeDtypeStruct((m, d), BF16),
                   jax.ShapeDtypeStruct((m, LANE), F32)],
        compiler_params=_MIXER_PARAMS,
        name="merge_and_norm",
    )(h, *outs, x, mod, mod, mod, p["ln1_g"], p["ln1_b"], p["w_gate"], p["b_gate"], p["w_branch"], p["w_out"], p["w_router"])


def _expert_body(x_ref, wt_ref, wg_ref, wu_ref, wd_ref, yh_ref, yl_ref, wg_s, wu_s, wd_s):
    @pl.when(pl.program_id(1) == 0)
    def _():
        wg_s[...] = wg_ref[...].astype(BF16)
        wu_s[...] = wu_ref[...].astype(BF16)
        wd_s[...] = wd_ref[...].astype(BF16)

    bb, cap, d = x_ref.shape
    x = x_ref[...].reshape(bb * cap, d)
    gate = jnp.dot(x, wg_s[...], preferred_element_type=F32)
    up = jnp.dot(x, wu_s[...], preferred_element_type=F32)
    hid = (gate * _sigmoid(gate) * up).astype(BF16)
    y = jnp.dot(hid, wd_s[...], preferred_element_type=F32) * wt_ref[...].reshape(bb * cap, 1)
    yh, yl = _split(y, 2)
    yh_ref[...] = yh.reshape(yh_ref.shape)
    yl_ref[...] = yl.reshape(yl_ref.shape)


def expert_swiglu(xe, weight, w_ff_gate, w_ff_up, w_ff_down, l):
    b, e, cap, d = xe.shape
    f = w_ff_gate.shape[-1]
    bb = min(b, max(1, MXU_ROWS // cap))
    x_spec = pl.BlockSpec((bb, None, cap, d), lambda ei, bi: (bi, ei, 0, 0))
    wt_spec = pl.BlockSpec((bb, None, cap, 1), lambda ei, bi: (bi, ei, 0, 0))
    w_spec = lambda shape: pl.BlockSpec((None, None) + shape, lambda ei, bi: (l, ei, 0, 0))
    return pl.pallas_call(
        _expert_body,
        grid=(e, b // bb),
        in_specs=[x_spec, wt_spec, w_spec((d, f)), w_spec((d, f)), w_spec((f, d))],
        out_specs=[x_spec, x_spec],
        out_shape=[jax.ShapeDtypeStruct(xe.shape, BF16)] * 2,
        scratch_shapes=[pltpu.VMEM((d, f), BF16), pltpu.VMEM((d, f), BF16), pltpu.VMEM((f, d), BF16)],
        compiler_params=pltpu.CompilerParams(dimension_semantics=("parallel", "arbitrary"),
                                             vmem_limit_bytes=VMEM_LIMIT),
        name="expert_swiglu",
    )(xe, weight[..., None], w_ff_gate, w_ff_up, w_ff_down)


def _combine_body(n_slots, idx_ref, yh_ref, yl_ref, x_ref, gate2_ref, lng_ref, lnb_ref, o_ref):
    tile = pl.program_id(1)
    token = tile * MXU_ROWS + lax.broadcasted_iota(jnp.int32, (MXU_ROWS, n_slots), 0)
    onehot = jnp.where(idx_ref[...] == token, 1.0, 0.0).astype(BF16)
    moe = (jnp.dot(onehot, yh_ref[...], preferred_element_type=F32)
           + jnp.dot(onehot, yl_ref[...], preferred_element_type=F32))
    o_ref[...] = _ln_rows(DEEPNORM_ALPHA * x_ref[...] + gate2_ref[...] * moe) * lng_ref[...] + lnb_ref[...]


def combine_and_norm(idx, yh, yl, x, mod, ln_g, ln_b, l, fixed_row):
    b, t, d = x.shape
    n_slots = idx.shape[-1]
    per_sample = lambda shape: pl.BlockSpec((None,) + shape, lambda i, j: (i, 0, 0))
    tile = pl.BlockSpec((None, MXU_ROWS, d), lambda i, j: (i, j, 0))
    vec = _layer_spec((1, d), l)
    row = (lambda i, j: fixed_row) if fixed_row is not None else (lambda i, j: i)
    return pl.pallas_call(
        functools.partial(_combine_body, n_slots),
        grid=(b, t // MXU_ROWS),
        in_specs=[per_sample((1, n_slots)), per_sample((n_slots, d)), per_sample((n_slots, d)), tile,
                  _mod_spec(l, 5, row), vec, vec],
        out_specs=tile,
        out_shape=jax.ShapeDtypeStruct((b, t, d), F32),
        compiler_params=pltpu.CompilerParams(dimension_semantics=("parallel", "arbitrary"),
                                             vmem_limit_bytes=VMEM_LIMIT),
        name="combine_and_norm",
    )(idx, yh, yl, x, mod, ln_g, ln_b)


def sincos_grid(rows, cols, dim):
    quarter = dim // 4
    omega = 1.0 / (10000.0 ** (jnp.arange(quarter, dtype=F32) / quarter))
    er = jnp.arange(rows, dtype=F32)[:, None] * omega
    ec = jnp.arange(cols, dtype=F32)[:, None] * omega
    er = jnp.concatenate([jnp.sin(er), jnp.cos(er)], axis=-1)
    ec = jnp.concatenate([jnp.sin(ec), jnp.cos(ec)], axis=-1)
    emb = jnp.concatenate([jnp.broadcast_to(er[:, None, :], (rows, cols, dim // 2)),
                           jnp.broadcast_to(ec[None, :, :], (rows, cols, dim // 2))], axis=-1)
    return emb.reshape(rows * cols, dim)


def hgrn_lower_bounds(logits):
    cum = jnp.cumsum(jax.nn.softmax(logits.astype(F32), axis=1), axis=1)
    return cum - cum[:, :1]


def _flat(a):
    return a.reshape(-1, a.shape[-1])


def token_mixer(xc, xl, mod, p, fnet_tabs, b, l, with_ctx):
    tc, tl = xc.shape[0] // b, xl.shape[0] // b
    zc, hc = modulated_project(xc, mod, p["w_in"], l, tc, b)
    zl, hl = modulated_project(xl, mod, p["w_in"], l, tl, None)
    zc, zl = zc.reshape(b, tc, Z_COLS), zl.reshape(b, tl, Z_COLS)
    a_c, a_l = hgrn_mixer(zc, zl, p["lb"][:, l], p["hgrn_norm_w"][l])
    b_c, b_l = gdn_mixer(zc, zl, p["gdn_conv_w"][l], p["gdn_a_log"][l], p["gdn_dt_bias"][l], p["gdn_norm_w"][l])
    d_c, d_l = ssd_mixer(zc, zl, p["ssd_conv_w"][l], p["ssd_conv_b"][l], p["ssd_a_log"][l], p["ssd_dt_bias"][l],
                         p["ssd_d"][l], p["ssd_norm_w"][l])

    def merged(h, outs, x, t, fixed_row):
        return merge_and_norm(h, tuple(_flat(o) for o in outs), x, mod, p, l, t, fixed_row)

    out_l = merged(hl, (a_l, b_l, fnet_mixer(zl, fnet_tabs[1]), d_l), xl, tl, None)
    out_c = merged(hc, (a_c, b_c, fnet_mixer(zc, fnet_tabs[0]), d_c), xc, tc, b) if with_ctx else None
    return out_c, out_l


def channel_mixer(x_mid, h, aff, b, mod, p, l, fixed_row):
    t_ = h.shape[0] // b
    d = h.shape[-1]
    h = h.reshape(b, t_, d)
    cap = EC_CAPACITY_FACTOR * t_ // N_EXPERTS
    aff = aff[:, :N_EXPERTS].reshape(b, t_, N_EXPERTS)
    weight, idx = lax.top_k(jnp.swapaxes(aff, 1, 2), cap)
    xe = h[jnp.arange(b)[:, None, None], idx]
    yh, yl = expert_swiglu(xe, weight, p["w_ff_gate"], p["w_ff_up"], p["w_ff_down"], l)
    n_slots = N_EXPERTS * cap
    x_new = combine_and_norm(idx.reshape(b, 1, n_slots), yh.reshape(b, n_slots, d), yl.reshape(b, n_slots, d),
                             x_mid.reshape(b, t_, d), mod, p["ln2_g"], p["ln2_b"], l, fixed_row)
    return _flat(x_new)


def kernel(x, c, ctx, c_ctx, ada_w, ada_b, w_in, hgrn_lb_logits, hgrn_norm_w,
           gdn_conv_w, gdn_a_log, gdn_dt_bias, gdn_norm_w,
           ssd_conv_w, ssd_conv_b, ssd_a_log, ssd_dt_bias, ssd_d, ssd_norm_w,
           w_gate, b_gate, w_branch, w_out, ln1_g, ln1_b,
           w_router, w_ff_gate, w_ff_up, w_ff_down, ln2_g, ln2_b):
    b, n_lat, d = x.shape
    n_ctx = ctx.shape[1]
    rows = n_lat // GRID_W
    xl = _flat(x + sincos_grid(rows, GRID_W, D_MODEL).astype(x.dtype))
    xc = _flat(ctx)
    fnet_tabs = (fnet_tables(n_ctx), fnet_tables(n_lat))
    cond = jnp.concatenate([c, c_ctx[None, :], jnp.zeros((-(b + 1) % SUBLANE, d), c.dtype)], axis=0)
    mod = adaln_modulation(cond, ada_w, ada_b).reshape(DEPTH, cond.shape[0], 6, 1, d)
    p = {
        "w_in": permute_w_in(w_in).astype(BF16), "lb": hgrn_lower_bounds(hgrn_lb_logits), "hgrn_norm_w": hgrn_norm_w,
        "gdn_conv_w": gdn_conv_w, "gdn_a_log": gdn_a_log, "gdn_dt_bias": gdn_dt_bias, "gdn_norm_w": gdn_norm_w,
        "ssd_conv_w": ssd_conv_w, "ssd_conv_b": ssd_conv_b, "ssd_a_log": ssd_a_log, "ssd_dt_bias": ssd_dt_bias,
        "ssd_d": ssd_d, "ssd_norm_w": ssd_norm_w,
        "w_gate": w_gate.astype(BF16), "b_gate": b_gate[:, :, None, :], "w_branch": w_branch.astype(BF16),
        "w_out": w_out.astype(BF16), "ln1_g": ln1_g[:, None, :], "ln1_b": ln1_b[:, None, :],
        "w_router": jnp.pad(w_router, ((0, 0), (0, 0), (0, LANE - N_EXPERTS))).astype(BF16),
        "w_ff_gate": w_ff_gate, "w_ff_up": w_ff_up, "w_ff_down": w_ff_down,
        "ln2_g": ln2_g[:, None, :], "ln2_b": ln2_b[:, None, :],
    }
    for l in range(DEPTH):
        with_ctx = l < DEPTH - 1
        out_c, out_l = token_mixer(xc, xl, mod, p, fnet_tabs, b, l, with_ctx)
        xl = channel_mixer(*out_l, b, mod, p, l, None)
        if with_ctx:
            xc = channel_mixer(*out_c, b, mod, p, l, b)
    return xl.reshape(b, n_lat, d)
```

```python
import functools
import math

import jax
import jax.numpy as jnp
import numpy as np
from jax import lax
from jax.experimental import pallas as pl
from jax.experimental.pallas import tpu as pltpu

D_MODEL = 1024
DEPTH = 4
GRID_W = 64
N_BRANCH = 4
BRANCH_W = D_MODEL // N_BRANCH
HEAD_DIM = 64
N_HEADS = BRANCH_W // HEAD_DIM
N_PAIRS = N_HEADS // 2
FOURIER_GROUP_W = HEAD_DIM
CHUNK = 64
CONV_K = 5
SSD_STATE = 64
SSD_GROUPS = 2
N_EXPERTS = 16
EC_CAPACITY_FACTOR = 2
DEEPNORM_ALPHA = (2.0 * DEPTH) ** 0.25
NORM_EPS = 1e-6

LANE = 128
SUBLANE = 8
MXU_ROWS = 256
VMEM_LIMIT = 48 * 1024 * 1024
NEG_BIG = -1e30

F32 = jnp.float32
BF16 = jnp.bfloat16

XBC_W = BRANCH_W + 2 * SSD_GROUPS * SSD_STATE
Z_ORDER = ("b_qkv", "a_q", "a_f_fwd", "a_f_bwd", "a_v", "a_g", "b_g", "c_u", "d_xbc", "d_z", "b_a", "b_beta", "d_dt")
REF_SPLITS = (
    ("a_q", BRANCH_W), ("a_f_fwd", BRANCH_W), ("a_f_bwd", BRANCH_W), ("a_v", BRANCH_W), ("a_g", BRANCH_W),
    ("b_qkv", 3 * BRANCH_W), ("b_g", BRANCH_W), ("b_a", 2 * N_HEADS), ("b_beta", 2 * N_HEADS),
    ("c_u", BRANCH_W), ("d_xbc", XBC_W), ("d_z", BRANCH_W), ("d_dt", 2 * N_HEADS),
)
Z_COLS = 3 * BRANCH_W + 5 * BRANCH_W + 2 * BRANCH_W + XBC_W + BRANCH_W + LANE
GDN_BLOCKS = (0, 8, 26)
HGRN_BLOCKS = (3, 4, 5, 6, 7)
FNET_BLOCK = 9
SSD_BLOCKS = (5, 12, 26)
DT_LANE0 = 4 * N_HEADS


def permute_w_in(w_in):
    start, spans = 0, {}
    for name, size in REF_SPLITS:
        spans[name] = (start, start + size)
        start += size
    parts = [w_in[..., spans[n][0]:spans[n][1]] for n in Z_ORDER]
    used = sum(p.shape[-1] for p in parts)
    return jnp.concatenate(parts + [jnp.zeros(w_in.shape[:-1] + (Z_COLS - used,), w_in.dtype)], axis=-1)


def _bdot(a, b):
    return jnp.dot(a.astype(BF16), b.astype(BF16), preferred_element_type=F32)


def _bdot_nt(a, b):
    return lax.dot_general(a.astype(BF16), b.astype(BF16), (((1,), (1,)), ((), ())), preferred_element_type=F32)


def _bdot_tn(a, b):
    return lax.dot_general(a.astype(BF16), b.astype(BF16), (((0,), (0,)), ((), ())), preferred_element_type=F32)


def _split(x, terms):
    out = []
    for _ in range(terms):
        p = x.astype(BF16)
        out.append(p)
        x = x - p.astype(F32)
    return out


def _dot_exact_lhs(m, x, terms=3):
    mb = m.astype(BF16)
    return sum(jnp.dot(mb, p, preferred_element_type=F32) for p in _split(x, terms))


def _dot_exact_rhs(x, m, terms=2):
    mb = m.astype(BF16)
    return sum(jnp.dot(p, mb, preferred_element_type=F32) for p in _split(x, terms))


def _iota2(n, m):
    return lax.broadcasted_iota(jnp.int32, (n, m), 0), lax.broadcasted_iota(jnp.int32, (n, m), 1)


def _sigmoid(x):
    return 1.0 / (1.0 + jnp.exp(-x))


def _softplus(x):
    return jnp.maximum(x, 0.0) + jnp.log(1.0 + jnp.exp(-jnp.abs(x)))


def _conv(x_ref, w_ref, n, n_chunks):
    t = n_chunks * CHUNK
    start = pl.multiple_of(n * CHUNK, CHUNK)
    cur = x_ref[pl.ds(start, CHUNK), :]
    prev_start = pl.multiple_of(jnp.maximum(start - SUBLANE, 0), SUBLANE)
    next_start = pl.multiple_of(jnp.minimum(start + CHUNK, t - SUBLANE), SUBLANE)
    prev = x_ref[pl.ds(prev_start, SUBLANE), :] * jnp.where(n > 0, 1.0, 0.0)
    nxt = x_ref[pl.ds(next_start, SUBLANE), :] * jnp.where(n < n_chunks - 1, 1.0, 0.0)
    ext = jnp.concatenate([prev, cur, nxt], axis=0)
    pad = CONV_K // 2
    acc = None
    n_ext = CHUNK + 2 * SUBLANE
    for k in range(CONV_K):
        shifted = ext if k == pad else pltpu.roll(ext, (pad - k) % n_ext, 0)
        term = shifted[SUBLANE:SUBLANE + CHUNK, :] * w_ref[k:k + 1, :]
        acc = term if acc is None else acc + term
    return acc


PREP_UNROLL = 4
SCAN_UNROLL = 6


def _interleave(progs):
    live = list(progs)
    while live:
        nxt = []
        for p in live:
            try:
                next(p)
                nxt.append(p)
            except StopIteration:
                pass
        live = nxt


def _chunk_loop(prog, n_chunks):
    def body(i, carry):
        _interleave([prog(i * PREP_UNROLL + j) for j in range(PREP_UNROLL)])
        return carry
    lax.fori_loop(0, n_chunks // PREP_UNROLL, body, 0)


def _backward_chunk(step, nc_c, n_tot):
    return jnp.where(step < nc_c, nc_c - 1 - step, n_tot - 1 - (step - nc_c))


def _chunk_rows(n):
    return pl.ds(pl.multiple_of(n * CHUNK, CHUNK), CHUNK)


def _head_norm_gate(o_s, gate_ref, out_ref, normw, bdm, n_chunks, base):
    def body(n, carry):
        rows = pl.ds(pl.multiple_of(n * MXU_ROWS, MXU_ROWS), MXU_ROWS)
        o = o_s[pl.ds(pl.multiple_of(base + n * MXU_ROWS, MXU_ROWS), MXU_ROWS), :]
        ms = _dot_exact_rhs(o * o, bdm) * (1.0 / HEAD_DIM)
        gt = gate_ref[rows, :]
        out_ref[rows, :] = o * lax.rsqrt(ms + NORM_EPS) * normw[...] * (gt * _sigmoid(gt))
        return carry
    lax.fori_loop(0, n_chunks * CHUNK // MXU_ROWS, body, 0)


def head_block_ones():
    i = np.arange(BRANCH_W)
    return jnp.asarray((i[:, None] // HEAD_DIM) == (i[None, :] // HEAD_DIM), BF16)


def _zspec(t, width, blk, **kw):
    return pl.BlockSpec((None, t, width), lambda i: (i, 0, blk), **kw)


def _cspec(shape, **kw):
    return pl.BlockSpec(shape, lambda *_: (0,) * len(shape), **kw)


def _mixer_out(b, tc, tl):
    assert tc % MXU_ROWS == 0 and tl % MXU_ROWS == 0 and MXU_ROWS % (PREP_UNROLL * CHUNK) == 0
    specs = [pl.BlockSpec((None, tc, BRANCH_W), lambda i: (i, 0, 0)), pl.BlockSpec((None, tl, BRANCH_W), lambda i: (i, 0, 0))]
    shapes = [jax.ShapeDtypeStruct((b, tc, BRANCH_W), F32), jax.ShapeDtypeStruct((b, tl, BRANCH_W), F32)]
    return specs, shapes


_MIXER_PARAMS = pltpu.CompilerParams(dimension_semantics=("parallel",), vmem_limit_bytes=VMEM_LIMIT)


def _pair_blocks(x):
    first = (lax.broadcasted_iota(jnp.int32, x.shape, 1) % LANE) < HEAD_DIM
    zero = jnp.zeros_like(x)
    return jnp.concatenate([jnp.where(first, x, zero), jnp.where(first, zero, x)], axis=0)


def _pair_cols(x, l0):
    lane = lax.broadcasted_iota(jnp.int32, (x.shape[0], LANE), 1)
    return jnp.where(lane < HEAD_DIM, x[:, l0:l0 + 1], x[:, l0 + 1:l0 + 2])


def _pair_iota():
    i, j = _iota2(CHUNK, LANE)
    return i, j % HEAD_DIM


def _pair_order_masks(d):
    i, j = _pair_iota()
    return ((j <= i), (j < i)) if d == 0 else ((j >= i), (j > i))


def _pair_dot(x, y):
    return jnp.dot(x.astype(BF16), _pair_blocks(y.astype(BF16)), preferred_element_type=F32)


def _pair_dot_x3(x, y):
    xh, xl = _split(x, 2)
    yh, yl = (_pair_blocks(t) for t in _split(y, 2))
    d = lambda p, q: jnp.dot(p, q, preferred_element_type=F32)
    return d(xh, yh) + d(xl, yh) + d(xh, yl)


def _head_block_mask():
    r, c = _iota2(LANE, LANE)
    return (r < HEAD_DIM) == (c < HEAD_DIM)


def _pairs(x):
    return [x[:, p * LANE:(p + 1) * LANE] for p in range(N_PAIRS)]


def _unit_lower_inverses(mats):
    i, j = _pair_iota()
    eye = (i == j).astype(F32)
    same4 = (i // 4) == (j // 4)
    d4 = [jnp.where(same4, a, 0.0) for a in mats]
    sq = [_pair_dot(d, d) for d in d4]
    xs = [eye - d for d in d4]
    yield
    xs = [x + _pair_dot(x, q) for x, q in zip(xs, sq)]
    yield
    s = 4
    while s < CHUNK:
        sel = ((i // (2 * s)) == (j // (2 * s))) & ((i // s) != (j // s))
        ox = [_pair_dot(jnp.where(sel, a, 0.0), x) for a, x in zip(mats, xs)]
        yield
        xs = [x - _pair_dot(x, y) for x, y in zip(xs, ox)]
        yield
        s *= 2
    return xs


def _gdn_body(nc_c, nc_l, qkv_c, qkv_l, gate_c, gate_l, sm_c, sm_l, convw, prow, normw, bd,
              out_c, out_l, u_s, w_s, qk_s, qd_s, kd_s, gl_s, o_s, st_s):
    w = BRANCH_W
    bdm = bd[...]
    lane = lax.broadcasted_iota(jnp.int32, (CHUNK, LANE), 1)
    ii, jj = _iota2(CHUNK, CHUNK)
    incl_lower = (jj <= ii).astype(F32)
    masks = [_pair_order_masks(d) for d in range(2)]
    dp = [(d, p) for d in range(2) for p in range(N_PAIRS)]
    n_tot = nc_c + nc_l

    def prep(x_ref, s_ref, n_chunks, base):
        def prog(n):
            y = _conv(x_ref, convw, n, n_chunks)
            y = y * _sigmoid(y)
            q, k, v = y[:, :w], y[:, w:2 * w], y[:, 2 * w:]
            qss, kss = _dot_exact_rhs(q * q, bdm), _dot_exact_rhs(k * k, bdm)
            yield
            q = q * lax.rsqrt(qss + NORM_EPS) * HEAD_DIM ** -0.5
            k = k * lax.rsqrt(kss + NORM_EPS)
            sm = s_ref[_chunk_rows(n), :]
            la = jnp.where(lane < 2 * N_HEADS, -jnp.exp(prow[0:1, :]) * _softplus(sm + prow[1:2, :]), 0.0)
            beta_all = _sigmoid(sm)
            prefix = _dot_exact_lhs(incl_lower, la)
            total = jnp.sum(la, axis=0, keepdims=True)
            g_all = jnp.where(lane < N_HEADS, prefix, total - prefix + la)
            g_t = g_all.T
            qp, kp, vp = _pairs(q), _pairs(k), _pairs(v)
            qkk = [lax.dot_general(jnp.concatenate([qp[p], kp[p]], axis=0).astype(BF16), _pair_blocks(kp[p].astype(BF16)),
                                   (((1,), (1,)), ((), ())), preferred_element_type=F32) for p in range(N_PAIRS)]
            yield
            l0 = [d * N_HEADS + 2 * p for d, p in dp]
            g_col = [_pair_cols(g_all, l) for l in l0]
            g_row = [jnp.concatenate([g_t[l:l + 1, :], g_t[l + 1:l + 2, :]], axis=1) for l in l0]
            beta = [_pair_cols(beta_all, 2 * N_HEADS + l) for l in l0]
            gl = [_pair_cols(total, l) for l in l0]
            decay = [jnp.exp(jnp.where(masks[d][0], g_col[x] - g_row[x], NEG_BIG)) for x, (d, p) in enumerate(dp)]
            qk = [qkk[p][:CHUNK] * decay[x] for x, (d, p) in enumerate(dp)]
            a = [jnp.where(masks[d][1], beta[x] * qkk[p][CHUNK:] * decay[x], 0.0) for x, (d, p) in enumerate(dp)]
            tinv = yield from _unit_lower_inverses(a)
            eg = [jnp.exp(g) for g in g_col]
            rhs = [jnp.concatenate([beta[x] * vp[p], beta[x] * kp[p] * eg[x]], axis=1) for x, (d, p) in enumerate(dp)]
            uw = [_pair_dot(t, r) for t, r in zip(tinv, rhs)]
            yield
            resid = [r - y0 - _pair_dot_x3(m, y0) for r, y0, m in zip(rhs, uw, a)]
            yield
            uw = [y0 + _pair_dot(t, r) for y0, t, r in zip(uw, tinv, resid)]
            yield
            rows = pl.ds(pl.multiple_of(base + n * CHUNK, CHUNK), CHUNK)
            for x, (d, p) in enumerate(dp):
                cols = slice(p * LANE, (p + 1) * LANE)
                u_s[d, rows, cols] = uw[x][:, :LANE]
                w_s[d, rows, cols] = uw[x][:, LANE:].astype(BF16)
                qk_s[d, rows, cols] = qk[x].astype(BF16)
                qd_s[d, rows, cols] = (qp[p] * eg[x]).astype(BF16)
                kd_s[d, rows, cols] = (kp[p] * jnp.exp(gl[x] - g_col[x])).astype(BF16)
            gl_s[pl.ds(base // CHUNK + n, 1), :] = jnp.exp(total)
            yield
        _chunk_loop(prog, n_chunks)

    prep(qkv_c, sm_c, nc_c, 0)
    prep(qkv_l, sm_l, nc_l, nc_c * CHUNK)

    st_s[...] = jnp.zeros_like(st_s)
    o_s[...] = jnp.zeros_like(o_s)
    block = _head_block_mask()
    first_rows = lax.broadcasted_iota(jnp.int32, (LANE, 1), 0) < HEAD_DIM

    def scan_body(step, carry):
        n_dir = (step, _backward_chunk(step, nc_c, n_tot))
        rows = [_chunk_rows(n) for n in n_dir]
        egl_rows = [gl_s[pl.ds(n, 1), :] for n in n_dir]
        tiles = lambda ref, x: ref[dp[x][0], rows[dp[x][0]], dp[x][1] * LANE:(dp[x][1] + 1) * LANE]
        s_prev = [st_s[x] for x in range(len(dp))]
        v_new = [tiles(u_s, x) - jnp.dot(tiles(w_s, x), s_prev[x].astype(BF16), preferred_element_type=F32)
                 for x in range(len(dp))]
        o = [jnp.dot(tiles(qd_s, x), s_prev[x].astype(BF16), preferred_element_type=F32)
             + jnp.dot(tiles(qk_s, x), _pair_blocks(v_new[x].astype(BF16)), preferred_element_type=F32)
             for x in range(len(dp))]
        for x, (d, p) in enumerate(dp):
            l = d * N_HEADS + 2 * p
            egl = jnp.where(first_rows, egl_rows[d][:, l:l + 1], egl_rows[d][:, l + 1:l + 2])
            st_s[x] = egl * s_prev[x] + jnp.where(block, _bdot_tn(tiles(kd_s, x), v_new[x]), 0.0)
            cols = slice(p * LANE, (p + 1) * LANE)
            o_s[rows[d], cols] = o_s[rows[d], cols] + o[x]
        return carry
    lax.fori_loop(0, n_tot, scan_body, 0)

    _head_norm_gate(o_s, gate_c, out_c, normw, bdm, nc_c, 0)
    _head_norm_gate(o_s, gate_l, out_l, normw, bdm, nc_l, nc_c * CHUNK)


def gdn_mixer(zc, zl, conv_w, a_log, dt_bias, norm_w):
    b, tc, _ = zc.shape
    tl = zl.shape[1]
    nc_c, nc_l = tc // CHUNK, tl // CHUNK
    w = BRANCH_W
    qkv_blk, gate_blk, sm_blk = GDN_BLOCKS
    prow = jnp.zeros((SUBLANE, LANE), F32)
    prow = prow.at[0, :2 * N_HEADS].set(a_log.reshape(-1)).at[1, :2 * N_HEADS].set(dt_bias.reshape(-1))
    out_specs, out_shape = _mixer_out(b, tc, tl)
    return pl.pallas_call(
        functools.partial(_gdn_body, nc_c, nc_l),
        grid=(b,),
        in_specs=[_zspec(tc, 3 * w, qkv_blk), _zspec(tl, 3 * w, qkv_blk, pipeline_mode=pl.Buffered(1)),
                  _zspec(tc, w, gate_blk), _zspec(tl, w, gate_blk),
                  _zspec(tc, LANE, sm_blk), _zspec(tl, LANE, sm_blk),
                  _cspec((SUBLANE, 3 * w)), _cspec((SUBLANE, LANE)), _cspec((1, w)), _cspec((w, w))],
        out_specs=out_specs,
        out_shape=out_shape,
        scratch_shapes=[pltpu.VMEM((2, tc + tl, w), F32)] + [pltpu.VMEM((2, tc + tl, w), BF16)] * 4 + [
            pltpu.VMEM((nc_c + nc_l, LANE), F32), pltpu.VMEM((tc + tl, w), F32),
            pltpu.VMEM((2 * N_PAIRS, LANE, LANE), F32)],
        compiler_params=_MIXER_PARAMS,
        name="gdn_mixer",
    )(zc, zl, zc, zl, zc, zl,
      jnp.pad(conv_w, ((0, SUBLANE - CONV_K), (0, 0))), prow, norm_w[None, :], head_block_ones())


GLA_LEVELS = (32, 16, 8, 4, 2, 1)


def _gla_level_tables(d):
    n_lv = len(GLA_LEVELS)
    i, t = _iota2(n_lv * CHUNK, CHUNK)
    sel = jnp.zeros((n_lv * CHUNK, CHUNK), F32)
    r, c = _pair_iota()
    masks = []
    for x, s in enumerate(GLA_LEVELS):
        row = i - x * CHUNK
        bound = 2 * s * (row // (2 * s)) + s - 1 + d
        sel = jnp.where((i // CHUNK == x) & (t == bound), 1.0, sel)
        same = (r // (2 * s)) == (c // (2 * s))
        r_hi, c_hi = (r % (2 * s)) >= s, (c % (2 * s)) >= s
        masks.append(same & (r_hi & ~c_hi if d == 0 else ~r_hi & c_hi))
    return sel, masks


def _hgrn_body(nc_c, nc_l, q_c, q_l, ff_c, ff_l, fb_c, fb_l, v_c, v_l, gate_c, gate_l, lbrow, normw, bd,
               out_c, out_l, att_s, qd_s, kd_s, v_s, gl_s, o_s, st_s):
    bdm = bd[...]
    ii, jj = _iota2(CHUNK, CHUNK)
    incl_lower = (jj <= ii).astype(F32)
    pi, pj = _pair_iota()
    eye = pi == pj
    tables = [_gla_level_tables(d) for d in range(2)]
    dp = [(d, p) for d in range(2) for p in range(N_PAIRS)]
    n_tot = nc_c + nc_l

    def prep(q_ref, f_refs, v_ref, n_chunks, base):
        def prog(n):
            rin = _chunk_rows(n)
            rows = pl.ds(pl.multiple_of(base + n * CHUNK, CHUNK), CHUNK)
            zq = q_ref[rin, :]
            q = zq * _sigmoid(zq)
            v_s[rows, :] = v_ref[rin, :].astype(BF16)
            for d in range(2):
                zf = f_refs[d][rin, :]
                log1m_lb, one_m_lb = lbrow[2 * d:2 * d + 1, :], lbrow[2 * d + 1:2 * d + 2, :]
                e = jnp.exp(-jnp.abs(zf))
                r = 1.0 / (1.0 + e)
                sig, sig_neg = jnp.where(zf >= 0, r, e * r), jnp.where(zf >= 0, e * r, r)
                log_sig = jnp.minimum(zf, 0.0) - jnp.log(1.0 + e)
                lf = jnp.maximum(jnp.log((1.0 - one_m_lb) + one_m_lb * sig), log1m_lb + log_sig)
                k = one_m_lb * sig_neg
                prefix = _dot_exact_lhs(incl_lower, lf)
                diag = _dot_exact_rhs(q * k, bdm)
                yield
                total = jnp.sum(lf, axis=0, keepdims=True)
                g = prefix if d == 0 else total - prefix + lf
                sel, masks = tables[d]
                c_all = _dot_exact_lhs(sel, g)
                yield
                acc = [jnp.where(eye, t, 0.0) for t in _pairs(diag)]
                for x in range(len(GLA_LEVELS)):
                    c = c_all[x * CHUNK:(x + 1) * CHUNK, :]
                    qt = _pairs((q * jnp.exp(jnp.minimum(g - c, 0.0))).astype(BF16))
                    kt = _pairs((k * jnp.exp(jnp.minimum(c - g, 0.0))).astype(BF16))
                    lvl = [lax.dot_general(qt[p], _pair_blocks(kt[p]), (((1,), (1,)), ((), ())),
                                           preferred_element_type=F32) for p in range(N_PAIRS)]
                    acc = [a + jnp.where(masks[x], t, 0.0) for a, t in zip(acc, lvl)]
                    yield
                att_s[d, rows, :] = jnp.concatenate(acc, axis=1).astype(BF16)
                qd_s[d, rows, :] = (q * jnp.exp(g)).astype(BF16)
                kd_s[d, rows, :] = (k * jnp.exp(total - g)).astype(BF16)
                gl_s[d, pl.ds(base // CHUNK + n, 1), :] = jnp.exp(total)
        _chunk_loop(prog, n_chunks)

    prep(q_c, (ff_c, fb_c), v_c, nc_c, 0)
    prep(q_l, (ff_l, fb_l), v_l, nc_l, nc_c * CHUNK)

    st_s[...] = jnp.zeros_like(st_s)
    o_s[...] = jnp.zeros_like(o_s)
    block = _head_block_mask()

    def scan_body(trip, carry):
        state = [st_s[x] for x in range(len(dp))]
        steps = [trip * SCAN_UNROLL + u for u in range(SCAN_UNROLL)]
        n_dirs = [(step, _backward_chunk(step, nc_c, n_tot)) for step in steps]
        pre = []
        for n_dir in n_dirs:
            rows = [_chunk_rows(n) for n in n_dir]
            per_chain = []
            for x, (d, p) in enumerate(dp):
                cols = slice(p * LANE, (p + 1) * LANE)
                att, qd, kd = att_s[d, rows[d], cols], qd_s[d, rows[d], cols], kd_s[d, rows[d], cols]
                v = v_s[rows[d], cols]
                intra = jnp.dot(att, _pair_blocks(v), preferred_element_type=F32)
                delta = jnp.where(block, _bdot_tn(v, kd), 0.0)
                egl = gl_s[d, pl.ds(n_dir[d], 1), :][:, cols]
                per_chain.append((rows[d], cols, qd, intra, delta, egl))
            pre.append(per_chain)
        for per_chain in pre:
            for x, (rows_d, cols, qd, intra, delta, egl) in enumerate(per_chain):
                o = intra + lax.dot_general(qd, state[x].astype(BF16), (((1,), (1,)), ((), ())),
                                            preferred_element_type=F32)
                state[x] = egl * state[x] + delta
                o_s[rows_d, cols] = o_s[rows_d, cols] + o
        for x in range(len(dp)):
            st_s[x] = state[x]
        return carry
    assert n_tot % SCAN_UNROLL == 0
    lax.fori_loop(0, n_tot // SCAN_UNROLL, scan_body, 0)

    _head_norm_gate(o_s, gate_c, out_c, normw, bdm, nc_c, 0)
    _head_norm_gate(o_s, gate_l, out_l, normw, bdm, nc_l, nc_c * CHUNK)


def hgrn_mixer(zc, zl, lb, norm_w):
    b, tc, _ = zc.shape
    tl = zl.shape[1]
    nc_c, nc_l = tc // CHUNK, tl // CHUNK
    w = BRANCH_W
    lbrow = jnp.zeros((SUBLANE, w), F32)
    for d in range(2):
        lbrow = lbrow.at[2 * d].set(jnp.log1p(-lb[d])).at[2 * d + 1].set(1.0 - lb[d])
    in_specs, args = [], []
    for blk in HGRN_BLOCKS:
        in_specs += [_zspec(tc, w, blk), _zspec(tl, w, blk, pipeline_mode=pl.Buffered(1))]
        args += [zc, zl]
    out_specs, out_shape = _mixer_out(b, tc, tl)
    return pl.pallas_call(
        functools.partial(_hgrn_body, nc_c, nc_l),
        grid=(b,),
        in_specs=in_specs + [_cspec((SUBLANE, w)), _cspec((1, w)), _cspec((w, w))],
        out_specs=out_specs,
        out_shape=out_shape,
        scratch_shapes=[pltpu.VMEM((2, tc + tl, w), BF16)] * 3 + [
            pltpu.VMEM((tc + tl, w), BF16), pltpu.VMEM((2, nc_c + nc_l, w), F32), pltpu.VMEM((tc + tl, w), F32),
            pltpu.VMEM((2 * N_PAIRS, LANE, LANE), F32)],
        compiler_params=_MIXER_PARAMS,
        name="hgrn_mixer",
    )(*args, lbrow, norm_w[None, :], head_block_ones())


def _ssd_body(nc_c, nc_l, xbc_c, xbc_l, z_c, z_l, sm_c, sm_l, convw, convb, prow, dskip, normw,
              out_c, out_l, att_s, v_s, qd_s, kd_s, x_s, gl_s, o_s, st_s):
    w = BRANCH_W
    gw = SSD_GROUPS * SSD_STATE
    lane = lax.broadcasted_iota(jnp.int32, (CHUNK, LANE), 1)
    ii, jj = _iota2(CHUNK, CHUNK)
    incl_lower = (jj <= ii).astype(F32)
    masks = [_pair_order_masks(d) for d in range(2)]
    dp = [(d, p) for d in range(2) for p in range(N_PAIRS)]
    n_tot = nc_c + nc_l
    assert N_PAIRS == SSD_GROUPS

    def prep(x_ref, s_ref, n_chunks, base):
        def prog(n):
            rows = pl.ds(pl.multiple_of(base + n * CHUNK, CHUNK), CHUNK)
            y = _conv(x_ref, convw, n, n_chunks) + convb[...]
            y = y * _sigmoid(y)
            xs, bs, cs = y[:, :w], y[:, w:w + gw], y[:, w + gw:]
            x_s[rows, :] = xs
            sm = s_ref[_chunk_rows(n), :]
            dt_all = _softplus(sm + prow[1:2, :])
            in_dt = (lane >= DT_LANE0) & (lane < DT_LANE0 + 2 * N_HEADS)
            la = jnp.where(in_dt, -jnp.exp(prow[0:1, :]) * dt_all, 0.0)
            prefix = _dot_exact_lhs(incl_lower, la)
            total = jnp.sum(la, axis=0, keepdims=True)
            g_all = jnp.where(lane < DT_LANE0 + N_HEADS, prefix, total - prefix + la)
            g_t = g_all.T
            bg = [bs[:, g * SSD_STATE:(g + 1) * SSD_STATE] for g in range(SSD_GROUPS)]
            cg = [cs[:, g * SSD_STATE:(g + 1) * SSD_STATE] for g in range(SSD_GROUPS)]
            cb = [_bdot_nt(cg[g], jnp.concatenate([bg[g], bg[g]], axis=0)) for g in range(SSD_GROUPS)]
            yield
            for d, p in dp:
                l = DT_LANE0 + d * N_HEADS + 2 * p
                cols = slice(p * LANE, (p + 1) * LANE)
                g_col = _pair_cols(g_all, l)
                g_row = jnp.concatenate([g_t[l:l + 1, :], g_t[l + 1:l + 2, :]], axis=1)
                decay = jnp.exp(jnp.where(masks[d][0], g_col - g_row, NEG_BIG))
                att_s[d, rows, cols] = (cb[p] * decay).astype(BF16)
                v_s[d, rows, cols] = (xs[:, cols] * _pair_cols(dt_all, l)).astype(BF16)
                qd_s[d, rows, cols] = (jnp.concatenate([cg[p], cg[p]], axis=1) * jnp.exp(g_col)).astype(BF16)
                kd_s[d, rows, cols] = (jnp.concatenate([bg[p], bg[p]], axis=1)
                                       * jnp.exp(_pair_cols(total, l) - g_col)).astype(BF16)
            gl_s[pl.ds(base // CHUNK + n, 1), :] = jnp.exp(total)
            yield
        _chunk_loop(prog, n_chunks)

    prep(xbc_c, sm_c, nc_c, 0)
    prep(xbc_l, sm_l, nc_l, nc_c * CHUNK)

    st_s[...] = jnp.zeros_like(st_s)
    o_s[...] = jnp.zeros_like(o_s)
    block = _head_block_mask()
    first_rows = lax.broadcasted_iota(jnp.int32, (LANE, 1), 0) < SSD_STATE

    def scan_body(trip, carry):
        state = [st_s[x] for x in range(len(dp))]
        steps = [trip * SCAN_UNROLL + u for u in range(SCAN_UNROLL)]
        n_dirs = [(step, _backward_chunk(step, nc_c, n_tot)) for step in steps]
        pre = []
        for n_dir in n_dirs:
            rows = [_chunk_rows(n) for n in n_dir]
            egl_rows = [gl_s[pl.ds(n, 1), :] for n in n_dir]
            per_chain = []
            for x, (d, p) in enumerate(dp):
                cols = slice(p * LANE, (p + 1) * LANE)
                l = DT_LANE0 + d * N_HEADS + 2 * p
                att, v = att_s[d, rows[d], cols], v_s[d, rows[d], cols]
                qd, kd = qd_s[d, rows[d], cols], kd_s[d, rows[d], cols]
                intra = jnp.dot(att, _pair_blocks(v), preferred_element_type=F32)
                delta = jnp.where(block, _bdot_tn(kd, v), 0.0)
                egl = jnp.where(first_rows, egl_rows[d][:, l:l + 1], egl_rows[d][:, l + 1:l + 2])
                per_chain.append((rows[d], cols, qd, intra, delta, egl))
            pre.append(per_chain)
        for per_chain in pre:
            for x, (rows_d, cols, qd, intra, delta, egl) in enumerate(per_chain):
                o = intra + jnp.dot(qd, state[x].astype(BF16), preferred_element_type=F32)
                state[x] = egl * state[x] + delta
                o_s[rows_d, cols] = o_s[rows_d, cols] + o
        for x in range(len(dp)):
            st_s[x] = state[x]
        return carry
    assert n_tot % SCAN_UNROLL == 0
    lax.fori_loop(0, n_tot // SCAN_UNROLL, scan_body, 0)

    def finish(z_ref, out_ref, n_chunks, base):
        def body(n, carry):
            rows_in = pl.ds(pl.multiple_of(base + n * MXU_ROWS, MXU_ROWS), MXU_ROWS)
            rows = pl.ds(pl.multiple_of(n * MXU_ROWS, MXU_ROWS), MXU_ROWS)
            zt = z_ref[rows, :]
            y = (o_s[rows_in, :] + dskip[...] * x_s[rows_in, :]) * (zt * _sigmoid(zt))
            ms = jnp.sum(y * y, axis=1, keepdims=True) * (1.0 / w)
            out_ref[rows, :] = y * lax.rsqrt(ms + NORM_EPS) * normw[...]
            return carry
        lax.fori_loop(0, n_chunks * CHUNK // MXU_ROWS, body, 0)

    finish(z_c, out_c, nc_c, 0)
    finish(z_l, out_l, nc_l, nc_c * CHUNK)


def ssd_mixer(zc, zl, conv_w, conv_b, a_log, dt_bias, d_skip, norm_w):
    b, tc, _ = zc.shape
    tl = zl.shape[1]
    nc_c, nc_l = tc // CHUNK, tl // CHUNK
    w = BRANCH_W
    xbc_blk, z_blk, sm_blk = SSD_BLOCKS
    prow = jnp.zeros((SUBLANE, LANE), F32)
    prow = prow.at[0, DT_LANE0:DT_LANE0 + 2 * N_HEADS].set(a_log.reshape(-1))
    prow = prow.at[1, DT_LANE0:DT_LANE0 + 2 * N_HEADS].set(dt_bias.reshape(-1))
    out_specs, out_shape = _mixer_out(b, tc, tl)
    return pl.pallas_call(
        functools.partial(_ssd_body, nc_c, nc_l),
        grid=(b,),
        in_specs=[_zspec(tc, XBC_W, xbc_blk), _zspec(tl, XBC_W, xbc_blk, pipeline_mode=pl.Buffered(1)),
                  _zspec(tc, w, z_blk), _zspec(tl, w, z_blk),
                  _zspec(tc, LANE, sm_blk), _zspec(tl, LANE, sm_blk),
                  _cspec((SUBLANE, XBC_W)), _cspec((1, XBC_W)), _cspec((SUBLANE, LANE)), _cspec((1, w)), _cspec((1, w))],
        out_specs=out_specs,
        out_shape=out_shape,
        scratch_shapes=[pltpu.VMEM((2, tc + tl, w), BF16)] * 4 + [
            pltpu.VMEM((tc + tl, w), F32), pltpu.VMEM((nc_c + nc_l, LANE), F32), pltpu.VMEM((tc + tl, w), F32),
            pltpu.VMEM((2 * N_PAIRS, LANE, LANE), F32)],
        compiler_params=_MIXER_PARAMS,
        name="ssd_mixer",
    )(zc, zl, zc, zl, zc, zl,
      jnp.pad(conv_w, ((0, SUBLANE - CONV_K), (0, 0))), conv_b[None, :], prow,
      jnp.repeat(d_skip, HEAD_DIM)[None, :], norm_w[None, :])


def _fnet_body(t, u0_ref, u1_ref, ceh, cel, coh, col, seh, sel, soh, sol, gch_ref, gcl_ref, gsh_ref, gsl_ref,
               out_ref, p_s, q_s):
    d = lambda a, b: jnp.dot(a, b, preferred_element_type=F32)
    half = t // 2
    tile = min(MXU_ROWS, half)

    def channel_dft(n, carry):
        rows = pl.ds(pl.multiple_of(n * tile, tile), tile)
        for parity in range(2):
            src = pl.ds(2 * n * tile + parity, tile, stride=2)
            uh, ul = _split(jnp.concatenate([u0_ref[src, :], u1_ref[src, :]], axis=1), 2)
            p = d(uh, gch_ref[...]) + d(ul, gch_ref[...]) + d(uh, gcl_ref[...])
            q = d(uh, gsh_ref[...]) + d(ul, gsh_ref[...]) + d(uh, gsl_ref[...])
            p_s[parity, 0, rows, :], p_s[parity, 1, rows, :] = _split(p, 2)
            q_s[parity, 0, rows, :], q_s[parity, 1, rows, :] = _split(q, 2)
        return carry
    lax.fori_loop(0, half // tile, channel_dft, 0)

    def sequence_dft(n, carry):
        rows = pl.ds(pl.multiple_of(n * tile, tile), tile)
        parts = []
        for parity, (ch, cl, sh, sl) in enumerate(((ceh, cel, seh, sel), (coh, col, soh, sol))):
            ph, plo, qh, qlo = p_s[parity, 0], p_s[parity, 1], q_s[parity, 0], q_s[parity, 1]
            re = d(ch[rows, :], ph) + d(cl[rows, :], ph) + d(ch[rows, :], plo)
            im = d(sh[rows, :], qh) + d(sl[rows, :], qh) + d(sh[rows, :], qlo)
            parts.append(re - im)
        out_ref[rows, :] = parts[0] + parts[1]
        out_ref[pl.ds(pl.multiple_of(half + n * tile, tile), tile), :] = parts[0] - parts[1]
        return carry
    lax.fori_loop(0, half // tile, sequence_dft, 0)


def _hi_lo(tab):
    hi = tab.astype(BF16)
    return [hi, (tab - hi.astype(F32)).astype(BF16)]


def _dft_tables(n, scale):
    j = lax.broadcasted_iota(jnp.int32, (n, n), 0)
    k = lax.broadcasted_iota(jnp.int32, (n, n), 1)
    ang = ((j * k) % n).astype(F32) * (2.0 * math.pi / n)
    return _hi_lo(jnp.cos(ang) * scale) + _hi_lo(jnp.sin(ang) * scale)


def _half_dft_tables(t):
    j = lax.broadcasted_iota(jnp.int32, (t // 2, t // 2), 0)
    m = lax.broadcasted_iota(jnp.int32, (t // 2, t // 2), 1)
    out = {}
    for parity in range(2):
        ang = ((j * (2 * m + parity)) % t).astype(F32) * (2.0 * math.pi / t)
        out["c", parity], out["s", parity] = _hi_lo(jnp.cos(ang) * t ** -0.5), _hi_lo(jnp.sin(ang) * t ** -0.5)
    return out["c", 0] + out["c", 1] + out["s", 0] + out["s", 1]


def fnet_tables(t):
    grp = _dft_tables(FOURIER_GROUP_W, FOURIER_GROUP_W ** -0.5)
    n_grp = BRANCH_W // FOURIER_GROUP_W
    grp = [jnp.kron(jnp.eye(n_grp, dtype=F32), g.astype(F32)).astype(BF16) for g in grp]
    return _half_dft_tables(t) + grp


def fnet_mixer(z, tables):
    b, t, _ = z.shape
    w = BRANCH_W
    const = functools.partial(_cspec, pipeline_mode=pl.Buffered(1))
    first = FNET_BLOCK * w // LANE
    return pl.pallas_call(
        functools.partial(_fnet_body, t),
        grid=(b,),
        in_specs=[_zspec(t, LANE, first), _zspec(t, LANE, first + 1)] + [const((t // 2, t // 2))] * 8 + [const((w, w))] * 4,
        out_specs=pl.BlockSpec((None, t, w), lambda i: (i, 0, 0)),
        out_shape=jax.ShapeDtypeStruct((b, t, w), F32),
        scratch_shapes=[pltpu.VMEM((2, 2, t // 2, w), BF16)] * 2,
        compiler_params=_MIXER_PARAMS,
        name="fnet_mixer",
    )(z, z, *tables)


def _ln_rows(x):
    mu = jnp.mean(x, axis=-1, keepdims=True)
    xc = x - mu
    return xc * lax.rsqrt(jnp.mean(xc * xc, axis=-1, keepdims=True) + NORM_EPS)


def _row_spec(width):
    return pl.BlockSpec((MXU_ROWS, width), lambda i: (i, 0))


def _layer_spec(shape, l):
    return pl.BlockSpec((None,) + shape, lambda *_: (l,) + (0,) * len(shape), pipeline_mode=pl.Buffered(1))


def _mod_spec(l, j, row):
    return pl.BlockSpec((None, None, None, 1, D_MODEL), lambda *idx: (l, row(*idx), j, 0, 0))


def _stream_row(t, fixed_row):
    return (lambda i: fixed_row) if fixed_row is not None else (lambda i: i // (t // MXU_ROWS))


def _adaln_body(c_ref, w_ref, b_ref, o_ref):
    cc = c_ref[...]
    act = (cc * _sigmoid(cc)).astype(BF16)
    o_ref[...] = jnp.dot(act, w_ref[...].astype(BF16), preferred_element_type=F32) + b_ref[...]


def adaln_modulation(cond, ada_w, ada_b):
    r, d = cond.shape
    n_layers, _, n = ada_w.shape
    tn = n // 4
    return pl.pallas_call(
        _adaln_body,
        grid=(n_layers, n // tn),
        in_specs=[pl.BlockSpec((r, d), lambda l, j: (0, 0)), pl.BlockSpec((None, d, tn), lambda l, j: (l, 0, j)),
                  pl.BlockSpec((None, 1, tn), lambda l, j: (l, 0, j))],
        out_specs=pl.BlockSpec((None, r, tn), lambda l, j: (l, 0, j)),
        out_shape=jax.ShapeDtypeStruct((n_layers, r, n), F32),
        compiler_params=pltpu.CompilerParams(dimension_semantics=("parallel", "parallel"), vmem_limit_bytes=VMEM_LIMIT),
        name="adaln",
    )(cond, ada_w, ada_b[:, None, :])


def _modproj_body(x_ref, sh_ref, sc_ref, w_ref, z_ref, h_ref):
    h = (_ln_rows(x_ref[...]) * (1.0 + sc_ref[...]) + sh_ref[...]).astype(BF16)
    h_ref[...] = h
    z_ref[...] = jnp.dot(h, w_ref[...], preferred_element_type=F32)


def modulated_project(x, mod, w, l, t, fixed_row):
    m, d = x.shape
    n = w.shape[-1]
    row = _stream_row(t, fixed_row)
    return pl.pallas_call(
        _modproj_body,
        grid=(m // MXU_ROWS,),
        in_specs=[_row_spec(d), _mod_spec(l, 0, row), _mod_spec(l, 1, row), _layer_spec((d, n), l)],
        out_specs=[_row_spec(n), _row_spec(d)],
        out_shape=[jax.ShapeDtypeStruct((m, n), F32), jax.ShapeDtypeStruct((m, d), BF16)],
        compiler_params=_MIXER_PARAMS,
        name="modulated_project",
    )(x, mod, mod, w)


def _merge_body(h_ref, oa_ref, ob_ref, oc_ref, od_ref, x_ref, gate1_ref, sh2_ref, sc2_ref, lng_ref, lnb_ref,
                wg_ref, bg_ref, wb_ref, wo_ref, wr_ref, xo_ref, h2_ref, aff_ref):
    h = h_ref[...]
    acc = None
    for g, o_ref in enumerate((oa_ref, ob_ref, oc_ref, od_ref)):
        gate = _sigmoid(jnp.dot(h, wg_ref[g], preferred_element_type=F32) + bg_ref[g])
        t = gate * jnp.dot(o_ref[...].astype(BF16), wb_ref[g], preferred_element_type=F32)
        acc = t if acc is None else acc + t
    y = jnp.dot(acc.astype(BF16), wo_ref[...], preferred_element_type=F32)
    x_new = _ln_rows(DEEPNORM_ALPHA * x_ref[...] + gate1_ref[...] * y) * lng_ref[...] + lnb_ref[...]
    xo_ref[...] = x_new
    h2 = (_ln_rows(x_new) * (1.0 + sc2_ref[...]) + sh2_ref[...]).astype(BF16)
    h2_ref[...] = h2
    logits = jnp.dot(h2, wr_ref[...], preferred_element_type=F32)
    valid = lax.broadcasted_iota(jnp.int32, logits.shape, 1) < N_EXPERTS
    logits = jnp.where(valid, logits, NEG_BIG)
    e = jnp.exp(logits - jnp.max(logits, axis=-1, keepdims=True))
    aff_ref[...] = e / jnp.sum(e, axis=-1, keepdims=True)


def merge_and_norm(h, outs, x, mod, p, l, t, fixed_row):
    m, d = x.shape
    w = outs[0].shape[1]
    row = _stream_row(t, fixed_row)
    stacked = lambda name: _layer_spec(p[name].shape[1:], l)
    return pl.pallas_call(
        _merge_body,
        grid=(m // MXU_ROWS,),
        in_specs=[_row_spec(d)] + [_row_spec(w)] * N_BRANCH + [_row_spec(d)] + [_mod_spec(l, j, row) for j in (2, 3, 4)]
        + [stacked(name) for name in ("ln1_g", "ln1_b", "w_gate", "b_gate", "w_branch", "w_out", "w_router")],
        out_specs=[_row_spec(d), _row_spec(d), _row_spec(LANE)],
        out_shape=[jax.ShapeDtypeStruct((m, d), F32), jax.ShapeDtypeStruct((m, d), BF16),
                   jax.ShapeDtypeStruct((m, LANE), F32)],
        compiler_params=_MIXER_PARAMS,
        name="merge_and_norm",
    )(h, *outs, x, mod, mod, mod, p["ln1_g"], p["ln1_b"], p["w_gate"], p["b_gate"], p["w_branch"], p["w_out"], p["w_router"])


def _expert_body(x_ref, wt_ref, wg_ref, wu_ref, wd_ref, y_ref, wg_s, wu_s, wd_s):
    @pl.when(pl.program_id(1) == 0)
    def _():
        wg_s[...] = wg_ref[...].astype(BF16)
        wu_s[...] = wu_ref[...].astype(BF16)
        wd_s[...] = wd_ref[...].astype(BF16)

    bb, cap, d = x_ref.shape
    x = x_ref[...].reshape(bb * cap, d)
    gate = jnp.dot(x, wg_s[...], preferred_element_type=F32)
    up = jnp.dot(x, wu_s[...], preferred_element_type=F32)
    hid = (gate * _sigmoid(gate) * up).astype(BF16)
    y = jnp.dot(hid, wd_s[...], preferred_element_type=F32) * wt_ref[...].reshape(bb * cap, 1)
    y_ref[...] = y.astype(BF16).reshape(y_ref.shape)


def expert_swiglu(xe, weight, w_ff_gate, w_ff_up, w_ff_down, l):
    b, e, cap, d = xe.shape
    f = w_ff_gate.shape[-1]
    bb = min(b, max(1, MXU_ROWS // cap))
    x_spec = pl.BlockSpec((bb, None, cap, d), lambda ei, bi: (bi, ei, 0, 0))
    wt_spec = pl.BlockSpec((bb, None, cap, 1), lambda ei, bi: (bi, ei, 0, 0))
    w_spec = lambda shape: pl.BlockSpec((None, None) + shape, lambda ei, bi: (l, ei, 0, 0))
    return pl.pallas_call(
        _expert_body,
        grid=(e, b // bb),
        in_specs=[x_spec, wt_spec, w_spec((d, f)), w_spec((d, f)), w_spec((f, d))],
        out_specs=x_spec,
        out_shape=jax.ShapeDtypeStruct(xe.shape, BF16),
        scratch_shapes=[pltpu.VMEM((d, f), BF16), pltpu.VMEM((d, f), BF16), pltpu.VMEM((f, d), BF16)],
        compiler_params=pltpu.CompilerParams(dimension_semantics=("parallel", "arbitrary"),
                                             vmem_limit_bytes=VMEM_LIMIT),
        name="expert_swiglu",
    )(xe, weight[..., None], w_ff_gate, w_ff_up, w_ff_down)


def _combine_body(n_slots, idx_ref, y_ref, x_ref, gate2_ref, lng_ref, lnb_ref, o_ref):
    tile = pl.program_id(1)
    token = tile * MXU_ROWS + lax.broadcasted_iota(jnp.int32, (MXU_ROWS, n_slots), 0)
    onehot = jnp.where(idx_ref[...] == token, 1.0, 0.0).astype(BF16)
    moe = jnp.dot(onehot, y_ref[...], preferred_element_type=F32)
    o_ref[...] = _ln_rows(DEEPNORM_ALPHA * x_ref[...] + gate2_ref[...] * moe) * lng_ref[...] + lnb_ref[...]


def combine_and_norm(idx, y, x, mod, ln_g, ln_b, l, fixed_row):
    b, t, d = x.shape
    n_slots = idx.shape[-1]
    per_sample = lambda shape: pl.BlockSpec((None,) + shape, lambda i, j: (i, 0, 0))
    tile = pl.BlockSpec((None, MXU_ROWS, d), lambda i, j: (i, j, 0))
    vec = _layer_spec((1, d), l)
    row = (lambda i, j: fixed_row) if fixed_row is not None else (lambda i, j: i)
    return pl.pallas_call(
        functools.partial(_combine_body, n_slots),
        grid=(b, t // MXU_ROWS),
        in_specs=[per_sample((1, n_slots)), per_sample((n_slots, d)), tile,
                  _mod_spec(l, 5, row), vec, vec],
        out_specs=tile,
        out_shape=jax.ShapeDtypeStruct((b, t, d), F32),
        compiler_params=pltpu.CompilerParams(dimension_semantics=("parallel", "arbitrary"),
                                             vmem_limit_bytes=VMEM_LIMIT),
        name="combine_and_norm",
    )(idx, y, x, mod, ln_g, ln_b)


def sincos_grid(rows, cols, dim):
    quarter = dim // 4
    omega = 1.0 / (10000.0 ** (jnp.arange(quarter, dtype=F32) / quarter))
    er = jnp.arange(rows, dtype=F32)[:, None] * omega
    ec = jnp.arange(cols, dtype=F32)[:, None] * omega
    er = jnp.concatenate([jnp.sin(er), jnp.cos(er)], axis=-1)
    ec = jnp.concatenate([jnp.sin(ec), jnp.cos(ec)], axis=-1)
    emb = jnp.concatenate([jnp.broadcast_to(er[:, None, :], (rows, cols, dim // 2)),
                           jnp.broadcast_to(ec[None, :, :], (rows, cols, dim // 2))], axis=-1)
    return emb.reshape(rows * cols, dim)


def hgrn_lower_bounds(logits):
    cum = jnp.cumsum(jax.nn.softmax(logits.astype(F32), axis=1), axis=1)
    return cum - cum[:, :1]


def _flat(a):
    return a.reshape(-1, a.shape[-1])


def token_mixer(xc, xl, mod, p, fnet_tabs, b, l, with_ctx):
    tc, tl = xc.shape[0] // b, xl.shape[0] // b
    zc, hc = modulated_project(xc, mod, p["w_in"], l, tc, b)
    zl, hl = modulated_project(xl, mod, p["w_in"], l, tl, None)
    zc, zl = zc.reshape(b, tc, Z_COLS), zl.reshape(b, tl, Z_COLS)
    a_c, a_l = hgrn_mixer(zc, zl, p["lb"][:, l], p["hgrn_norm_w"][l])
    b_c, b_l = gdn_mixer(zc, zl, p["gdn_conv_w"][l], p["gdn_a_log"][l], p["gdn_dt_bias"][l], p["gdn_norm_w"][l])
    d_c, d_l = ssd_mixer(zc, zl, p["ssd_conv_w"][l], p["ssd_conv_b"][l], p["ssd_a_log"][l], p["ssd_dt_bias"][l],
                         p["ssd_d"][l], p["ssd_norm_w"][l])

    def merged(h, outs, x, t, fixed_row):
        return merge_and_norm(h, tuple(_flat(o) for o in outs), x, mod, p, l, t, fixed_row)

    out_l = merged(hl, (a_l, b_l, fnet_mixer(zl, fnet_tabs[1]), d_l), xl, tl, None)
    out_c = merged(hc, (a_c, b_c, fnet_mixer(zc, fnet_tabs[0]), d_c), xc, tc, b) if with_ctx else None
    return out_c, out_l


def channel_mixer(x_mid, h, aff, b, mod, p, l, fixed_row):
    t_ = h.shape[0] // b
    d = h.shape[-1]
    h = h.reshape(b, t_, d)
    cap = EC_CAPACITY_FACTOR * t_ // N_EXPERTS
    aff = aff[:, :N_EXPERTS].reshape(b, t_, N_EXPERTS)
    weight, idx = lax.top_k(jnp.swapaxes(aff, 1, 2), cap)
    xe = h[jnp.arange(b)[:, None, None], idx]
    y = expert_swiglu(xe, weight, p["w_ff_gate"], p["w_ff_up"], p["w_ff_down"], l)
    n_slots = N_EXPERTS * cap
    x_new = combine_and_norm(idx.reshape(b, 1, n_slots), y.reshape(b, n_slots, d),
                             x_mid.reshape(b, t_, d), mod, p["ln2_g"], p["ln2_b"], l, fixed_row)
    return _flat(x_new)


def kernel(x, c, ctx, c_ctx, ada_w, ada_b, w_in, hgrn_lb_logits, hgrn_norm_w,
           gdn_conv_w, gdn_a_log, gdn_dt_bias, gdn_norm_w,
           ssd_conv_w, ssd_conv_b, ssd_a_log, ssd_dt_bias, ssd_d, ssd_norm_w,
           w_gate, b_gate, w_branch, w_out, ln1_g, ln1_b,
           w_router, w_ff_gate, w_ff_up, w_ff_down, ln2_g, ln2_b):
    b, n_lat, d = x.shape
    n_ctx = ctx.shape[1]
    rows = n_lat // GRID_W
    xl = _flat(x + sincos_grid(rows, GRID_W, D_MODEL).astype(x.dtype))
    xc = _flat(ctx)
    fnet_tabs = (fnet_tables(n_ctx), fnet_tables(n_lat))
    cond = jnp.concatenate([c, c_ctx[None, :], jnp.zeros((-(b + 1) % SUBLANE, d), c.dtype)], axis=0)
    mod = adaln_modulation(cond, ada_w, ada_b).reshape(DEPTH, cond.shape[0], 6, 1, d)
    p = {
        "w_in": permute_w_in(w_in).astype(BF16), "lb": hgrn_lower_bounds(hgrn_lb_logits), "hgrn_norm_w": hgrn_norm_w,
        "gdn_conv_w": gdn_conv_w, "gdn_a_log": gdn_a_log, "gdn_dt_bias": gdn_dt_bias, "gdn_norm_w": gdn_norm_w,
        "ssd_conv_w": ssd_conv_w, "ssd_conv_b": ssd_conv_b, "ssd_a_log": ssd_a_log, "ssd_dt_bias": ssd_dt_bias,
        "ssd_d": ssd_d, "ssd_norm_w": ssd_norm_w,
        "w_gate": w_gate.astype(BF16), "b_gate": b_gate[:, :, None, :], "w_branch": w_branch.astype(BF16),
        "w_out": w_out.astype(BF16), "ln1_g": ln1_g[:, None, :], "ln1_b": ln1_b[:, None, :],
        "w_router": jnp.pad(w_router, ((0, 0), (0, 0), (0, LANE - N_EXPERTS))).astype(BF16),
        "w_ff_gate": w_ff_gate, "w_ff_up": w_ff_up, "w_ff_down": w_ff_down,
        "ln2_g": ln2_g[:, None, :], "ln2_b": ln2_b[:, None, :],
    }
    for l in range(DEPTH):
        with_ctx = l < DEPTH - 1
        out_c, out_l = token_mixer(xc, xl, mod, p, fnet_tabs, b, l, with_ctx)
        xl = channel_mixer(*out_l, b, mod, p, l, None)
        if with_ctx:
            xc = channel_mixer(*out_c, b, mod, p, l, b)
    return xl.reshape(b, n_lat, d)
```
